```python
import jax
import jax.numpy as jnp
from jax import lax
import numpy as np

D_MODEL = 1024
BATCH = 4
SEQ = 8192
DEPTH = 1
DEC_BATCH = 128
DEC_SEQ = 8
PAST_LEN = 8192
PAGE_SIZE = 128

ATTN_WIDTH = D_MODEL // 2
CONV_CH = D_MODEL - ATTN_WIDTH
HEAD_DIM = 64
N_HEADS = ATTN_WIDTH // HEAD_DIM
N_KV_HEADS = 2
KV_GROUP = N_HEADS // N_KV_HEADS
IDX_HEADS = 4
IDX_DIM = 64
INDEX_TOPK = 256
Q_BLOCK = 128
ROPE_THETA = 500000.0
CONV_WIDTH = 31
N_EXPERTS = 32
TOP_K = 4
D_FF = D_MODEL
SWIGLU_LIMIT = 7.0
SWIGLU_ALPHA = 1.702
MOE_BLOCK = 128
PLE_DIM = 256
LN_EPS = 1e-5
DN_ALPHA = (2 * DEPTH) ** 0.25
DN_BETA = (8 * DEPTH) ** -0.25
ATTN_SCALE = HEAD_DIM ** -0.5
IDX_W_SCALE = (IDX_HEADS * IDX_DIM) ** -0.5
Q_W = N_HEADS * HEAD_DIM
KV_W = N_KV_HEADS * HEAD_DIM
QI_W = IDX_HEADS * IDX_DIM
IN_WIDTH = Q_W + 2 * KV_W + QI_W + IDX_DIM + IDX_HEADS + 2 * CONV_CH

kernel_name = 'hymba_dsa_conformer_moe_step'


def layer_norm(x, g, b):
    xf = x.astype(jnp.float32)
    mu = jnp.mean(xf, axis=-1, keepdims=True)
    var = jnp.mean(jnp.square(xf - mu), axis=-1, keepdims=True)
    return ((xf - mu) * lax.rsqrt(var + LN_EPS) * g + b).astype(x.dtype)


def rope_partial(x, pos):
    rd = x.shape[-1] // 4
    half = rd // 2
    inv_freq = ROPE_THETA ** (-jnp.arange(half, dtype=jnp.float32) * 2.0 / rd)
    ang = pos.astype(jnp.float32)[:, None] * inv_freq[None, :]
    ang = ang.reshape((1, pos.shape[0]) + (1,) * (x.ndim - 3) + (half,))
    cos, sin = jnp.cos(ang), jnp.sin(ang)
    x1 = x[..., :half].astype(jnp.float32)
    x2 = x[..., half:rd].astype(jnp.float32)
    rot = jnp.concatenate([x1 * cos - x2 * sin, x1 * sin + x2 * cos], axis=-1).astype(x.dtype)
    return jnp.concatenate([rot, x[..., rd:]], axis=-1)


def dsa_attention(q, k_all, v_all, qi, ki_all, wi, q_pos):
    B, Tq = q.shape[0], q.shape[1]
    L = k_all.shape[1]
    n_sel = min(INDEX_TOPK, L // 4)
    qb = min(Q_BLOCK, Tq)
    nb = -(-Tq // qb)
    pad = nb * qb - Tq

    def blocks(a):
        a = jnp.pad(a, [(0, 0), (0, pad)] + [(0, 0)] * (a.ndim - 2))
        return jnp.swapaxes(a.reshape((B, nb, qb) + a.shape[2:]), 0, 1)

    qpos_b = jnp.pad(q_pos, (0, pad), mode='edge').reshape(nb, qb)
    key_pos = jnp.arange(L, dtype=jnp.int32)
    ki_f = ki_all.astype(jnp.float32)
    gather = jax.vmap(lambda a, i: a[i])

    def attend_block(args):
        q_b, qi_b, wi_b, qp = args
        s = jnp.einsum('bqhd,bkd->bqhk', qi_b.astype(jnp.float32), ki_f)
        score = jnp.einsum('bqh,bqhk->bqk', wi_b.astype(jnp.float32), jax.nn.relu(s))
        admissible = key_pos[None, :] <= qp[:, None]
        score = jnp.where(admissible[None], score, -jnp.inf)
        _, idx = lax.top_k(score, n_sel)
        valid = idx <= qp[None, :, None]
        k_sel = gather(k_all, idx)
        v_sel = gather(v_all, idx)
        qg = q_b.reshape(B, qb, N_KV_HEADS, KV_GROUP, HEAD_DIM)
        logits = jnp.einsum('bqgrd,bqngd->bqgrn', qg, k_sel).astype(jnp.float32) * ATTN_SCALE
        logits = jnp.where(valid[:, :, None, None, :], logits, -jnp.inf)
        probs = jax.nn.softmax(logits, axis=-1).astype(v_all.dtype)
        o = jnp.einsum('bqgrn,bqngd->bqgrd', probs, v_sel)
        return o.reshape(B, qb, N_HEADS * HEAD_DIM)

    out = lax.map(attend_block, (blocks(q), blocks(qi), blocks(wi), qpos_b))
    return jnp.swapaxes(out, 0, 1).reshape(B, nb * qb, -1)[:, :Tq]


def conformer_conv(u, buf, conv_w, conv_b, ln_g, ln_b):
    xp = jnp.concatenate([buf, u], axis=1)
    y = lax.conv_general_dilated(xp, conv_w[:, None, :], window_strides=(1,), padding='VALID',
                                 dimension_numbers=('NWC', 'WIO', 'NWC'),
                                 feature_group_count=CONV_CH) + conv_b
    y = layer_norm(y, ln_g, ln_b)
    y = y * jax.nn.sigmoid(y)
    return y, xp[:, -(CONV_WIDTH - 1):]


def moe(h, w_router, b_router, w_gate_up, b_gate_up, w_down, b_down):
    shp = h.shape
    xt = h.reshape(-1, shp[-1])
    N = xt.shape[0]
    logits = (xt @ w_router + b_router).astype(jnp.float32)
    top_l, top_i = lax.top_k(logits, TOP_K)
    gates = jax.nn.softmax(top_l, axis=-1).astype(h.dtype)
    NK = N * TOP_K
    e_flat = top_i.reshape(-1)
    order = jnp.argsort(e_flat)
    e_sorted = e_flat[order]
    tok_sorted = order // TOP_K
    counts = jnp.bincount(e_flat, length=N_EXPERTS)
    padded = (counts + MOE_BLOCK - 1) // MOE_BLOCK * MOE_BLOCK
    start = jnp.cumsum(counts) - counts
    pend = jnp.cumsum(padded)
    pstart = pend - padded
    slot = pstart[e_sorted] + jnp.arange(NK, dtype=jnp.int32) - start[e_sorted]
    n_blocks = -(-(NK + N_EXPERTS * (MOE_BLOCK - 1)) // MOE_BLOCK)
    P = n_blocks * MOE_BLOCK
    slot_tok = jnp.full((P,), N, dtype=jnp.int32).at[slot].set(tok_sorted.astype(jnp.int32))
    x_pad = jnp.concatenate([xt, jnp.zeros((1, shp[-1]), xt.dtype)], axis=0)[slot_tok]
    x_pad = x_pad.reshape(n_blocks, MOE_BLOCK, shp[-1])
    block_e = jnp.clip(jnp.searchsorted(pend, jnp.arange(n_blocks) * MOE_BLOCK, side='right'), 0, N_EXPERTS - 1)

    def expert_block(args):
        xb, e = args
        gu = xb @ w_gate_up[e] + b_gate_up[e]
        gate = jnp.minimum(gu[:, :D_FF], SWIGLU_LIMIT)
        up = jnp.clip(gu[:, D_FF:], -SWIGLU_LIMIT, SWIGLU_LIMIT)
        act = (up + 1.0) * gate * jax.nn.sigmoid(SWIGLU_ALPHA * gate)
        return act @ w_down[e] + b_down[e]

    y_pad = lax.map(expert_block, (x_pad, block_e)).reshape(P, shp[-1])
    slot_orig = jnp.zeros((NK,), jnp.int32).at[order].set(slot.astype(jnp.int32))
    y = jnp.einsum('nk,nkd->nd', gates, y_pad[slot_orig].reshape(N, TOP_K, shp[-1]))
    return y.reshape(shp)


def decoder_layer(x, pe, k_past, v_past, ki_past, conv_buf,
                  w_in, b_in, w_o, b_o, ln1_g, ln1_b, conv_w, conv_b, conv_ln_g, conv_ln_b,
                  w_router, b_router, w_gate_up, b_gate_up, w_down, b_down, ln2_g, ln2_b,
                  w_ple_gate, b_ple_gate, w_ple_proj, ln3_g, ln3_b):
    B, T = x.shape[0], x.shape[1]
    past = k_past.shape[1]
    pos = past + jnp.arange(T, dtype=jnp.int32)
    z = x @ w_in + b_in
    o1 = Q_W
    o2 = o1 + KV_W
    o3 = o2 + KV_W
    o4 = o3 + QI_W
    o5 = o4 + IDX_DIM
    o6 = o5 + IDX_HEADS
    q = rope_partial(z[..., :o1].reshape(B, T, N_HEADS, HEAD_DIM), pos)
    k = rope_partial(z[..., o1:o2].reshape(B, T, N_KV_HEADS, HEAD_DIM), pos)
    v = z[..., o2:o3].reshape(B, T, N_KV_HEADS, HEAD_DIM)
    qi = rope_partial(z[..., o3:o4].reshape(B, T, IDX_HEADS, IDX_DIM), pos)
    ki = rope_partial(z[..., o4:o5], pos)
    wi = z[..., o5:o6] * IDX_W_SCALE
    glu_in = z[..., o6:]
    u = glu_in[..., :CONV_CH] * jax.nn.sigmoid(glu_in[..., CONV_CH:])
    attn = dsa_attention(q, jnp.concatenate([k_past, k], axis=1), jnp.concatenate([v_past, v], axis=1),
                         qi, jnp.concatenate([ki_past, ki], axis=1), wi, pos)
    conv, new_buf = conformer_conv(u, conv_buf, conv_w, conv_b, conv_ln_g, conv_ln_b)
    mixed = jnp.concatenate([attn, conv], axis=-1) @ w_o + b_o
    h = layer_norm(DN_ALPHA * x + mixed, ln1_g, ln1_b)
    h2 = layer_norm(DN_ALPHA * h + moe(h, w_router, b_router, w_gate_up, b_gate_up, w_down, b_down), ln2_g, ln2_b)
    gate = jax.nn.sigmoid(h2 @ w_ple_gate + b_ple_gate)
    y = layer_norm(DN_ALPHA * h2 + gate * (pe @ w_ple_proj), ln3_g, ln3_b)
    return y, k, v, ki, new_buf


def setup_inputs(seed: int = 0) -> dict:
    key = jax.random.key(seed)
    ks = jax.random.split(key, 40)
    f32 = jnp.float32

    def nrm(k, shape, scale):
        return scale * jax.random.normal(k, shape, f32)

    n_pages = PAST_LEN // PAGE_SIZE
    n_pool = (DEC_BATCH * n_pages * 5) // 4
    perm = jax.random.permutation(ks[0], n_pool).astype(jnp.int32)
    page_table = perm[:DEC_BATCH * n_pages].reshape(DEC_BATCH, n_pages)
    dm = D_MODEL ** -0.5
    return {
        'x_prompt': nrm(ks[1], (BATCH, SEQ, D_MODEL), 1.0),
        'x_sample': nrm(ks[2], (DEC_BATCH, DEC_SEQ, D_MODEL), 1.0),
        'p_prompt': nrm(ks[3], (DEPTH, BATCH, SEQ, PLE_DIM), 1.0),
        'p_sample': nrm(ks[4], (DEPTH, DEC_BATCH, DEC_SEQ, PLE_DIM), 1.0),
        'cache_k': nrm(ks[5], (DEPTH, n_pool, PAGE_SIZE, N_KV_HEADS, HEAD_DIM), 1.0),
        'cache_v': nrm(ks[6], (DEPTH, n_pool, PAGE_SIZE, N_KV_HEADS, HEAD_DIM), 1.0),
        'cache_kidx': nrm(ks[7], (DEPTH, n_pool, PAGE_SIZE, IDX_DIM), 1.0),
        'state_conv': nrm(ks[8], (DEPTH, DEC_BATCH, CONV_WIDTH - 1, CONV_CH), 0.5),
        'page_table': page_table,
        'w_in': nrm(ks[9], (DEPTH, D_MODEL, IN_WIDTH), dm),
        'b_in': nrm(ks[10], (DEPTH, IN_WIDTH), 0.02),
        'w_o': nrm(ks[11], (DEPTH, ATTN_WIDTH + CONV_CH, D_MODEL), dm * DN_BETA),
        'b_o': nrm(ks[12], (DEPTH, D_MODEL), 0.02),
        'ln1_g': 1.0 + nrm(ks[13], (DEPTH, D_MODEL), 0.02),
        'ln1_b': nrm(ks[14], (DEPTH, D_MODEL), 0.02),
        'conv_w': nrm(ks[15], (DEPTH, CONV_WIDTH, CONV_CH), CONV_WIDTH ** -0.5),
        'conv_b': nrm(ks[16], (DEPTH, CONV_CH), 0.02),
        'conv_ln_g': 1.0 + nrm(ks[17], (DEPTH, CONV_CH), 0.02),
        'conv_ln_b': nrm(ks[18], (DEPTH, CONV_CH), 0.02),
        'w_router': nrm(ks[19], (DEPTH, D_MODEL, N_EXPERTS), dm),
        'b_router': nrm(ks[20], (DEPTH, N_EXPERTS), 0.01),
        'w_gate_up': nrm(ks[21], (DEPTH, N_EXPERTS, D_MODEL, 2 * D_FF), dm),
        'b_gate_up': nrm(ks[22], (DEPTH, N_EXPERTS, 2 * D_FF), 0.02),
        'w_down': nrm(ks[23], (DEPTH, N_EXPERTS, D_FF, D_MODEL), (D_FF ** -0.5) * DN_BETA),
        'b_down': nrm(ks[24], (DEPTH, N_EXPERTS, D_MODEL), 0.02),
        'ln2_g': 1.0 + nrm(ks[25], (DEPTH, D_MODEL), 0.02),
        'ln2_b': nrm(ks[26], (DEPTH, D_MODEL), 0.02),
        'w_ple_gate': nrm(ks[27], (DEPTH, D_MODEL, D_MODEL), dm),
        'b_ple_gate': nrm(ks[28], (DEPTH, D_MODEL), 0.02),
        'w_ple_proj': nrm(ks[29], (DEPTH, PLE_DIM, D_MODEL), (PLE_DIM ** -0.5) * DN_BETA),
        'ln3_g': 1.0 + nrm(ks[30], (DEPTH, D_MODEL), 0.02),
        'ln3_b': nrm(ks[31], (DEPTH, D_MODEL), 0.02),
    }


def reference(x_prompt, x_sample, p_prompt, p_sample, cache_k, cache_v, cache_kidx, state_conv, page_table,
              w_in, b_in, w_o, b_o, ln1_g, ln1_b, conv_w, conv_b, conv_ln_g, conv_ln_b,
              w_router, b_router, w_gate_up, b_gate_up, w_down, b_down, ln2_g, ln2_b,
              w_ple_gate, b_ple_gate, w_ple_proj, ln3_g, ln3_b):
    n_pages = page_table.shape[1]

    def paged_rows(pool):
        rows = pool[page_table]
        return rows.reshape((rows.shape[0], n_pages * rows.shape[2]) + rows.shape[3:])

    B = x_prompt.shape[0]
    yp, ys = x_prompt, x_sample
    kp_l, vp_l, kip_l, cp_l = [], [], [], []
    ks_l, vs_l, kis_l, cs_l = [], [], [], []
    for i in range(DEPTH):
        lw = (w_in[i], b_in[i], w_o[i], b_o[i], ln1_g[i], ln1_b[i], conv_w[i], conv_b[i],
              conv_ln_g[i], conv_ln_b[i], w_router[i], b_router[i], w_gate_up[i], b_gate_up[i],
              w_down[i], b_down[i], ln2_g[i], ln2_b[i], w_ple_gate[i], b_ple_gate[i],
              w_ple_proj[i], ln3_g[i], ln3_b[i])
        empty_kv = jnp.zeros((B, 0, N_KV_HEADS, HEAD_DIM), x_prompt.dtype)
        empty_ki = jnp.zeros((B, 0, IDX_DIM), x_prompt.dtype)
        fresh_buf = jnp.zeros((B, CONV_WIDTH - 1, CONV_CH), x_prompt.dtype)
        yp, kp, vp, kip, cp = decoder_layer(yp, p_prompt[i], empty_kv, empty_kv, empty_ki, fresh_buf, *lw)
        ys, kss, vss, kis, cs = decoder_layer(ys, p_sample[i], paged_rows(cache_k[i]), paged_rows(cache_v[i]),
                                              paged_rows(cache_kidx[i]), state_conv[i], *lw)
        kp_l.append(kp)
        vp_l.append(vp)
        kip_l.append(kip)
        cp_l.append(cp)
        ks_l.append(kss)
        vs_l.append(vss)
        kis_l.append(kis)
        cs_l.append(cs)
    return (yp, ys, jnp.stack(kp_l), jnp.stack(vp_l), jnp.stack(kip_l), jnp.stack(cp_l),
            jnp.stack(ks_l), jnp.stack(vs_l), jnp.stack(kis_l), jnp.stack(cs_l))
```

```python
import functools
import math

import jax
import jax.numpy as jnp
from jax import lax
from jax.experimental import pallas as pl
from jax.experimental.pallas import tpu as pltpu

F32 = jnp.float32
BF16 = jnp.bfloat16
I32 = jnp.int32

HEAD_DIM = 64
N_HEADS = 8
N_KV_HEADS = 2
KV_GROUP = N_HEADS // N_KV_HEADS
IDX_HEADS = 4
IDX_DIM = 64
INDEX_TOPK = 256
ROPE_THETA = 500000.0
ROPE_HALF = HEAD_DIM // 8
CONV_WIDTH = 31
N_EXPERTS = 32
TOP_K = 4
SWIGLU_LIMIT = 7.0
SWIGLU_ALPHA = 1.702
LN_EPS = 1e-5
DEPTH = 1
DN_ALPHA = (2 * DEPTH) ** 0.25
ATTN_SCALE = HEAD_DIM ** -0.5
IDX_W_SCALE = (IDX_HEADS * IDX_DIM) ** -0.5
LOG2E = math.log2(math.e)

LANES = 128
SUBLANES = 8
VMEM_LIMIT = 56 * 1024 * 1024
NEG_BIG = -1e30
INT_MIN = -(2 ** 31)

Q_W = N_HEADS * HEAD_DIM
KV_W = N_KV_HEADS * HEAD_DIM
QI_W = IDX_HEADS * IDX_DIM


def _cparams(sem):
    return pltpu.CompilerParams(dimension_semantics=sem, vmem_limit_bytes=VMEM_LIMIT)


def _rope_slab(z, cos, sin):
    lane = lax.broadcasted_iota(I32, (1, LANES), 1)
    first = (lane % HEAD_DIM) < ROPE_HALF
    up = pltpu.roll(z, LANES - ROPE_HALF, axis=1)
    dn = pltpu.roll(z, ROPE_HALF, axis=1)
    return z * cos + jnp.where(first, up, dn) * sin


def _inproj_kernel(x_ref, w_ref, b_ref, cos_ref, sin_ref,
                   q_ref, kf_ref, vf_ref, kb_ref, vt_ref, qi_ref, kiwi_ref, kib_ref, u_ref, *, conv_ch):
    x = x_ref[...].astype(BF16)
    cos = cos_ref[...]
    sin = sin_ref[...]

    def proj(lo, width):
        return jnp.dot(x, w_ref[:, lo:lo + width], preferred_element_type=F32) + b_ref[:, lo:lo + width]

    def rope(z):
        n = z.shape[1] // LANES
        slabs = [_rope_slab(z[:, s * LANES:(s + 1) * LANES], cos, sin) for s in range(n)]
        return slabs[0] if n == 1 else jnp.concatenate(slabs, axis=1)

    o = 0
    q = rope(proj(o, Q_W))
    q_ref[...] = (q * (ATTN_SCALE * LOG2E)).astype(BF16)
    o += Q_W
    k = rope(proj(o, KV_W))
    kf_ref[...] = k
    kb_ref[...] = k.astype(BF16)
    o += KV_W
    v = proj(o, KV_W)
    vf_ref[...] = v
    vt_ref[0] = v.T.astype(BF16)
    o += KV_W
    qi_ref[...] = rope(proj(o, QI_W)).astype(BF16)
    o += QI_W
    lane = lax.broadcasted_iota(I32, (1, LANES), 1)
    is_ki = lane < IDX_DIM
    kiwi = _rope_slab(proj(o, LANES), jnp.where(is_ki, cos, 1.0), jnp.where(is_ki, sin, 0.0))
    kiwi = kiwi * jnp.where((lane >= IDX_DIM) & (lane < IDX_DIM + IDX_HEADS), IDX_W_SCALE, 1.0)
    kiwi_ref[...] = kiwi
    kib_ref[...] = kiwi[:, :IDX_DIM].astype(BF16)
    o += LANES
    a = proj(o, conv_ch)
    g = proj(o + conv_ch, conv_ch)
    u_ref[...] = a * jax.nn.sigmoid(g)


def _inproj_call(x, w_pad, b_pad, cos_tab, sin_tab, tab_index, tm):
    n, d = x.shape
    nw = w_pad.shape[1]
    conv_ch = (nw - (Q_W + 2 * KV_W + QI_W + LANES)) // 2
    nt = n // tm
    row = lambda i: (i, 0)
    const = lambda i: (0, 0)
    out_shapes = (
        jax.ShapeDtypeStruct((n, Q_W), BF16),
        jax.ShapeDtypeStruct((n, KV_W), F32),
        jax.ShapeDtypeStruct((n, KV_W), F32),
        jax.ShapeDtypeStruct((n, KV_W), BF16),
        jax.ShapeDtypeStruct((nt, KV_W, tm), BF16),
        jax.ShapeDtypeStruct((n, QI_W), BF16),
        jax.ShapeDtypeStruct((n, LANES), F32),
        jax.ShapeDtypeStruct((n, IDX_DIM), BF16),
        jax.ShapeDtypeStruct((n, conv_ch), F32),
    )
    out_specs = (
        pl.BlockSpec((tm, Q_W), row),
        pl.BlockSpec((tm, KV_W), row),
        pl.BlockSpec((tm, KV_W), row),
        pl.BlockSpec((tm, KV_W), row),
        pl.BlockSpec((1, KV_W, tm), lambda i: (i, 0, 0)),
        pl.BlockSpec((tm, QI_W), row),
        pl.BlockSpec((tm, LANES), row),
        pl.BlockSpec((tm, IDX_DIM), row),
        pl.BlockSpec((tm, conv_ch), row),
    )
    return pl.pallas_call(
        functools.partial(_inproj_kernel, conv_ch=conv_ch),
        grid=(nt,),
        in_specs=[
            pl.BlockSpec((tm, d), row),
            pl.BlockSpec((d, nw), const),
            pl.BlockSpec((1, nw), const),
            pl.BlockSpec((tm, LANES), lambda i: (tab_index(i), 0)),
            pl.BlockSpec((tm, LANES), lambda i: (tab_index(i), 0)),
        ],
        out_specs=out_specs,
        out_shape=out_shapes,
        compiler_params=_cparams(("parallel",)),
    )(x, w_pad, b_pad, cos_tab, sin_tab)


def _sortable(score):
    bits = lax.bitcast_convert_type(score, I32)
    key = bits ^ ((bits >> 31) & 0x7FFFFFFF)
    return jnp.where(score == 0.0, 0, key)


def _kth_largest(count_ge, shape, n_sel):
    def bit_body(t, thr):
        cand = thr + jnp.left_shift(jnp.int32(1), 31 - t)
        return jnp.where(count_ge(cand) >= n_sel, cand, thr)

    return lax.fori_loop(0, 32, bit_body, jnp.full(shape, INT_MIN, I32))


def _tie_bound(count_eq_below, need, shape, idx_bits):
    def bit_body(t, j0):
        cand = j0 + jnp.left_shift(jnp.int32(1), idx_bits - 1 - t)
        return jnp.where(count_eq_below(cand) < need, cand, j0)

    return lax.fori_loop(0, idx_bits, bit_body, jnp.zeros(shape, I32))


NO_BOUND = 2 ** 30


def _prompt_attn_kernel(q_ref, qi_ref, kiwi_ref, kib_ref, kb_ref, vt_ref, o_ref,
                        qbd_ref, qit_ref, keys_ref, bias_ref, j0_ref, m_ref, l_ref, acc_ref,
                        *, qb, kc, n_sel, idx_bits):
    i = pl.program_id(1)
    n_chunks = (i * qb + qb + kc - 1) // kc
    q_pos = i * qb + lax.broadcasted_iota(I32, (1, qb), 1)

    def key_pos(k0):
        return k0 + lax.broadcasted_iota(I32, (kc, 1), 0)

    qt = q_ref[...].astype(F32).T
    zero = jnp.zeros((HEAD_DIM, qb), F32)
    for j in range(KV_GROUP):
        a = qt[j * HEAD_DIM:(j + 1) * HEAD_DIM]
        b = qt[(KV_GROUP + j) * HEAD_DIM:(KV_GROUP + j + 1) * HEAD_DIM]
        top = jnp.concatenate([a, zero], axis=1)
        bot = jnp.concatenate([zero, b], axis=1)
        qbd_ref[j] = jnp.concatenate([top, bot], axis=0).astype(BF16)
    qit = qi_ref[...].astype(F32).T
    qit_ref[...] = jnp.concatenate(
        [qit[h * IDX_DIM:(h + 1) * IDX_DIM] for h in range(IDX_HEADS)], axis=1).astype(BF16)
    wit = kiwi_ref[...].T[IDX_DIM:IDX_DIM + SUBLANES]

    def score_body(c, carry):
        k0 = pl.multiple_of(c * kc, kc)
        s = jnp.dot(kib_ref[pl.ds(k0, kc), :], qit_ref[...], preferred_element_type=F32)
        score = jnp.zeros((kc, qb), F32)
        for h in range(IDX_HEADS):
            score = score + wit[h:h + 1, :] * jnp.maximum(s[:, h * qb:(h + 1) * qb], 0.0)
        score = jnp.where(key_pos(k0) <= q_pos, score, -jnp.inf)
        keys_ref[pl.ds(k0, kc), :] = _sortable(score)
        return carry

    lax.fori_loop(0, n_chunks, score_body, 0)

    def count(pred):
        def body(c, acc):
            k0 = pl.multiple_of(c * kc, kc)
            ones = jnp.where(pred(keys_ref[pl.ds(k0, kc), :], k0), 1, 0).astype(I32)
            return acc + jnp.sum(ones.reshape(kc // SUBLANES, SUBLANES, qb), axis=0)

        acc = lax.fori_loop(0, n_chunks, body, jnp.zeros((SUBLANES, qb), I32))
        return jnp.sum(acc, axis=0, keepdims=True)

    thr = _kth_largest(lambda cand: count(lambda blk, k0: blk >= cand), (1, qb), n_sel)
    need = n_sel - count(lambda blk, k0: blk > thr)
    tie = count(lambda blk, k0: blk >= thr) > n_sel

    j0_ref[...] = jnp.full((1, qb), NO_BOUND, I32)

    @pl.when(jnp.max(jnp.where(tie, 1, 0)) > 0)
    def _():
        j0 = _tie_bound(lambda cand: count(lambda blk, k0: (blk == thr) & (key_pos(k0) < cand)),
                        need, (1, qb), idx_bits)
        j0_ref[...] = jnp.where(tie, j0, NO_BOUND)

    j0 = j0_ref[...]

    def bias_body(c, carry):
        k0 = pl.multiple_of(c * kc, kc)
        blk = keys_ref[pl.ds(k0, kc), :]
        kp = key_pos(k0)
        sel = ((blk > thr) | ((blk == thr) & (kp <= j0))) & (kp <= q_pos)
        bias_ref[pl.ds(k0, kc), :] = jnp.where(sel, 0.0, NEG_BIG)
        return carry

    lax.fori_loop(0, n_chunks, bias_body, 0)

    m_ref[...] = jnp.full(m_ref.shape, NEG_BIG, F32)
    l_ref[...] = jnp.zeros(l_ref.shape, F32)
    acc_ref[...] = jnp.zeros(acc_ref.shape, F32)

    def attn_body(c, carry):
        k0 = pl.multiple_of(c * kc, kc)
        kblk = kb_ref[pl.ds(k0, kc), :]
        vt = vt_ref[c]
        bias = bias_ref[pl.ds(k0, kc), :]
        bias2 = jnp.concatenate([bias, bias], axis=1)
        for j in range(KV_GROUP):
            lg = jnp.dot(kblk, qbd_ref[j], preferred_element_type=F32) + bias2
            m_old = m_ref[j]
            m_new = jnp.maximum(m_old, jnp.max(lg, axis=0, keepdims=True))
            alpha = jnp.exp2(m_old - m_new)
            p = jnp.exp2(lg - m_new)
            l_ref[j] = alpha * l_ref[j] + jnp.sum(p, axis=0, keepdims=True)
            acc_ref[j] = alpha * acc_ref[j] + jnp.dot(vt, p.astype(BF16), preferred_element_type=F32)
            m_ref[j] = m_new
        return carry

    lax.fori_loop(0, n_chunks, attn_body, 0)

    heads = [None] * N_HEADS
    for j in range(KV_GROUP):
        o = acc_ref[j] * (1.0 / l_ref[j])
        heads[j] = o[:HEAD_DIM, :qb]
        heads[KV_GROUP + j] = o[HEAD_DIM:, qb:]
    o_ref[...] = jnp.concatenate(heads, axis=0).T.astype(o_ref.dtype)


def _prompt_attn_call(q, qi, kiwi, kib, kb, vt, nb, t, qb, kc):
    nqb = t // qb
    n_sel = min(INDEX_TOPK, t // 4)
    blk = lambda b, i: (b * nqb + i, 0)
    seq = lambda b, i: (b, 0)
    kern = functools.partial(_prompt_attn_kernel, qb=qb, kc=kc, n_sel=n_sel, idx_bits=max(1, (t - 1).bit_length()))
    return pl.pallas_call(
        kern,
        grid=(nb, nqb),
        in_specs=[
            pl.BlockSpec((qb, Q_W), blk),
            pl.BlockSpec((qb, QI_W), blk),
            pl.BlockSpec((qb, LANES), blk),
            pl.BlockSpec((t, IDX_DIM), seq),
            pl.BlockSpec((t, KV_W), seq),
            pl.BlockSpec((t // kc, KV_W, kc), lambda b, i: (b, 0, 0)),
        ],
        out_specs=pl.BlockSpec((qb, Q_W), blk),
        out_shape=jax.ShapeDtypeStruct((nb * t, Q_W), BF16),
        scratch_shapes=[
            pltpu.VMEM((KV_GROUP, 2 * HEAD_DIM, 2 * qb), BF16),
            pltpu.VMEM((IDX_DIM, IDX_HEADS * qb), BF16),
            pltpu.VMEM((t, qb), I32),
            pltpu.VMEM((t, qb), F32),
            pltpu.VMEM((1, qb), I32),
            pltpu.VMEM((KV_GROUP, 1, 2 * qb), F32),
            pltpu.VMEM((KV_GROUP, 1, 2 * qb), F32),
            pltpu.VMEM((KV_GROUP, 2 * HEAD_DIM, 2 * qb), F32),
        ],
        compiler_params=_cparams(("parallel", "arbitrary")),
    )(q, qi, kiwi, kib, kb, vt)


def _sample_attn_kernel(q_ref, qi_ref, kiwi_ref, kn_ref, vn_ref, kip_ref, kp_ref, vp_ref, o_ref,
                        kin_s, knew_s, vnew_s, keys_ref, bias_ref, lg_ref, j0_ref,
                        *, past, kc, dq, n_sel, idx_bits):
    lp = past + LANES
    q = q_ref[0]
    qi = qi_ref[0]
    kiwi = kiwi_ref[0]
    qrow = lax.broadcasted_iota(I32, (dq, 1), 0)
    idx = lax.broadcasted_iota(I32, (1, lp), 1)
    rel = idx - past
    adm = (idx < past) | ((rel <= qrow) & (rel < dq))

    kin_s[...] = jnp.zeros(kin_s.shape, F32)
    knew_s[...] = jnp.zeros(knew_s.shape, F32)
    vnew_s[...] = jnp.zeros(vnew_s.shape, F32)
    kin_s[0:dq, :] = kiwi[:, :IDX_DIM]
    knew_s[0:dq, :] = kn_ref[0]
    vnew_s[0:dq, :] = vn_ref[0]

    nt = (((1,), (1,)), ((), ()))
    qi_rows = jnp.concatenate([qi[:, h * IDX_DIM:(h + 1) * IDX_DIM] for h in range(IDX_HEADS)], axis=0).astype(BF16)
    wi_col = jnp.concatenate([kiwi[:, IDX_DIM + h:IDX_DIM + h + 1] for h in range(IDX_HEADS)], axis=0)

    def scores(ki_bf):
        s = lax.dot_general(qi_rows, ki_bf, nt, preferred_element_type=F32)
        r = jnp.maximum(s, 0.0) * wi_col
        out = r[0:dq]
        for h in range(1, IDX_HEADS):
            out = out + r[h * dq:(h + 1) * dq]
        return out

    for c in range(past // kc):
        keys_ref[:, c * kc:(c + 1) * kc] = _sortable(scores(kip_ref[0, c * kc:(c + 1) * kc, :].astype(BF16)))
    s_new = scores(kin_s[...].astype(BF16))
    keys_ref[:, past:] = _sortable(jnp.where(adm[:, past:], s_new, -jnp.inf))

    def count(pred):
        return jnp.sum(jnp.where(pred(keys_ref[...]), 1, 0).astype(I32), axis=1, keepdims=True)

    thr = _kth_largest(lambda cand: count(lambda k: k >= cand), (dq, 1), n_sel)
    need = n_sel - count(lambda k: k > thr)
    tie = count(lambda k: k >= thr) > n_sel
    j0_ref[...] = jnp.full((dq, 1), NO_BOUND, I32)

    @pl.when(jnp.max(jnp.where(tie, 1, 0)) > 0)
    def _():
        j0 = _tie_bound(lambda cand: count(lambda k: (k == thr) & (idx < cand)), need, (dq, 1), idx_bits)
        j0_ref[...] = jnp.where(tie, j0, NO_BOUND)

    j0 = j0_ref[...]
    keys = keys_ref[...]
    sel = ((keys > thr) | ((keys == thr) & (idx <= j0))) & adm
    bias_ref[...] = jnp.where(sel, 0.0, NEG_BIG)

    zero = jnp.zeros((dq, HEAD_DIM), F32)
    rows = []
    for g in range(N_KV_HEADS):
        for r in range(KV_GROUP):
            h = g * KV_GROUP + r
            piece = q[:, h * HEAD_DIM:(h + 1) * HEAD_DIM]
            rows.append(jnp.concatenate([piece, zero] if g == 0 else [zero, piece], axis=1))
    qbd = jnp.concatenate(rows, axis=0).astype(BF16)

    for c in range(past // kc):
        lg_ref[:, c * kc:(c + 1) * kc] = lax.dot_general(
            qbd, kp_ref[0, c * kc:(c + 1) * kc, :].astype(BF16), nt, preferred_element_type=F32)
    lg_ref[:, past:] = lax.dot_general(qbd, knew_s[...].astype(BF16), nt, preferred_element_type=F32)

    bias = bias_ref[...]
    lg = lg_ref[...] + jnp.concatenate([bias] * N_HEADS, axis=0)
    m = jnp.max(lg, axis=1, keepdims=True)
    p = jnp.exp2(lg - m)
    l = jnp.sum(p, axis=1, keepdims=True)
    lg_ref[...] = p
    o = jnp.dot(lg_ref[:, past:].astype(BF16), vnew_s[...].astype(BF16), preferred_element_type=F32)
    for c in range(past // kc):
        o = o + jnp.dot(lg_ref[:, c * kc:(c + 1) * kc].astype(BF16),
                        vp_ref[0, c * kc:(c + 1) * kc, :].astype(BF16), preferred_element_type=F32)
    o_ref[0] = o * (1.0 / l)


def _sample_attn_call(q, qi, kiwi, kn, vn, kip, kp, vp, kc):
    nb, dq, _ = q.shape
    past = kip.shape[1]
    lp = past + LANES
    n_sel = min(INDEX_TOPK, (past + dq) // 4)
    b3 = lambda b: (b, 0, 0)
    kern = functools.partial(_sample_attn_kernel, past=past, kc=kc, dq=dq, n_sel=n_sel,
                             idx_bits=max(1, (lp - 1).bit_length()))
    return pl.pallas_call(
        kern,
        grid=(nb,),
        in_specs=[
            pl.BlockSpec((1, dq, Q_W), b3),
            pl.BlockSpec((1, dq, QI_W), b3),
            pl.BlockSpec((1, dq, LANES), b3),
            pl.BlockSpec((1, dq, KV_W), b3),
            pl.BlockSpec((1, dq, KV_W), b3),
            pl.BlockSpec((1, past, IDX_DIM), b3),
            pl.BlockSpec((1, past, KV_W), b3),
            pl.BlockSpec((1, past, KV_W), b3),
        ],
        out_specs=pl.BlockSpec((1, N_HEADS * dq, KV_W), b3),
        out_shape=jax.ShapeDtypeStruct((nb, N_HEADS * dq, KV_W), F32),
        scratch_shapes=[
            pltpu.VMEM((LANES, IDX_DIM), F32),
            pltpu.VMEM((LANES, KV_W), F32),
            pltpu.VMEM((LANES, KV_W), F32),
            pltpu.VMEM((dq, lp), I32),
            pltpu.VMEM((dq, lp), F32),
            pltpu.VMEM((N_HEADS * dq, lp), F32),
            pltpu.VMEM((dq, 1), I32),
        ],
        compiler_params=_cparams(("parallel",)),
    )(q, qi, kiwi, kn, vn, kip, kp, vp)


def _layer_norm(x, g, b):
    mu = jnp.mean(x, axis=-1, keepdims=True)
    xc = x - mu
    var = jnp.mean(xc * xc, axis=-1, keepdims=True)
    return xc * lax.rsqrt(var + LN_EPS) * g + b


HIST_ROWS = 32
HIST_PAD = HIST_ROWS - (CONV_WIDTH - 1)
CONV_ROWS = 64


def _mix_kernel(*refs, tm, nseq, use_prev, aliased):
    (attn_ref, u_ref, uprev_ref, hist_ref, x_ref, cw_ref, cb_ref, cg_ref, cbb_ref, wo_ref, bo_ref,
     g1_ref, b1_ref, wr_ref, br_ref) = refs[:15]
    h_ref, hb_ref, route_ref, xp_ref, conv_ref = refs[15 + aliased:]
    rows = tm // nseq
    ch = u_ref.shape[1]

    if use_prev:
        first = pl.program_id(1) == 0

        @pl.when(first)
        def _():
            xp_ref[:, 0:HIST_ROWS, :] = hist_ref[...]

        @pl.when(jnp.logical_not(first))
        def _():
            xp_ref[0, 0:HIST_ROWS, :] = uprev_ref[...]
    else:
        xp_ref[:, 0:HIST_ROWS, :] = hist_ref[...]
    xp_ref[:, HIST_ROWS:, :] = u_ref[...].reshape(nseq, rows, ch)

    rs = min(rows, CONV_ROWS)

    def seq_body(s, carry):
        for r0 in range(0, rows, rs):
            acc = jnp.zeros((rs, ch), F32)
            for j in range(CONV_WIDTH):
                lo = r0 + HIST_PAD + j
                acc = acc + cw_ref[j:j + 1, :] * xp_ref[s, lo:lo + rs, :]
            conv_ref[pl.ds(pl.multiple_of(s * rows + r0, SUBLANES), rs), :] = acc
        return carry

    if nseq == 1:
        seq_body(0, 0)
    else:
        lax.fori_loop(0, nseq, seq_body, 0)

    y = _layer_norm(conv_ref[...] + cb_ref[...], cg_ref[...], cbb_ref[...])
    conv = (y * jax.nn.sigmoid(y)).astype(BF16)
    aw = attn_ref.shape[1]
    mixed = (jnp.dot(attn_ref[...].astype(BF16), wo_ref[0:aw, :], preferred_element_type=F32)
             + jnp.dot(conv, wo_ref[aw:, :], preferred_element_type=F32) + bo_ref[...])
    h = _layer_norm(DN_ALPHA * x_ref[...] + mixed, g1_ref[...], b1_ref[...])
    h_ref[...] = h
    hb = h.astype(BF16)
    hb_ref[...] = hb

    logits = jnp.dot(hb, wr_ref[...], preferred_element_type=F32) + br_ref[...]
    lane = lax.broadcasted_iota(I32, (1, LANES), 1)
    work = logits
    vals, ids = [], []
    for _ in range(TOP_K):
        mx = jnp.max(work, axis=1, keepdims=True)
        ix = jnp.min(jnp.where(work == mx, lane, LANES), axis=1, keepdims=True)
        vals.append(mx)
        ids.append(ix)
        work = jnp.where(lane == ix, -jnp.inf, work)
    ex = [jnp.exp(v - vals[0]) for v in vals]
    den = ex[0]
    for e in ex[1:]:
        den = den + e
    route = jnp.zeros((tm, LANES), F32)
    for k in range(TOP_K):
        route = jnp.where(lane == k, ex[k] / den, route)
        route = jnp.where(lane == TOP_K + k, ids[k].astype(F32), route)
    route_ref[...] = route


def _mix_call(attn, u, hist, x, weights, *, n_total, row_off, tm, nseq, grid, seq_len, prev_bufs=None):
    cw, cb, cg, cbb, wo, bo, g1, b1, wr, br = weights
    d = x.shape[1]
    ch = u.shape[1]
    use_prev = grid[1] > 1
    aliased = 0 if prev_bufs is None else 3
    tiles_per_seq = grid[1]
    off_t = row_off // tm
    tile = lambda b, i: (b * tiles_per_seq + i, 0)
    otile = lambda b, i: (off_t + b * tiles_per_seq + i, 0)
    const = lambda b, i: (0, 0)
    if use_prev:
        per_tile = tm // HIST_ROWS
        prev = lambda b, i: (jnp.maximum((off_t + b * tiles_per_seq + i) * per_tile - 1, 0), 0)
    else:
        prev = const
    in_specs = [
        pl.BlockSpec((tm, attn.shape[1]), tile),
        pl.BlockSpec((tm, ch), otile),
        pl.BlockSpec((HIST_ROWS, ch), prev),
        pl.BlockSpec((nseq, HIST_ROWS, ch), lambda b, i: (b * tiles_per_seq + i, 0, 0) if not use_prev else (b, 0, 0)),
        pl.BlockSpec((tm, d), tile),
        pl.BlockSpec(cw.shape, const),
        pl.BlockSpec(cb.shape, const),
        pl.BlockSpec(cg.shape, const),
        pl.BlockSpec(cbb.shape, const),
        pl.BlockSpec(wo.shape, const),
        pl.BlockSpec(bo.shape, const),
        pl.BlockSpec(g1.shape, const),
        pl.BlockSpec(b1.shape, const),
        pl.BlockSpec(wr.shape, const),
        pl.BlockSpec(br.shape, const),
    ]
    args = [attn, u, u, hist, x, cw, cb, cg, cbb, wo, bo, g1, b1, wr, br]
    io_alias = {}
    if prev_bufs is not None:
        for k, buf in enumerate(prev_bufs):
            in_specs.append(pl.BlockSpec(memory_space=pl.ANY))
            io_alias[len(args)] = k
            args.append(buf)
    kern = functools.partial(_mix_kernel, tm=tm, nseq=nseq, use_prev=use_prev, aliased=aliased)
    return pl.pallas_call(
        kern,
        grid=grid,
        in_specs=in_specs,
        out_specs=(
            pl.BlockSpec((tm, d), otile),
            pl.BlockSpec((tm, d), otile),
            pl.BlockSpec((tm, LANES), otile),
        ),
        out_shape=(
            jax.ShapeDtypeStruct((n_total, d), F32),
            jax.ShapeDtypeStruct((n_total, d), BF16),
            jax.ShapeDtypeStruct((n_total, LANES), F32),
        ),
        scratch_shapes=[
            pltpu.VMEM((nseq, HIST_ROWS + tm // nseq, ch), F32),
            pltpu.VMEM((tm, ch), F32),
        ],
        input_output_aliases=io_alias,
        compiler_params=_cparams(("parallel", "arbitrary")),
    )(*args)


CAST_ROWS = 128


def _moe_kernel(be_ref, nu_ref, x_ref, wgu_ref, bgu_ref, wd_ref, bd_ref, y_ref, wgu_s, wd_s):
    i = pl.program_id(0)
    used = i < nu_ref[0]
    changed = (i == 0) | (be_ref[i] != be_ref[jnp.maximum(i - 1, 0)])
    dff = wd_s.shape[0]

    @pl.when(used & changed)
    def _():
        def cast_gu(r, carry):
            r0 = pl.multiple_of(r * CAST_ROWS, CAST_ROWS)
            wgu_s[pl.ds(r0, CAST_ROWS), :] = wgu_ref[0, pl.ds(r0, CAST_ROWS), :].astype(BF16)
            return carry

        def cast_d(r, carry):
            r0 = pl.multiple_of(r * CAST_ROWS, CAST_ROWS)
            wd_s[pl.ds(r0, CAST_ROWS), :] = wd_ref[0, pl.ds(r0, CAST_ROWS), :].astype(BF16)
            return carry

        lax.fori_loop(0, wgu_s.shape[0] // CAST_ROWS, cast_gu, 0)
        lax.fori_loop(0, wd_s.shape[0] // CAST_ROWS, cast_d, 0)

    @pl.when(used)
    def _():
        gu = jnp.dot(x_ref[...], wgu_s[...], preferred_element_type=F32) + bgu_ref[0]
        gate = jnp.minimum(gu[:, :dff], SWIGLU_LIMIT)
        up = jnp.clip(gu[:, dff:], -SWIGLU_LIMIT, SWIGLU_LIMIT)
        act = (up + 1.0) * gate * jax.nn.sigmoid(SWIGLU_ALPHA * gate)
        y_ref[...] = jnp.dot(act.astype(BF16), wd_s[...], preferred_element_type=F32) + bd_ref[0]


def _moe_call(block_e, n_used, x_pad, wgu, bgu, wd, bd, bm):
    p, d = x_pad.shape
    ne, _, dgu = wgu.shape
    dff = wd.shape[1]
    n_blocks = p // bm
    grid_spec = pltpu.PrefetchScalarGridSpec(
        num_scalar_prefetch=2,
        grid=(n_blocks,),
        in_specs=[
            pl.BlockSpec((bm, d), lambda i, be, nu: (i, 0)),
            pl.BlockSpec((1, d, dgu), lambda i, be, nu: (be[i], 0, 0)),
            pl.BlockSpec((1, 1, dgu), lambda i, be, nu: (be[i], 0, 0)),
            pl.BlockSpec((1, dff, d), lambda i, be, nu: (be[i], 0, 0)),
            pl.BlockSpec((1, 1, d), lambda i, be, nu: (be[i], 0, 0)),
        ],
        out_specs=pl.BlockSpec((bm, d), lambda i, be, nu: (i, 0)),
        scratch_shapes=[pltpu.VMEM((d, dgu), BF16), pltpu.VMEM((dff, d), BF16)],
    )
    return pl.pallas_call(
        _moe_kernel,
        grid_spec=grid_spec,
        out_shape=jax.ShapeDtypeStruct((p, d), F32),
        compiler_params=_cparams(("arbitrary",)),
    )(block_e, n_used, x_pad, wgu, bgu.reshape(ne, 1, dgu), wd, bd.reshape(ne, 1, d))


def _dispatch_plan(route, bm):
    n = route.shape[0]
    nk = n * TOP_K
    e_flat = route[:, TOP_K:2 * TOP_K].astype(I32).reshape(-1)
    order = jnp.argsort(e_flat, stable=True)
    e_sorted = e_flat[order]
    counts = jnp.bincount(e_flat, length=N_EXPERTS)
    padded = (counts + bm - 1) // bm * bm
    start = jnp.cumsum(counts) - counts
    pend = jnp.cumsum(padded)
    pstart = pend - padded
    slot = (pstart[e_sorted] + jnp.arange(nk, dtype=I32) - start[e_sorted]).astype(I32)
    n_blocks = -(-(nk + N_EXPERTS * (bm - 1)) // bm)
    slot_tok = jnp.zeros((n_blocks * bm,), I32).at[slot].set((order // TOP_K).astype(I32))
    block_e = jnp.clip(jnp.searchsorted(pend, jnp.arange(n_blocks) * bm, side='right'), 0, N_EXPERTS - 1).astype(I32)
    n_used = (pend[-1:] // bm).astype(I32)
    slot_orig = jnp.zeros((nk,), I32).at[order].set(slot)
    return slot_tok, block_e, n_used, slot_orig


def _final_kernel(h_ref, yg_ref, route_ref, pe_ref, g2_ref, b2_ref, wg_ref, bg_ref, wp_ref, g3_ref, b3_ref, o_ref):
    d = h_ref.shape[1]
    route = route_ref[...]
    y = route[:, 0:1] * yg_ref[:, 0:d]
    for k in range(1, TOP_K):
        y = y + route[:, k:k + 1] * yg_ref[:, k * d:(k + 1) * d]
    h2 = _layer_norm(DN_ALPHA * h_ref[...] + y, g2_ref[...], b2_ref[...])
    gate = jax.nn.sigmoid(jnp.dot(h2.astype(BF16), wg_ref[...], preferred_element_type=F32) + bg_ref[...])
    proj = jnp.dot(pe_ref[...].astype(BF16), wp_ref[...], preferred_element_type=F32)
    o_ref[...] = _layer_norm(DN_ALPHA * h2 + gate * proj, g3_ref[...], b3_ref[...])


def _final_call(h, yg, route, pe, weights, row_off, tm):
    g2, b2, wg, bg, wp, g3, b3 = weights
    n, pd = pe.shape
    d = h.shape[1]
    off_t = row_off // tm
    src = lambda i: (off_t + i, 0)
    row = lambda i: (i, 0)
    const = lambda i: (0, 0)
    return pl.pallas_call(
        _final_kernel,
        grid=(n // tm,),
        in_specs=[
            pl.BlockSpec((tm, d), src),
            pl.BlockSpec((tm, TOP_K * d), src),
            pl.BlockSpec((tm, LANES), src),
            pl.BlockSpec((tm, pd), row),
            pl.BlockSpec(g2.shape, const),
            pl.BlockSpec(b2.shape, const),
            pl.BlockSpec(wg.shape, const),
            pl.BlockSpec(bg.shape, const),
            pl.BlockSpec(wp.shape, const),
            pl.BlockSpec(g3.shape, const),
            pl.BlockSpec(b3.shape, const),
        ],
        out_specs=pl.BlockSpec((tm, d), row),
        out_shape=jax.ShapeDtypeStruct((n, d), F32),
        compiler_params=_cparams(("parallel",)),
    )(h, yg, route, pe, g2, b2, wg, bg, wp, g3, b3)


TOKEN_TILE = 512
QUERY_BLOCK = 128
MOE_ROWS = 512


def _rope_tables(pos):
    rd = 2 * ROPE_HALF
    inv_freq = ROPE_THETA ** (-jnp.arange(ROPE_HALF, dtype=F32) * 2.0 / rd)
    ang = pos.astype(F32)[:, None] * inv_freq[None, :]
    cos, sin = jnp.cos(ang), jnp.sin(ang)
    ones = jnp.ones((pos.shape[0], HEAD_DIM - rd), F32)
    cos_h = jnp.concatenate([cos, cos, ones], axis=1)
    sin_h = jnp.concatenate([-sin, sin, 0.0 * ones], axis=1)
    reps = LANES // HEAD_DIM
    return jnp.tile(cos_h, (1, reps)), jnp.tile(sin_h, (1, reps))


def _layer(x_prompt, x_sample, p_prompt, p_sample, cache_k, cache_v, cache_kidx, state_conv, page_table,
           w_in, b_in, w_o, b_o, ln1_g, ln1_b, conv_w, conv_b, conv_ln_g, conv_ln_b,
           w_router, b_router, w_gate_up, b_gate_up, w_down, b_down, ln2_g, ln2_b,
           w_ple_gate, b_ple_gate, w_ple_proj, ln3_g, ln3_b):
    nb, t, d = x_prompt.shape
    db, dq, _ = x_sample.shape
    n_pages, page = page_table.shape[1], cache_k.shape[1]
    past = n_pages * page
    ch = conv_w.shape[1]
    tm = min(TOKEN_TILE, t)
    np_rows, ns_rows = nb * t, db * dq
    n = np_rows + ns_rows
    assert t % tm == 0 and ns_rows % tm == 0 and tm % dq == 0 and dq == SUBLANES
    row2 = lambda a: a.reshape(1, -1)

    n_head = Q_W + 2 * KV_W + QI_W + IDX_DIM + IDX_HEADS
    pad = LANES - IDX_DIM - IDX_HEADS
    w_pad = jnp.concatenate([w_in[:, :n_head], jnp.zeros((d, pad), F32), w_in[:, n_head:]], axis=1).astype(BF16)
    b_pad = jnp.concatenate([b_in[:n_head], jnp.zeros((pad,), F32), b_in[n_head:]]).reshape(1, -1)
    pos = jnp.concatenate([jnp.arange(t, dtype=I32), past + (jnp.arange(tm, dtype=I32) % dq)])
    cos_tab, sin_tab = _rope_tables(pos)
    tiles_p = np_rows // tm
    x_all = jnp.concatenate([x_prompt.reshape(np_rows, d), x_sample.reshape(ns_rows, d)], axis=0)
    tab_index = lambda i: jnp.where(i < tiles_p, i % (t // tm), t // tm)
    q, k_f, v_f, k_b, v_t, qi, kiwi, ki_b, u = _inproj_call(x_all, w_pad, b_pad, cos_tab, sin_tab, tab_index, tm)

    attn_p = _prompt_attn_call(q, qi, kiwi, ki_b, k_b, v_t, nb, t, min(QUERY_BLOCK, t), tm)
    smp = lambda a: a[np_rows:].astype(F32).reshape(db, dq, -1)
    paged = lambda pool: pool[page_table].reshape(db, past, -1)
    o_s = _sample_attn_call(smp(q), smp(qi), smp(kiwi), smp(k_f), smp(v_f),
                            paged(cache_kidx), paged(cache_k), paged(cache_v), min(1024, past))
    o_s = o_s.reshape(db, N_KV_HEADS, KV_GROUP, dq, N_KV_HEADS, HEAD_DIM)
    attn_s = jnp.stack([o_s[:, g, :, :, g, :] for g in range(N_KV_HEADS)], axis=1)
    attn_s = attn_s.transpose(0, 3, 1, 2, 4).reshape(ns_rows, Q_W)

    wr_pad = jnp.concatenate([w_router, jnp.zeros((d, LANES - N_EXPERTS), F32)], axis=1).astype(BF16)
    br_pad = jnp.concatenate([b_router, jnp.full((LANES - N_EXPERTS,), -jnp.inf, F32)]).reshape(1, -1)
    mix_w = (conv_w, row2(conv_b), row2(conv_ln_g), row2(conv_ln_b), w_o.astype(BF16), row2(b_o),
             row2(ln1_g), row2(ln1_b), wr_pad, br_pad)
    hist_p = jnp.zeros((nb, HIST_ROWS, ch), F32)
    hist_s = jnp.concatenate([jnp.zeros((db, HIST_PAD, ch), F32), state_conv], axis=1)
    bufs = _mix_call(attn_p, u, hist_p, x_prompt.reshape(np_rows, d), mix_w, n_total=n, row_off=0, tm=tm,
                     nseq=1, grid=(nb, t // tm), seq_len=t)
    h, h_b, route = _mix_call(attn_s, u, hist_s, x_sample.reshape(ns_rows, d), mix_w, n_total=n, row_off=np_rows,
                              tm=tm, nseq=tm // dq, grid=(ns_rows // tm, 1), seq_len=dq, prev_bufs=bufs)

    slot_tok, block_e, n_used, slot_orig = _dispatch_plan(route, MOE_ROWS)
    y_pad = _moe_call(block_e, n_used, h_b[slot_tok], w_gate_up, b_gate_up, w_down, b_down, MOE_ROWS)
    yg = y_pad[slot_orig].reshape(n, TOP_K * d)

    fin_w = (row2(ln2_g), row2(ln2_b), w_ple_gate.astype(BF16), row2(b_ple_gate), w_ple_proj.astype(BF16),
             row2(ln3_g), row2(ln3_b))
    y_p = _final_call(h, yg, route, p_prompt.reshape(np_rows, -1), fin_w, 0, tm)
    y_s = _final_call(h, yg, route, p_sample.reshape(ns_rows, -1), fin_w, np_rows, tm)

    u_p = u[:np_rows].reshape(nb, t, ch)
    u_s = u[np_rows:].reshape(db, dq, ch)
    keep = CONV_WIDTH - 1
    conv_p = jnp.concatenate([jnp.zeros((nb, keep, ch), F32), u_p], axis=1)[:, -keep:]
    conv_s = jnp.concatenate([state_conv, u_s], axis=1)[:, -keep:]
    kv = lambda a, lo, hi, b_, t_: a[lo:hi].reshape(b_, t_, N_KV_HEADS, HEAD_DIM)
    return (y_p.reshape(nb, t, d), y_s.reshape(db, dq, d),
            kv(k_f, 0, np_rows, nb, t), kv(v_f, 0, np_rows, nb, t),
            kiwi[:np_rows, :IDX_DIM].reshape(nb, t, IDX_DIM), conv_p,
            kv(k_f, np_rows, n, db, dq), kv(v_f, np_rows, n, db, dq),
            kiwi[np_rows:, :IDX_DIM].reshape(db, dq, IDX_DIM), conv_s)


def kernel(x_prompt, x_sample, p_prompt, p_sample, cache_k, cache_v, cache_kidx, state_conv, page_table, w_in, b_in, w_o, b_o, ln1_g, ln1_b, conv_w, conv_b, conv_ln_g, conv_ln_b, w_router, b_router, w_gate_up, b_gate_up, w_down, b_down, ln2_g, ln2_b, w_ple_gate, b_ple_gate, w_ple_proj, ln3_g, ln3_b):
    assert w_in.shape[0] == DEPTH
    outs = _layer(x_prompt, x_sample, p_prompt[0], p_sample[0], cache_k[0], cache_v[0], cache_kidx[0],
                  state_conv[0], page_table, w_in[0], b_in[0], w_o[0], b_o[0], ln1_g[0], ln1_b[0],
                  conv_w[0], conv_b[0], conv_ln_g[0], conv_ln_b[0], w_router[0], b_router[0],
                  w_gate_up[0], b_gate_up[0], w_down[0], b_down[0], ln2_g[0], ln2_b[0],
                  w_ple_gate[0], b_ple_gate[0], w_ple_proj[0], ln3_g[0], ln3_b[0])
    y_p, y_s = outs[0], outs[1]
    return (y_p, y_s) + tuple(o[None] for o in outs[2:])
```

```python
import functools
import math

import jax
import jax.numpy as jnp
from jax import lax
from jax.experimental import pallas as pl
from jax.experimental.pallas import tpu as pltpu

F32 = jnp.float32
BF16 = jnp.bfloat16
I32 = jnp.int32

HEAD_DIM = 64
N_HEADS = 8
N_KV_HEADS = 2
KV_GROUP = N_HEADS // N_KV_HEADS
IDX_HEADS = 4
IDX_DIM = 64
INDEX_TOPK = 256
ROPE_THETA = 500000.0
ROPE_HALF = HEAD_DIM // 8
CONV_WIDTH = 31
N_EXPERTS = 32
TOP_K = 4
SWIGLU_LIMIT = 7.0
SWIGLU_ALPHA = 1.702
LN_EPS = 1e-5
DEPTH = 1
DN_ALPHA = (2 * DEPTH) ** 0.25
ATTN_SCALE = HEAD_DIM ** -0.5
IDX_W_SCALE = (IDX_HEADS * IDX_DIM) ** -0.5
LOG2E = math.log2(math.e)

LANES = 128
SUBLANES = 8
VMEM_LIMIT = 56 * 1024 * 1024
NEG_BIG = -1e30
INT_MIN = -(2 ** 31)

Q_W = N_HEADS * HEAD_DIM
KV_W = N_KV_HEADS * HEAD_DIM
QI_W = IDX_HEADS * IDX_DIM
VT_ROWS = HEAD_DIM + 16


def _cparams(sem):
    return pltpu.CompilerParams(dimension_semantics=sem, vmem_limit_bytes=VMEM_LIMIT)


def _rope_slab(z, cos, sin):
    lane = lax.broadcasted_iota(I32, (1, LANES), 1)
    first = (lane % HEAD_DIM) < ROPE_HALF
    up = pltpu.roll(z, LANES - ROPE_HALF, axis=1)
    dn = pltpu.roll(z, ROPE_HALF, axis=1)
    return z * cos + jnp.where(first, up, dn) * sin


def _inproj_kernel(x_ref, w_ref, b_ref, cos_ref, sin_ref,
                   q_ref, kf_ref, vf_ref, kb_ref, vt_ref, qi_ref, kiwi_ref, kib_ref, u_ref, *, conv_ch):
    x = x_ref[...].astype(BF16)
    cos = cos_ref[...]
    sin = sin_ref[...]

    def proj(lo, width):
        return jnp.dot(x, w_ref[:, lo:lo + width], preferred_element_type=F32) + b_ref[:, lo:lo + width]

    def rope(z):
        n = z.shape[1] // LANES
        slabs = [_rope_slab(z[:, s * LANES:(s + 1) * LANES], cos, sin) for s in range(n)]
        return slabs[0] if n == 1 else jnp.concatenate(slabs, axis=1)

    o = 0
    q = rope(proj(o, Q_W))
    q_ref[...] = (q * (ATTN_SCALE * LOG2E)).astype(BF16)
    o += Q_W
    k = rope(proj(o, KV_W))
    kf_ref[...] = k
    kb_ref[...] = k.astype(BF16)
    o += KV_W
    v = proj(o, KV_W)
    vf_ref[...] = v
    vt = v.T
    ones = jnp.ones((VT_ROWS - HEAD_DIM, vt.shape[1]), F32)
    vt_ref[0] = jnp.concatenate(
        [piece for g in range(N_KV_HEADS) for piece in (vt[g * HEAD_DIM:(g + 1) * HEAD_DIM], ones)],
        axis=0).astype(BF16)
    o += KV_W
    qi_ref[...] = rope(proj(o, QI_W)).astype(BF16)
    o += QI_W
    lane = lax.broadcasted_iota(I32, (1, LANES), 1)
    is_ki = lane < IDX_DIM
    kiwi = _rope_slab(proj(o, LANES), jnp.where(is_ki, cos, 1.0), jnp.where(is_ki, sin, 0.0))
    kiwi = kiwi * jnp.where((lane >= IDX_DIM) & (lane < IDX_DIM + IDX_HEADS), IDX_W_SCALE, 1.0)
    kiwi_ref[...] = kiwi
    kib_ref[...] = kiwi[:, :IDX_DIM].astype(BF16)
    o += LANES
    a = proj(o, conv_ch)
    g = proj(o + conv_ch, conv_ch)
    u_ref[...] = a * jax.nn.sigmoid(g)


def _inproj_call(x, w_pad, b_pad, cos_tab, sin_tab, tab_index, tm):
    n, d = x.shape
    nw = w_pad.shape[1]
    conv_ch = (nw - (Q_W + 2 * KV_W + QI_W + LANES)) // 2
    nt = n // tm
    row = lambda i: (i, 0)
    const = lambda i: (0, 0)
    out_shapes = (
        jax.ShapeDtypeStruct((n, Q_W), BF16),
        jax.ShapeDtypeStruct((n, KV_W), F32),
        jax.ShapeDtypeStruct((n, KV_W), F32),
        jax.ShapeDtypeStruct((n, KV_W), BF16),
        jax.ShapeDtypeStruct((nt, N_KV_HEADS * VT_ROWS, tm), BF16),
        jax.ShapeDtypeStruct((n, QI_W), BF16),
        jax.ShapeDtypeStruct((n, LANES), F32),
        jax.ShapeDtypeStruct((n, IDX_DIM), BF16),
        jax.ShapeDtypeStruct((n, conv_ch), F32),
    )
    out_specs = (
        pl.BlockSpec((tm, Q_W), row),
        pl.BlockSpec((tm, KV_W), row),
        pl.BlockSpec((tm, KV_W), row),
        pl.BlockSpec((tm, KV_W), row),
        pl.BlockSpec((1, N_KV_HEADS * VT_ROWS, tm), lambda i: (i, 0, 0)),
        pl.BlockSpec((tm, QI_W), row),
        pl.BlockSpec((tm, LANES), row),
        pl.BlockSpec((tm, IDX_DIM), row),
        pl.BlockSpec((tm, conv_ch), row),
    )
    return pl.pallas_call(
        functools.partial(_inproj_kernel, conv_ch=conv_ch),
        grid=(nt,),
        in_specs=[
            pl.BlockSpec((tm, d), row),
            pl.BlockSpec((d, nw), const),
            pl.BlockSpec((1, nw), const),
            pl.BlockSpec((tm, LANES), lambda i: (tab_index(i), 0)),
            pl.BlockSpec((tm, LANES), lambda i: (tab_index(i), 0)),
        ],
        out_specs=out_specs,
        out_shape=out_shapes,
        compiler_params=_cparams(("parallel",)),
    )(x, w_pad, b_pad, cos_tab, sin_tab)


def _sortable(score):
    bits = lax.bitcast_convert_type(score, I32)
    key = bits ^ ((bits >> 31) & 0x7FFFFFFF)
    return jnp.where(score == 0.0, 0, key)


def _kth_largest(count_ge, shape, n_sel):
    def bit_body(t, thr):
        cand = thr + jnp.left_shift(jnp.int32(1), 31 - t)
        return jnp.where(count_ge(cand) >= n_sel, cand, thr)

    return lax.fori_loop(0, 32, bit_body, jnp.full(shape, INT_MIN, I32))


def _tie_bound(count_eq_below, need, shape, idx_bits):
    def bit_body(t, j0):
        cand = j0 + jnp.left_shift(jnp.int32(1), idx_bits - 1 - t)
        return jnp.where(count_eq_below(cand) < need, cand, j0)

    return lax.fori_loop(0, idx_bits, bit_body, jnp.zeros(shape, I32))


NO_BOUND = 2 ** 30
I16 = jnp.int16
HALF_MIN = -(2 ** 15)
HALF_ROWS = 16


N_PAIRS = N_HEADS // 2
PAIRS_PER_GROUP = KV_GROUP // 2


def _prompt_attn_kernel(q_ref, qi_ref, kiwi_ref, kib_ref, kb_ref, vt_ref, o_ref,
                        qst_ref, qit_ref, keys_ref, half_ref, bias_ref, j0_ref,
                        lg0_ref, lg1_ref, mx0_ref, mx1_ref, pr0_ref, pr1_ref, al0_ref, al1_ref, m_ref, acc_ref,
                        *, qb, kc, n_sel, idx_bits):
    i = pl.program_id(1)
    n_chunks = (i * qb + qb + kc - 1) // kc
    q_pos = i * qb + lax.broadcasted_iota(I32, (1, qb), 1)

    def key_pos(k0):
        return k0 + lax.broadcasted_iota(I32, (kc, 1), 0)

    qt = q_ref[...].astype(F32).T
    zero = jnp.zeros((HEAD_DIM, 2 * qb), F32)
    for p in range(N_PAIRS):
        g = p // PAIRS_PER_GROUP
        pair = jnp.concatenate([qt[(2 * p) * HEAD_DIM:(2 * p + 1) * HEAD_DIM],
                                qt[(2 * p + 1) * HEAD_DIM:(2 * p + 2) * HEAD_DIM]], axis=1)
        qst_ref[p] = jnp.concatenate([pair if gg == g else zero for gg in range(N_KV_HEADS)], axis=0).astype(BF16)
    qit = qi_ref[...].astype(F32).T
    qit_ref[...] = jnp.concatenate(
        [qit[h * IDX_DIM:(h + 1) * IDX_DIM] for h in range(IDX_HEADS)], axis=1).astype(BF16)
    wit = kiwi_ref[...].T[IDX_DIM:IDX_DIM + SUBLANES]

    def score_body(c, carry):
        k0 = pl.multiple_of(c * kc, kc)
        s = jnp.dot(kib_ref[pl.ds(k0, kc), :], qit_ref[...], preferred_element_type=F32)
        score = jnp.zeros((kc, qb), F32)
        for h in range(IDX_HEADS):
            score = score + wit[h:h + 1, :] * jnp.maximum(s[:, h * qb:(h + 1) * qb], 0.0)
        score = jnp.where(key_pos(k0) <= q_pos, score, -jnp.inf)
        key = _sortable(score)
        keys_ref[pl.ds(k0, kc), :] = key
        half_ref[pl.ds(k0, kc), :] = (key >> 16).astype(I16)
        return carry

    lax.fori_loop(0, n_chunks, score_body, 0)

    def count(pred):
        def body(c, acc):
            k0 = pl.multiple_of(c * kc, kc)
            ones = jnp.where(pred(keys_ref[pl.ds(k0, kc), :], k0), 1, 0).astype(I32)
            return acc + jnp.sum(ones.reshape(kc // SUBLANES, SUBLANES, qb), axis=0)

        acc = lax.fori_loop(0, n_chunks, body, jnp.zeros((SUBLANES, qb), I32))
        return jnp.sum(acc, axis=0, keepdims=True)

    def count_half_ge(cand):
        cand16 = cand.astype(I16)

        def body(c, acc):
            k0 = pl.multiple_of(c * kc, kc)
            ones = jnp.where(half_ref[pl.ds(k0, kc), :] >= cand16, jnp.ones((), I16), jnp.zeros((), I16))
            parts = [ones[s * HALF_ROWS:(s + 1) * HALF_ROWS] for s in range(kc // HALF_ROWS)]
            while len(parts) > 1:
                parts = [a + b for a, b in zip(parts[0::2], parts[1::2])]
            return acc + parts[0]

        acc = lax.fori_loop(0, n_chunks, body, jnp.zeros((HALF_ROWS, qb), I16))
        return jnp.sum(acc.astype(I32), axis=0, keepdims=True)

    def kth_largest_half(n_want):
        def bit_body(tb, thr16):
            cand = thr16 + jnp.left_shift(jnp.int32(1), 15 - tb)
            return jnp.where(count_half_ge(cand) >= n_want, cand, thr16)

        return lax.fori_loop(0, 16, bit_body, jnp.full((1, qb), HALF_MIN, I32))

    thr_hi = kth_largest_half(n_sel)
    above = count(lambda blk, k0: (blk >> 16) > thr_hi)

    def low_body(c, carry):
        k0 = pl.multiple_of(c * kc, kc)
        blk = keys_ref[pl.ds(k0, kc), :]
        low = (blk & 0xFFFF) + HALF_MIN
        half_ref[pl.ds(k0, kc), :] = jnp.where((blk >> 16) == thr_hi, low, HALF_MIN).astype(I16)
        return carry

    lax.fori_loop(0, n_chunks, low_body, 0)
    thr_lo = kth_largest_half(n_sel - above)
    thr = thr_hi * 65536 + (thr_lo - HALF_MIN)
    need = n_sel - count(lambda blk, k0: blk > thr)
    tie = count(lambda blk, k0: blk >= thr) > n_sel

    j0_ref[...] = jnp.full((1, qb), NO_BOUND, I32)

    @pl.when(jnp.max(jnp.where(tie, 1, 0)) > 0)
    def _():
        j0 = _tie_bound(lambda cand: count(lambda blk, k0: (blk == thr) & (key_pos(k0) < cand)),
                        need, (1, qb), idx_bits)
        j0_ref[...] = jnp.where(tie, j0, NO_BOUND)

    j0 = j0_ref[...]

    def bias_body(c, carry):
        k0 = pl.multiple_of(c * kc, kc)
        blk = keys_ref[pl.ds(k0, kc), :]
        kp = key_pos(k0)
        sel = ((blk > thr) | ((blk == thr) & (kp <= j0))) & (kp <= q_pos)
        bias_ref[pl.ds(k0, kc), :] = jnp.where(sel, 0.0, NEG_BIG)
        return carry

    lax.fori_loop(0, n_chunks, bias_body, 0)

    m_ref[...] = jnp.full(m_ref.shape, NEG_BIG, F32)
    acc_ref[...] = jnp.zeros(acc_ref.shape, F32)
    lgs, mxs, prs, als = (lg0_ref, lg1_ref), (mx0_ref, mx1_ref), (pr0_ref, pr1_ref), (al0_ref, al1_ref)

    bias_ref[pl.ds(pl.multiple_of(n_chunks * kc, kc), kc), :] = jnp.full((kc, qb), NEG_BIG, F32)

    def logits(c, slot):
        k0 = pl.multiple_of(jnp.minimum(c, n_chunks - 1) * kc, kc)
        b0 = pl.multiple_of(jnp.minimum(c, n_chunks) * kc, kc)
        for p in range(N_PAIRS):
            lg = jnp.dot(kb_ref[pl.ds(k0, kc), :], qst_ref[p], preferred_element_type=F32)
            halves = [lg[:, h * qb:(h + 1) * qb] + bias_ref[pl.ds(b0, kc), :] for h in range(2)]
            for h in range(2):
                lgs[slot][p, :, h * qb:(h + 1) * qb] = halves[h]
            mxs[slot][p] = jnp.concatenate([jnp.max(x, axis=0, keepdims=True) for x in halves], axis=1)

    def probs(slot):
        for p in range(N_PAIRS):
            m_old = m_ref[p]
            m_new = jnp.maximum(m_old, mxs[slot][p])
            als[slot][p] = jnp.exp2(m_old - m_new)
            prs[slot][p] = jnp.exp2(lgs[slot][p] - m_new).astype(BF16)
            m_ref[p] = m_new

    def weighted(c, slot):
        cc = jnp.clip(c, 0, n_chunks - 1)
        for p in range(N_PAIRS):
            g = p // PAIRS_PER_GROUP
            vt = vt_ref[cc, g * VT_ROWS:(g + 1) * VT_ROWS, :]
            acc_ref[p] = als[slot][p] * acc_ref[p] + jnp.dot(vt, prs[slot][p], preferred_element_type=F32)

    logits(0, 0)
    pr1_ref[...] = jnp.zeros(pr1_ref.shape, BF16)
    al1_ref[...] = jnp.ones(al1_ref.shape, F32)

    def attn_body(j, carry):
        c = 2 * j
        logits(c + 1, 1)
        probs(0)
        weighted(c - 1, 1)
        logits(c + 2, 0)
        probs(1)
        weighted(c, 0)
        return carry

    lax.fori_loop(0, (n_chunks + 1) // 2, attn_body, 0)
    weighted(n_chunks - 1, 1)

    heads = []
    for p in range(N_PAIRS):
        acc = acc_ref[p]
        o = acc[:HEAD_DIM] * (1.0 / acc[HEAD_DIM:HEAD_DIM + 1])
        heads += [o[:, :qb], o[:, qb:]]
    o_ref[...] = jnp.concatenate(heads, axis=0).T.astype(o_ref.dtype)


def _prompt_attn_call(q, qi, kiwi, kib, kb, vt, nb, t, qb, kc):
    nqb = t // qb
    n_sel = min(INDEX_TOPK, t // 4)
    blk = lambda b, i: (b * nqb + i, 0)
    seq = lambda b, i: (b, 0)
    kern = functools.partial(_prompt_attn_kernel, qb=qb, kc=kc, n_sel=n_sel, idx_bits=max(1, (t - 1).bit_length()))
    return pl.pallas_call(
        kern,
        grid=(nb, nqb),
        in_specs=[
            pl.BlockSpec((qb, Q_W), blk),
            pl.BlockSpec((qb, QI_W), blk),
            pl.BlockSpec((qb, LANES), blk),
            pl.BlockSpec((t, IDX_DIM), seq),
            pl.BlockSpec((t, KV_W), seq),
            pl.BlockSpec((t // kc, N_KV_HEADS * VT_ROWS, kc), lambda b, i: (b, 0, 0)),
        ],
        out_specs=pl.BlockSpec((qb, Q_W), blk),
        out_shape=jax.ShapeDtypeStruct((nb * t, Q_W), BF16),
        scratch_shapes=[
            pltpu.VMEM((N_PAIRS, KV_W, 2 * qb), BF16),
            pltpu.VMEM((IDX_DIM, IDX_HEADS * qb), BF16),
            pltpu.VMEM((t, qb), I32),
            pltpu.VMEM((t, qb), I16),
            pltpu.VMEM((t + kc, qb), F32),
            pltpu.VMEM((1, qb), I32),
            pltpu.VMEM((N_PAIRS, kc, 2 * qb), F32),
            pltpu.VMEM((N_PAIRS, kc, 2 * qb), F32),
            pltpu.VMEM((N_PAIRS, 1, 2 * qb), F32),
            pltpu.VMEM((N_PAIRS, 1, 2 * qb), F32),
            pltpu.VMEM((N_PAIRS, kc, 2 * qb), BF16),
            pltpu.VMEM((N_PAIRS, kc, 2 * qb), BF16),
            pltpu.VMEM((N_PAIRS, 1, 2 * qb), F32),
            pltpu.VMEM((N_PAIRS, 1, 2 * qb), F32),
            pltpu.VMEM((N_PAIRS, 1, 2 * qb), F32),
            pltpu.VMEM((N_PAIRS, VT_ROWS, 2 * qb), F32),
        ],
        compiler_params=_cparams(("parallel", "arbitrary")),
    )(q, qi, kiwi, kib, kb, vt)


PAGES_PER_DOT = 4


def _sample_attn_kernel(pt_ref, q_ref, qi_ref, kiwi_ref, kn_ref, vn_ref, cki_hbm, ck_hbm, cv_hbm, o_ref,
                        kit_buf, kt_buf, vt_buf, sem, kin_s, knew_s, vnew_s, keys_ref, bias_ref, lg_ref, j0_ref,
                        *, n_pages, dq, n_sel, idx_bits):
    b = pl.program_id(0)
    nb = pl.num_programs(0)
    slot = b % 2
    past = n_pages * LANES
    lp = past + LANES
    streams = ((cki_hbm, kit_buf), (ck_hbm, kt_buf), (cv_hbm, vt_buf))

    def page_copy(a, seq, pg, sl):
        src, dst = streams[a]
        return pltpu.make_async_copy(src.at[pt_ref[seq, pg]], dst.at[sl, pg], sem.at[sl, a])

    def fetch(seq, sl):
        def body(pg, carry):
            for a in range(len(streams)):
                page_copy(a, seq, pg, sl).start()
            return carry

        lax.fori_loop(0, n_pages, body, 0)

    @pl.when(b == 0)
    def _():
        fetch(0, 0)

    @pl.when(b + 1 < nb)
    def _():
        fetch(b + 1, 1 - slot)

    def wait_body(pg, carry):
        for a in range(len(streams)):
            page_copy(a, b, pg, slot).wait()
        return carry

    lax.fori_loop(0, n_pages, wait_body, 0)

    def pages(buf, c):
        return jnp.concatenate([buf[slot, c * PAGES_PER_DOT + r] for r in range(PAGES_PER_DOT)], axis=1).astype(BF16)

    kc = PAGES_PER_DOT * LANES
    q = q_ref[0]
    qi = qi_ref[0]
    kiwi = kiwi_ref[0]
    qrow = lax.broadcasted_iota(I32, (dq, 1), 0)
    idx = lax.broadcasted_iota(I32, (1, lp), 1)
    rel = idx - past
    adm = (idx < past) | ((rel <= qrow) & (rel < dq))

    kin_s[...] = jnp.zeros(kin_s.shape, F32)
    knew_s[...] = jnp.zeros(knew_s.shape, F32)
    vnew_s[...] = jnp.zeros(vnew_s.shape, F32)
    kin_s[0:dq, :] = kiwi[:, :IDX_DIM]
    knew_s[0:dq, :] = kn_ref[0]
    vnew_s[0:dq, :] = vn_ref[0]

    nt = (((1,), (1,)), ((), ()))
    qi_rows = jnp.concatenate([qi[:, h * IDX_DIM:(h + 1) * IDX_DIM] for h in range(IDX_HEADS)], axis=0).astype(BF16)
    wi_col = jnp.concatenate([kiwi[:, IDX_DIM + h:IDX_DIM + h + 1] for h in range(IDX_HEADS)], axis=0)

    def scores(s):
        r = jnp.maximum(s, 0.0) * wi_col
        out = r[0:dq]
        for h in range(1, IDX_HEADS):
            out = out + r[h * dq:(h + 1) * dq]
        return out

    for c in range(n_pages // PAGES_PER_DOT):
        s = jnp.dot(qi_rows, pages(kit_buf, c), preferred_element_type=F32)
        keys_ref[:, c * kc:(c + 1) * kc] = _sortable(scores(s))
    s_new = scores(lax.dot_general(qi_rows, kin_s[...].astype(BF16), nt, preferred_element_type=F32))
    keys_ref[:, past:] = _sortable(jnp.where(adm[:, past:], s_new, -jnp.inf))

    def count(pred):
        return jnp.sum(jnp.where(pred(keys_ref[...]), 1, 0).astype(I32), axis=1, keepdims=True)

    thr = _kth_largest(lambda cand: count(lambda k: k >= cand), (dq, 1), n_sel)
    need = n_sel - count(lambda k: k > thr)
    tie = count(lambda k: k >= thr) > n_sel
    j0_ref[...] = jnp.full((dq, 1), NO_BOUND, I32)

    @pl.when(jnp.max(jnp.where(tie, 1, 0)) > 0)
    def _():
        j0 = _tie_bound(lambda cand: count(lambda k: (k == thr) & (idx < cand)), need, (dq, 1), idx_bits)
        j0_ref[...] = jnp.where(tie, j0, NO_BOUND)

    j0 = j0_ref[...]
    keys = keys_ref[...]
    sel = ((keys > thr) | ((keys == thr) & (idx <= j0))) & adm
    bias_ref[...] = jnp.where(sel, 0.0, NEG_BIG)

    zero = jnp.zeros((dq, HEAD_DIM), F32)
    rows = []
    for g in range(N_KV_HEADS):
        for r in range(KV_GROUP):
            h = g * KV_GROUP + r
            piece = q[:, h * HEAD_DIM:(h + 1) * HEAD_DIM]
            rows.append(jnp.concatenate([piece, zero] if g == 0 else [zero, piece], axis=1))
    qbd = jnp.concatenate(rows, axis=0).astype(BF16)

    for c in range(n_pages // PAGES_PER_DOT):
        lg_ref[:, c * kc:(c + 1) * kc] = jnp.dot(qbd, pages(kt_buf, c), preferred_element_type=F32)
    lg_ref[:, past:] = lax.dot_general(qbd, knew_s[...].astype(BF16), nt, preferred_element_type=F32)

    bias = bias_ref[...]
    lg = lg_ref[...] + jnp.concatenate([bias] * N_HEADS, axis=0)
    m = jnp.max(lg, axis=1, keepdims=True)
    p = jnp.exp2(lg - m)
    l = jnp.sum(p, axis=1, keepdims=True)
    lg_ref[...] = p
    o = jnp.dot(lg_ref[:, past:].astype(BF16), vnew_s[...].astype(BF16), preferred_element_type=F32)
    for c in range(n_pages // PAGES_PER_DOT):
        o = o + lax.dot_general(lg_ref[:, c * kc:(c + 1) * kc].astype(BF16), pages(vt_buf, c), nt,
                                preferred_element_type=F32)
    o_ref[0] = o * (1.0 / l)


def _sample_attn_call(page_table, q, qi, kiwi, kn, vn, cki_t, ck_t, cv_t):
    nb, dq, _ = q.shape
    n_pages = page_table.shape[1]
    assert cki_t.shape[2] == LANES and n_pages % PAGES_PER_DOT == 0
    lp = (n_pages + 1) * LANES
    n_sel = min(INDEX_TOPK, (n_pages * LANES + dq) // 4)
    b3 = lambda b, pt: (b, 0, 0)
    kern = functools.partial(_sample_attn_kernel, n_pages=n_pages, dq=dq, n_sel=n_sel,
                             idx_bits=max(1, (lp - 1).bit_length()))
    grid_spec = pltpu.PrefetchScalarGridSpec(
        num_scalar_prefetch=1,
        grid=(nb,),
        in_specs=[
            pl.BlockSpec((1, dq, Q_W), b3),
            pl.BlockSpec((1, dq, QI_W), b3),
            pl.BlockSpec((1, dq, LANES), b3),
            pl.BlockSpec((1, dq, KV_W), b3),
            pl.BlockSpec((1, dq, KV_W), b3),
            pl.BlockSpec(memory_space=pl.ANY),
            pl.BlockSpec(memory_space=pl.ANY),
            pl.BlockSpec(memory_space=pl.ANY),
        ],
        out_specs=pl.BlockSpec((1, N_HEADS * dq, KV_W), b3),
        scratch_shapes=[
            pltpu.VMEM((2, n_pages, IDX_DIM, LANES), F32),
            pltpu.VMEM((2, n_pages, KV_W, LANES), F32),
            pltpu.VMEM((2, n_pages, KV_W, LANES), F32),
            pltpu.SemaphoreType.DMA((2, 3)),
            pltpu.VMEM((LANES, IDX_DIM), F32),
            pltpu.VMEM((LANES, KV_W), F32),
            pltpu.VMEM((LANES, KV_W), F32),
            pltpu.VMEM((dq, lp), I32),
            pltpu.VMEM((dq, lp), F32),
            pltpu.VMEM((N_HEADS * dq, lp), F32),
            pltpu.VMEM((dq, 1), I32),
        ],
    )
    return pl.pallas_call(
        kern,
        grid_spec=grid_spec,
        out_shape=jax.ShapeDtypeStruct((nb, N_HEADS * dq, KV_W), F32),
        compiler_params=_cparams(("arbitrary",)),
    )(page_table, q, qi, kiwi, kn, vn, cki_t, ck_t, cv_t)


def _layer_norm(x, g, b):
    mu = jnp.mean(x, axis=-1, keepdims=True)
    xc = x - mu
    var = jnp.mean(xc * xc, axis=-1, keepdims=True)
    return xc * lax.rsqrt(var + LN_EPS) * g + b


HIST_ROWS = 32
HIST_PAD = HIST_ROWS - (CONV_WIDTH - 1)
CONV_ROWS = 64


def _mix_kernel(*refs, tm, nseq, use_prev, aliased):
    (attn_ref, u_ref, uprev_ref, hist_ref, x_ref, cw_ref, cb_ref, cg_ref, cbb_ref, wo_ref, bo_ref,
     g1_ref, b1_ref, wr_ref, br_ref) = refs[:15]
    h_ref, hb_ref, route_ref, xp_ref, conv_ref = refs[15 + aliased:]
    rows = tm // nseq
    ch = u_ref.shape[1]

    if use_prev:
        first = pl.program_id(1) == 0

        @pl.when(first)
        def _():
            xp_ref[:, 0:HIST_ROWS, :] = hist_ref[...]

        @pl.when(jnp.logical_not(first))
        def _():
            xp_ref[0, 0:HIST_ROWS, :] = uprev_ref[...]
    else:
        xp_ref[:, 0:HIST_ROWS, :] = hist_ref[...]
    xp_ref[:, HIST_ROWS:, :] = u_ref[...].reshape(nseq, rows, ch)

    rs = min(rows, CONV_ROWS)

    def seq_body(s, carry):
        for r0 in range(0, rows, rs):
            acc = jnp.zeros((rs, ch), F32)
            for j in range(CONV_WIDTH):
                lo = r0 + HIST_PAD + j
                acc = acc + cw_ref[j:j + 1, :] * xp_ref[s, lo:lo + rs, :]
            conv_ref[pl.ds(pl.multiple_of(s * rows + r0, SUBLANES), rs), :] = acc
        return carry

    if nseq == 1:
        seq_body(0, 0)
    else:
        lax.fori_loop(0, nseq, seq_body, 0)

    y = _layer_norm(conv_ref[...] + cb_ref[...], cg_ref[...], cbb_ref[...])
    conv = (y * jax.nn.sigmoid(y)).astype(BF16)
    aw = attn_ref.shape[1]
    mixed = (jnp.dot(attn_ref[...].astype(BF16), wo_ref[0:aw, :], preferred_element_type=F32)
             + jnp.dot(conv, wo_ref[aw:, :], preferred_element_type=F32) + bo_ref[...])
    h = _layer_norm(DN_ALPHA * x_ref[...] + mixed, g1_ref[...], b1_ref[...])
    h_ref[...] = h
    hb = h.astype(BF16)
    hb_ref[...] = hb

    logits = jnp.dot(hb, wr_ref[...], preferred_element_type=F32) + br_ref[...]
    lane = lax.broadcasted_iota(I32, (1, LANES), 1)
    work = logits
    vals, ids = [], []
    for _ in range(TOP_K):
        mx = jnp.max(work, axis=1, keepdims=True)
        ix = jnp.min(jnp.where(work == mx, lane, LANES), axis=1, keepdims=True)
        vals.append(mx)
        ids.append(ix)
        work = jnp.where(lane == ix, -jnp.inf, work)
    ex = [jnp.exp(v - vals[0]) for v in vals]
    den = ex[0]
    for e in ex[1:]:
        den = den + e
    route = jnp.zeros((tm, LANES), F32)
    for k in range(TOP_K):
        route = jnp.where(lane == k, ex[k] / den, route)
        route = jnp.where(lane == TOP_K + k, ids[k].astype(F32), route)
    route_ref[...] = route


def _mix_call(attn, u, hist, x, weights, *, n_total, row_off, tm, nseq, grid, seq_len, prev_bufs=None):
    cw, cb, cg, cbb, wo, bo, g1, b1, wr, br = weights
    d = x.shape[1]
    ch = u.shape[1]
    use_prev = grid[1] > 1
    aliased = 0 if prev_bufs is None else 3
    tiles_per_seq = grid[1]
    off_t = row_off // tm
    tile = lambda b, i: (b * tiles_per_seq + i, 0)
    otile = lambda b, i: (off_t + b * tiles_per_seq + i, 0)
    const = lambda b, i: (0, 0)
    if use_prev:
        per_tile = tm // HIST_ROWS
        prev = lambda b, i: (jnp.maximum((off_t + b * tiles_per_seq + i) * per_tile - 1, 0), 0)
    else:
        prev = const
    in_specs = [
        pl.BlockSpec((tm, attn.shape[1]), tile),
        pl.BlockSpec((tm, ch), otile),
        pl.BlockSpec((HIST_ROWS, ch), prev),
        pl.BlockSpec((nseq, HIST_ROWS, ch), lambda b, i: (b * tiles_per_seq + i, 0, 0) if not use_prev else (b, 0, 0)),
        pl.BlockSpec((tm, d), tile),
        pl.BlockSpec(cw.shape, const),
        pl.BlockSpec(cb.shape, const),
        pl.BlockSpec(cg.shape, const),
        pl.BlockSpec(cbb.shape, const),
        pl.BlockSpec(wo.shape, const),
        pl.BlockSpec(bo.shape, const),
        pl.BlockSpec(g1.shape, const),
        pl.BlockSpec(b1.shape, const),
        pl.BlockSpec(wr.shape, const),
        pl.BlockSpec(br.shape, const),
    ]
    args = [attn, u, u, hist, x, cw, cb, cg, cbb, wo, bo, g1, b1, wr, br]
    io_alias = {}
    if prev_bufs is not None:
        for k, buf in enumerate(prev_bufs):
            in_specs.append(pl.BlockSpec(memory_space=pl.ANY))
            io_alias[len(args)] = k
            args.append(buf)
    kern = functools.partial(_mix_kernel, tm=tm, nseq=nseq, use_prev=use_prev, aliased=aliased)
    return pl.pallas_call(
        kern,
        grid=grid,
        in_specs=in_specs,
        out_specs=(
            pl.BlockSpec((tm, d), otile),
            pl.BlockSpec((tm, d), otile),
            pl.BlockSpec((tm, LANES), otile),
        ),
        out_shape=(
            jax.ShapeDtypeStruct((n_total, d), F32),
            jax.ShapeDtypeStruct((n_total, d), BF16),
            jax.ShapeDtypeStruct((n_total, LANES), F32),
        ),
        scratch_shapes=[
            pltpu.VMEM((nseq, HIST_ROWS + tm // nseq, ch), F32),
            pltpu.VMEM((tm, ch), F32),
        ],
        input_output_aliases=io_alias,
        compiler_params=_cparams(("parallel", "arbitrary")),
    )(*args)


CAST_ROWS = 128


def _moe_kernel(be_ref, nu_ref, x_ref, wgu_ref, bgu_ref, wd_ref, bd_ref, y_ref, wgu_s, wd_s):
    i = pl.program_id(0)
    used = i < nu_ref[0]
    changed = (i == 0) | (be_ref[i] != be_ref[jnp.maximum(i - 1, 0)])
    dff = wd_s.shape[0]

    @pl.when(used & changed)
    def _():
        def cast_gu(r, carry):
            r0 = pl.multiple_of(r * CAST_ROWS, CAST_ROWS)
            wgu_s[pl.ds(r0, CAST_ROWS), :] = wgu_ref[0, pl.ds(r0, CAST_ROWS), :].astype(BF16)
            return carry

        def cast_d(r, carry):
            r0 = pl.multiple_of(r * CAST_ROWS, CAST_ROWS)
            wd_s[pl.ds(r0, CAST_ROWS), :] = wd_ref[0, pl.ds(r0, CAST_ROWS), :].astype(BF16)
            return carry

        lax.fori_loop(0, wgu_s.shape[0] // CAST_ROWS, cast_gu, 0)
        lax.fori_loop(0, wd_s.shape[0] // CAST_ROWS, cast_d, 0)

    @pl.when(used)
    def _():
        gu = jnp.dot(x_ref[...], wgu_s[...], preferred_element_type=F32) + bgu_ref[0]
        gate = jnp.minimum(gu[:, :dff], SWIGLU_LIMIT)
        up = jnp.clip(gu[:, dff:], -SWIGLU_LIMIT, SWIGLU_LIMIT)
        act = (up + 1.0) * gate * jax.nn.sigmoid(SWIGLU_ALPHA * gate)
        y_ref[...] = jnp.dot(act.astype(BF16), wd_s[...], preferred_element_type=F32) + bd_ref[0]


def _moe_call(block_e, n_used, x_pad, wgu, bgu, wd, bd, bm):
    p, d = x_pad.shape
    ne, _, dgu = wgu.shape
    dff = wd.shape[1]
    n_blocks = p // bm
    grid_spec = pltpu.PrefetchScalarGridSpec(
        num_scalar_prefetch=2,
        grid=(n_blocks,),
        in_specs=[
            pl.BlockSpec((bm, d), lambda i, be, nu: (i, 0)),
            pl.BlockSpec((1, d, dgu), lambda i, be, nu: (be[i], 0, 0)),
            pl.BlockSpec((1, 1, dgu), lambda i, be, nu: (be[i], 0, 0)),
            pl.BlockSpec((1, dff, d), lambda i, be, nu: (be[i], 0, 0)),
            pl.BlockSpec((1, 1, d), lambda i, be, nu: (be[i], 0, 0)),
        ],
        out_specs=pl.BlockSpec((bm, d), lambda i, be, nu: (i, 0)),
        scratch_shapes=[pltpu.VMEM((d, dgu), BF16), pltpu.VMEM((dff, d), BF16)],
    )
    return pl.pallas_call(
        _moe_kernel,
        grid_spec=grid_spec,
        out_shape=jax.ShapeDtypeStruct((p, d), F32),
        compiler_params=_cparams(("arbitrary",)),
    )(block_e, n_used, x_pad, wgu, bgu.reshape(ne, 1, dgu), wd, bd.reshape(ne, 1, d))


def _dispatch_plan(route, bm):
    n = route.shape[0]
    nk = n * TOP_K
    experts = jnp.arange(N_EXPERTS, dtype=I32)
    e_flat = route[:, TOP_K:2 * TOP_K].astype(I32).reshape(-1)
    iota = jnp.arange(nk, dtype=I32)
    e_sorted, order = lax.sort((e_flat, iota), num_keys=1, is_stable=True)
    _, inverse = lax.sort((order, iota), num_keys=1)
    counts = jnp.sum((e_flat[:, None] == experts[None, :]).astype(I32), axis=0)
    padded = (counts + bm - 1) // bm * bm
    start = jnp.cumsum(counts) - counts
    pend = jnp.cumsum(padded)
    pstart = pend - padded
    slot_sorted = pstart[e_sorted] + iota - start[e_sorted]
    slot_orig = slot_sorted[inverse]
    n_blocks = -(-(nk + N_EXPERTS * (bm - 1)) // bm)
    block_start = jnp.arange(n_blocks, dtype=I32) * bm
    block_e = jnp.minimum(jnp.sum((pend[None, :] <= block_start[:, None]).astype(I32), axis=1), N_EXPERTS - 1)
    n_used = (pend[-1:] // bm).astype(I32)
    e_slot = jnp.repeat(block_e, bm)
    rank = jnp.arange(n_blocks * bm, dtype=I32) - pstart[e_slot]
    src = jnp.clip(start[e_slot] + rank, 0, nk - 1)
    slot_tok = jnp.where(rank < counts[e_slot], order[src] // TOP_K, 0)
    return slot_tok, block_e, n_used, slot_orig


def _final_kernel(h_ref, yg_ref, route_ref, pe_ref, g2_ref, b2_ref, wg_ref, bg_ref, wp_ref, g3_ref, b3_ref, o_ref):
    route = route_ref[...]
    y = route[:, 0:1] * yg_ref[0]
    for k in range(1, TOP_K):
        y = y + route[:, k:k + 1] * yg_ref[k]
    h2 = _layer_norm(DN_ALPHA * h_ref[...] + y, g2_ref[...], b2_ref[...])
    gate = jax.nn.sigmoid(jnp.dot(h2.astype(BF16), wg_ref[...], preferred_element_type=F32) + bg_ref[...])
    proj = jnp.dot(pe_ref[...].astype(BF16), wp_ref[...], preferred_element_type=F32)
    o_ref[...] = _layer_norm(DN_ALPHA * h2 + gate * proj, g3_ref[...], b3_ref[...])


def _final_call(h, yg, route, pe, weights, row_off, tm):
    g2, b2, wg, bg, wp, g3, b3 = weights
    n, pd = pe.shape
    d = h.shape[1]
    off_t = row_off // tm
    src = lambda i: (off_t + i, 0)
    row = lambda i: (i, 0)
    const = lambda i: (0, 0)
    return pl.pallas_call(
        _final_kernel,
        grid=(n // tm,),
        in_specs=[
            pl.BlockSpec((tm, d), src),
            pl.BlockSpec((TOP_K, tm, d), lambda i: (0, off_t + i, 0)),
            pl.BlockSpec((tm, LANES), src),
            pl.BlockSpec((tm, pd), row),
            pl.BlockSpec(g2.shape, const),
            pl.BlockSpec(b2.shape, const),
            pl.BlockSpec(wg.shape, const),
            pl.BlockSpec(bg.shape, const),
            pl.BlockSpec(wp.shape, const),
            pl.BlockSpec(g3.shape, const),
            pl.BlockSpec(b3.shape, const),
        ],
        out_specs=pl.BlockSpec((tm, d), row),
        out_shape=jax.ShapeDtypeStruct((n, d), F32),
        compiler_params=_cparams(("parallel",)),
    )(h, yg, route, pe, g2, b2, wg, bg, wp, g3, b3)


TOKEN_TILE = 512
QUERY_BLOCK = 128
MOE_ROWS = 512


def _rope_tables(pos):
    rd = 2 * ROPE_HALF
    inv_freq = ROPE_THETA ** (-jnp.arange(ROPE_HALF, dtype=F32) * 2.0 / rd)
    ang = pos.astype(F32)[:, None] * inv_freq[None, :]
    cos, sin = jnp.cos(ang), jnp.sin(ang)
    ones = jnp.ones((pos.shape[0], HEAD_DIM - rd), F32)
    cos_h = jnp.concatenate([cos, cos, ones], axis=1)
    sin_h = jnp.concatenate([-sin, sin, 0.0 * ones], axis=1)
    reps = LANES // HEAD_DIM
    return jnp.tile(cos_h, (1, reps)), jnp.tile(sin_h, (1, reps))


def _layer(x_prompt, x_sample, p_prompt, p_sample, cache_k, cache_v, cache_kidx, state_conv, page_table,
           w_in, b_in, w_o, b_o, ln1_g, ln1_b, conv_w, conv_b, conv_ln_g, conv_ln_b,
           w_router, b_router, w_gate_up, b_gate_up, w_down, b_down, ln2_g, ln2_b,
           w_ple_gate, b_ple_gate, w_ple_proj, ln3_g, ln3_b):
    nb, t, d = x_prompt.shape
    db, dq, _ = x_sample.shape
    n_pages, page = page_table.shape[1], cache_k.shape[1]
    past = n_pages * page
    ch = conv_w.shape[1]
    tm = min(TOKEN_TILE, t)
    np_rows, ns_rows = nb * t, db * dq
    n = np_rows + ns_rows
    assert t % tm == 0 and ns_rows % tm == 0 and tm % dq == 0 and dq == SUBLANES
    row2 = lambda a: a.reshape(1, -1)

    n_head = Q_W + 2 * KV_W + QI_W + IDX_DIM + IDX_HEADS
    pad = LANES - IDX_DIM - IDX_HEADS
    w_pad = jnp.concatenate([w_in[:, :n_head], jnp.zeros((d, pad), F32), w_in[:, n_head:]], axis=1).astype(BF16)
    b_pad = jnp.concatenate([b_in[:n_head], jnp.zeros((pad,), F32), b_in[n_head:]]).reshape(1, -1)
    pos = jnp.concatenate([jnp.arange(t, dtype=I32), past + (jnp.arange(tm, dtype=I32) % dq)])
    cos_tab, sin_tab = _rope_tables(pos)
    tiles_p = np_rows // tm
    x_all = jnp.concatenate([x_prompt.reshape(np_rows, d), x_sample.reshape(ns_rows, d)], axis=0)
    tab_index = lambda i: jnp.where(i < tiles_p, i % (t // tm), t // tm)
    q, k_f, v_f, k_b, v_t, qi, kiwi, ki_b, u = _inproj_call(x_all, w_pad, b_pad, cos_tab, sin_tab, tab_index, tm)

    attn_p = _prompt_attn_call(q, qi, kiwi, ki_b, k_b, v_t, nb, t, min(QUERY_BLOCK, t), tm)
    smp = lambda a: a[np_rows:].astype(F32).reshape(db, dq, -1)
    n_pool = cache_k.shape[0]
    kv_t = lambda pool: pool.transpose(0, 2, 3, 1).reshape(n_pool, KV_W, page)
    o_s = _sample_attn_call(page_table, smp(q), smp(qi), smp(kiwi), smp(k_f), smp(v_f),
                            cache_kidx.transpose(0, 2, 1), kv_t(cache_k), kv_t(cache_v))
    o_s = o_s.reshape(db, N_KV_HEADS, KV_GROUP, dq, N_KV_HEADS, HEAD_DIM)
    attn_s = jnp.stack([o_s[:, g, :, :, g, :] for g in range(N_KV_HEADS)], axis=1)
    attn_s = attn_s.transpose(0, 3, 1, 2, 4).reshape(ns_rows, Q_W)

    wr_pad = jnp.concatenate([w_router, jnp.zeros((d, LANES - N_EXPERTS), F32)], axis=1).astype(BF16)
    br_pad = jnp.concatenate([b_router, jnp.full((LANES - N_EXPERTS,), -jnp.inf, F32)]).reshape(1, -1)
    mix_w = (conv_w, row2(conv_b), row2(conv_ln_g), row2(conv_ln_b), w_o.astype(BF16), row2(b_o),
             row2(ln1_g), row2(ln1_b), wr_pad, br_pad)
    hist_p = jnp.zeros((nb, HIST_ROWS, ch), F32)
    hist_s = jnp.concatenate([jnp.zeros((db, HIST_PAD, ch), F32), state_conv], axis=1)
    bufs = _mix_call(attn_p, u, hist_p, x_prompt.reshape(np_rows, d), mix_w, n_total=n, row_off=0, tm=tm,
                     nseq=1, grid=(nb, t // tm), seq_len=t)
    h, h_b, route = _mix_call(attn_s, u, hist_s, x_sample.reshape(ns_rows, d), mix_w, n_total=n, row_off=np_rows,
                              tm=tm, nseq=tm // dq, grid=(ns_rows // tm, 1), seq_len=dq, prev_bufs=bufs)

    slot_tok, block_e, n_used, slot_orig = _dispatch_plan(route, MOE_ROWS)
    y_pad = _moe_call(block_e, n_used, h_b[slot_tok], w_gate_up, b_gate_up, w_down, b_down, MOE_ROWS)
    yg = y_pad[slot_orig.reshape(n, TOP_K).T]

    fin_w = (row2(ln2_g), row2(ln2_b), w_ple_gate.astype(BF16), row2(b_ple_gate), w_ple_proj.astype(BF16),
             row2(ln3_g), row2(ln3_b))
    y_p = _final_call(h, yg, route, p_prompt.reshape(np_rows, -1), fin_w, 0, tm)
    y_s = _final_call(h, yg, route, p_sample.reshape(ns_rows, -1), fin_w, np_rows, tm)

    u_p = u[:np_rows].reshape(nb, t, ch)
    u_s = u[np_rows:].reshape(db, dq, ch)
    keep = CONV_WIDTH - 1
    conv_p = jnp.concatenate([jnp.zeros((nb, keep, ch), F32), u_p], axis=1)[:, -keep:]
    conv_s = jnp.concatenate([state_conv, u_s], axis=1)[:, -keep:]
    kv = lambda a, lo, hi, b_, t_: a[lo:hi].reshape(b_, t_, N_KV_HEADS, HEAD_DIM)
    return (y_p.reshape(nb, t, d), y_s.reshape(db, dq, d),
            kv(k_f, 0, np_rows, nb, t), kv(v_f, 0, np_rows, nb, t),
            kiwi[:np_rows, :IDX_DIM].reshape(nb, t, IDX_DIM), conv_p,
            kv(k_f, np_rows, n, db, dq), kv(v_f, np_rows, n, db, dq),
            kiwi[np_rows:, :IDX_DIM].reshape(db, dq, IDX_DIM), conv_s)


def kernel(x_prompt, x_sample, p_prompt, p_sample, cache_k, cache_v, cache_kidx, state_conv, page_table, w_in, b_in, w_o, b_o, ln1_g, ln1_b, conv_w, conv_b, conv_ln_g, conv_ln_b, w_router, b_router, w_gate_up, b_gate_up, w_down, b_down, ln2_g, ln2_b, w_ple_gate, b_ple_gate, w_ple_proj, ln3_g, ln3_b):
    assert w_in.shape[0] == DEPTH
    outs = _layer(x_prompt, x_sample, p_prompt[0], p_sample[0], cache_k[0], cache_v[0], cache_kidx[0],
                  state_conv[0], page_table, w_in[0], b_in[0], w_o[0], b_o[0], ln1_g[0], ln1_b[0],
                  conv_w[0], conv_b[0], conv_ln_g[0], conv_ln_b[0], w_router[0], b_router[0],
                  w_gate_up[0], b_gate_up[0], w_down[0], b_down[0], ln2_g[0], ln2_b[0],
                  w_ple_gate[0], b_ple_gate[0], w_ple_proj[0], ln3_g[0], ln3_b[0])
    y_p, y_s = outs[0], outs[1]
    return (y_p, y_s) + tuple(o[None] for o in outs[2:])
```

```python
import functools
import math

import jax
import jax.numpy as jnp
from jax import lax
from jax.experimental import pallas as pl
from jax.experimental.pallas import tpu as pltpu

F32 = jnp.float32
BF16 = jnp.bfloat16
I32 = jnp.int32

HEAD_DIM = 64
N_HEADS = 8
N_KV_HEADS = 2
KV_GROUP = N_HEADS // N_KV_HEADS
IDX_HEADS = 4
IDX_DIM = 64
INDEX_TOPK = 256
ROPE_THETA = 500000.0
ROPE_HALF = HEAD_DIM // 8
CONV_WIDTH = 31
N_EXPERTS = 32
TOP_K = 4
SWIGLU_LIMIT = 7.0
SWIGLU_ALPHA = 1.702
LN_EPS = 1e-5
DEPTH = 1
DN_ALPHA = (2 * DEPTH) ** 0.25
ATTN_SCALE = HEAD_DIM ** -0.5
IDX_W_SCALE = (IDX_HEADS * IDX_DIM) ** -0.5
LOG2E = math.log2(math.e)

LANES = 128
SUBLANES = 8
VMEM_LIMIT = 56 * 1024 * 1024
NEG_BIG = -1e30
INT_MIN = -(2 ** 31)

Q_W = N_HEADS * HEAD_DIM
KV_W = N_KV_HEADS * HEAD_DIM
QI_W = IDX_HEADS * IDX_DIM
VT_ROWS = HEAD_DIM + 16


def _cparams(sem):
    return pltpu.CompilerParams(dimension_semantics=sem, vmem_limit_bytes=VMEM_LIMIT)


def _rope_slab(z, cos, sin):
    lane = lax.broadcasted_iota(I32, (1, LANES), 1)
    first = (lane % HEAD_DIM) < ROPE_HALF
    up = pltpu.roll(z, LANES - ROPE_HALF, axis=1)
    dn = pltpu.roll(z, ROPE_HALF, axis=1)
    return z * cos + jnp.where(first, up, dn) * sin


def _inproj_kernel(xp_ref, xs_ref, w_ref, b_ref, cos_ref, sin_ref,
                   q_ref, kf_ref, vf_ref, kb_ref, vt_ref, qi_ref, kiwi_ref, kib_ref, u_ref, *, conv_ch, tiles_p):
    x = jnp.where(pl.program_id(0) < tiles_p, xp_ref[...], xs_ref[...]).astype(BF16)
    cos = cos_ref[...]
    sin = sin_ref[...]

    def proj(lo, width):
        return jnp.dot(x, w_ref[:, lo:lo + width], preferred_element_type=F32) + b_ref[:, lo:lo + width]

    def rope(z):
        n = z.shape[1] // LANES
        slabs = [_rope_slab(z[:, s * LANES:(s + 1) * LANES], cos, sin) for s in range(n)]
        return slabs[0] if n == 1 else jnp.concatenate(slabs, axis=1)

    o = 0
    q = rope(proj(o, Q_W))
    q_ref[...] = (q * (ATTN_SCALE * LOG2E)).astype(BF16)
    o += Q_W
    k = rope(proj(o, KV_W))
    kf_ref[...] = k
    kb_ref[...] = k.astype(BF16)
    o += KV_W
    v = proj(o, KV_W)
    vf_ref[...] = v
    vt = v.T
    ones = jnp.ones((VT_ROWS - HEAD_DIM, vt.shape[1]), F32)
    vt_ref[0] = jnp.concatenate(
        [piece for g in range(N_KV_HEADS) for piece in (vt[g * HEAD_DIM:(g + 1) * HEAD_DIM], ones)],
        axis=0).astype(BF16)
    o += KV_W
    qi_ref[...] = rope(proj(o, QI_W)).astype(BF16)
    o += QI_W
    lane = lax.broadcasted_iota(I32, (1, LANES), 1)
    is_ki = lane < IDX_DIM
    kiwi = _rope_slab(proj(o, LANES), jnp.where(is_ki, cos, 1.0), jnp.where(is_ki, sin, 0.0))
    kiwi = kiwi * jnp.where((lane >= IDX_DIM) & (lane < IDX_DIM + IDX_HEADS), IDX_W_SCALE, 1.0)
    kiwi_ref[...] = kiwi
    kib_ref[...] = kiwi[:, :IDX_DIM].astype(BF16)
    o += LANES
    a = proj(o, conv_ch)
    g = proj(o + conv_ch, conv_ch)
    u_ref[...] = a * jax.nn.sigmoid(g)


def _inproj_call(x_p, x_s, w_pad, b_pad, cos_tab, sin_tab, tab_index, tm):
    d = x_p.shape[1]
    tiles_p, tiles_s = x_p.shape[0] // tm, x_s.shape[0] // tm
    nt = tiles_p + tiles_s
    n = nt * tm
    nw = w_pad.shape[1]
    conv_ch = (nw - (Q_W + 2 * KV_W + QI_W + LANES)) // 2
    row = lambda i: (i, 0)
    const = lambda i: (0, 0)
    out_shapes = (
        jax.ShapeDtypeStruct((n, Q_W), BF16),
        jax.ShapeDtypeStruct((n, KV_W), F32),
        jax.ShapeDtypeStruct((n, KV_W), F32),
        jax.ShapeDtypeStruct((n, KV_W), BF16),
        jax.ShapeDtypeStruct((nt, N_KV_HEADS * VT_ROWS, tm), BF16),
        jax.ShapeDtypeStruct((n, QI_W), BF16),
        jax.ShapeDtypeStruct((n, LANES), F32),
        jax.ShapeDtypeStruct((n, IDX_DIM), BF16),
        jax.ShapeDtypeStruct((n, conv_ch), F32),
    )
    out_specs = (
        pl.BlockSpec((tm, Q_W), row),
        pl.BlockSpec((tm, KV_W), row),
        pl.BlockSpec((tm, KV_W), row),
        pl.BlockSpec((tm, KV_W), row),
        pl.BlockSpec((1, N_KV_HEADS * VT_ROWS, tm), lambda i: (i, 0, 0)),
        pl.BlockSpec((tm, QI_W), row),
        pl.BlockSpec((tm, LANES), row),
        pl.BlockSpec((tm, IDX_DIM), row),
        pl.BlockSpec((tm, conv_ch), row),
    )
    return pl.pallas_call(
        functools.partial(_inproj_kernel, conv_ch=conv_ch, tiles_p=tiles_p),
        grid=(nt,),
        in_specs=[
            pl.BlockSpec((tm, d), lambda i: (jnp.minimum(i, tiles_p - 1), 0)),
            pl.BlockSpec((tm, d), lambda i: (jnp.maximum(i - tiles_p, 0), 0)),
            pl.BlockSpec((d, nw), const),
            pl.BlockSpec((1, nw), const),
            pl.BlockSpec((tm, LANES), lambda i: (tab_index(i), 0)),
            pl.BlockSpec((tm, LANES), lambda i: (tab_index(i), 0)),
        ],
        out_specs=out_specs,
        out_shape=out_shapes,
        compiler_params=_cparams(("parallel",)),
    )(x_p, x_s, w_pad, b_pad, cos_tab, sin_tab)


def _sortable(score):
    bits = lax.bitcast_convert_type(score, I32)
    key = bits ^ ((bits >> 31) & 0x7FFFFFFF)
    return jnp.where(score == 0.0, 0, key)


def _kth_largest(count_ge, shape, n_sel):
    def bit_body(t, thr):
        cand = thr + jnp.left_shift(jnp.int32(1), 31 - t)
        return jnp.where(count_ge(cand) >= n_sel, cand, thr)

    return lax.fori_loop(0, 32, bit_body, jnp.full(shape, INT_MIN, I32))


def _tie_bound(count_eq_below, need, shape, idx_bits):
    def bit_body(t, j0):
        cand = j0 + jnp.left_shift(jnp.int32(1), idx_bits - 1 - t)
        return jnp.where(count_eq_below(cand) < need, cand, j0)

    return lax.fori_loop(0, idx_bits, bit_body, jnp.zeros(shape, I32))


NO_BOUND = 2 ** 30
COUNT_CHAINS = 4


N_PAIRS = N_HEADS // 2
PAIRS_PER_GROUP = KV_GROUP // 2


def _prompt_attn_kernel(q_ref, qi_ref, kiwi_ref, kib_ref, kb_ref, vt_ref, o_ref,
                        qst_ref, qit_ref, keys_ref, bias_ref, j0_ref,
                        lg0_ref, lg1_ref, mx0_ref, mx1_ref, pr0_ref, pr1_ref, al0_ref, al1_ref, m_ref, acc_ref,
                        *, qb, kc, n_sel, idx_bits):
    i = pl.program_id(1)
    n_chunks = (i * qb + qb + kc - 1) // kc
    q_pos = i * qb + lax.broadcasted_iota(I32, (1, qb), 1)

    def key_pos(k0):
        return k0 + lax.broadcasted_iota(I32, (kc, 1), 0)

    qt = q_ref[...].astype(F32).T
    zero = jnp.zeros((HEAD_DIM, 2 * qb), F32)
    for p in range(N_PAIRS):
        g = p // PAIRS_PER_GROUP
        pair = jnp.concatenate([qt[(2 * p) * HEAD_DIM:(2 * p + 1) * HEAD_DIM],
                                qt[(2 * p + 1) * HEAD_DIM:(2 * p + 2) * HEAD_DIM]], axis=1)
        qst_ref[p] = jnp.concatenate([pair if gg == g else zero for gg in range(N_KV_HEADS)], axis=0).astype(BF16)
    qit = qi_ref[...].astype(F32).T
    qit_ref[...] = jnp.concatenate(
        [qit[h * IDX_DIM:(h + 1) * IDX_DIM] for h in range(IDX_HEADS)], axis=1).astype(BF16)
    wit = kiwi_ref[...].T[IDX_DIM:IDX_DIM + SUBLANES]

    def score_body(c, carry):
        k0 = pl.multiple_of(c * kc, kc)
        s = jnp.dot(kib_ref[pl.ds(k0, kc), :], qit_ref[...], preferred_element_type=F32)
        score = jnp.zeros((kc, qb), F32)
        for h in range(IDX_HEADS):
            score = score + wit[h:h + 1, :] * jnp.maximum(s[:, h * qb:(h + 1) * qb], 0.0)
        score = jnp.where(key_pos(k0) <= q_pos, score, -jnp.inf)
        keys_ref[pl.ds(k0, kc), :] = _sortable(score)
        return carry

    lax.fori_loop(0, n_chunks, score_body, 0)

    @pl.when(n_chunks % 2 == 1)
    def _():
        keys_ref[pl.ds(pl.multiple_of(n_chunks * kc, kc), kc), :] = jnp.full((kc, qb), INT_MIN, I32)

    kc2 = 2 * kc

    def key_pos2(k0):
        return k0 + lax.broadcasted_iota(I32, (kc2, 1), 0)

    def count(pred):
        span = kc2 // COUNT_CHAINS

        def body(c, accs):
            k0 = pl.multiple_of(c * kc2, kc2)
            ones = jnp.where(pred(keys_ref[pl.ds(k0, kc2), :], k0), 1, 0).astype(I32)
            return tuple(
                acc + jnp.sum(ones[j * span:(j + 1) * span].reshape(span // SUBLANES, SUBLANES, qb), axis=0)
                for j, acc in enumerate(accs))

        accs = lax.fori_loop(0, (n_chunks + 1) // 2, body,
                             tuple(jnp.zeros((SUBLANES, qb), I32) for _ in range(COUNT_CHAINS)))
        return jnp.sum(sum(accs[1:], accs[0]), axis=0, keepdims=True)

    thr = _kth_largest(lambda cand: count(lambda blk, k0: blk >= cand), (1, qb), n_sel)
    need = n_sel - count(lambda blk, k0: blk > thr)
    tie = count(lambda blk, k0: blk >= thr) > n_sel

    j0_ref[...] = jnp.full((1, qb), NO_BOUND, I32)

    @pl.when(jnp.max(jnp.where(tie, 1, 0)) > 0)
    def _():
        j0 = _tie_bound(lambda cand: count(lambda blk, k0: (blk == thr) & (key_pos2(k0) < cand)),
                        need, (1, qb), idx_bits)
        j0_ref[...] = jnp.where(tie, j0, NO_BOUND)

    j0 = j0_ref[...]

    def bias_body(c, carry):
        k0 = pl.multiple_of(c * kc, kc)
        blk = keys_ref[pl.ds(k0, kc), :]
        kp = key_pos(k0)
        sel = ((blk > thr) | ((blk == thr) & (kp <= j0))) & (kp <= q_pos)
        bias_ref[pl.ds(k0, kc), :] = jnp.where(sel, 0.0, NEG_BIG)
        return carry

    lax.fori_loop(0, n_chunks, bias_body, 0)

    m_ref[...] = jnp.full(m_ref.shape, NEG_BIG, F32)
    acc_ref[...] = jnp.zeros(acc_ref.shape, F32)
    lgs, mxs, prs, als = (lg0_ref, lg1_ref), (mx0_ref, mx1_ref), (pr0_ref, pr1_ref), (al0_ref, al1_ref)

    bias_ref[pl.ds(pl.multiple_of(n_chunks * kc, kc), kc), :] = jnp.full((kc, qb), NEG_BIG, F32)

    def logits(c, slot):
        k0 = pl.multiple_of(jnp.minimum(c, n_chunks - 1) * kc, kc)
        b0 = pl.multiple_of(jnp.minimum(c, n_chunks) * kc, kc)
        for p in range(N_PAIRS):
            lg = jnp.dot(kb_ref[pl.ds(k0, kc), :], qst_ref[p], preferred_element_type=F32)
            halves = [lg[:, h * qb:(h + 1) * qb] + bias_ref[pl.ds(b0, kc), :] for h in range(2)]
            for h in range(2):
                lgs[slot][p, :, h * qb:(h + 1) * qb] = halves[h]
            mxs[slot][p] = jnp.concatenate([jnp.max(x, axis=0, keepdims=True) for x in halves], axis=1)

    def probs(slot):
        for p in range(N_PAIRS):
            m_old = m_ref[p]
            m_new = jnp.maximum(m_old, mxs[slot][p])
            als[slot][p] = jnp.exp2(m_old - m_new)
            prs[slot][p] = jnp.exp2(lgs[slot][p] - m_new).astype(BF16)
            m_ref[p] = m_new

    def weighted(c, slot):
        cc = jnp.clip(c, 0, n_chunks - 1)
        for p in range(N_PAIRS):
            g = p // PAIRS_PER_GROUP
            vt = vt_ref[cc, g * VT_ROWS:(g + 1) * VT_ROWS, :]
            acc_ref[p] = als[slot][p] * acc_ref[p] + jnp.dot(vt, prs[slot][p], preferred_element_type=F32)

    logits(0, 0)
    pr1_ref[...] = jnp.zeros(pr1_ref.shape, BF16)
    al1_ref[...] = jnp.ones(al1_ref.shape, F32)

    def attn_body(j, carry):
        c = 2 * j
        logits(c + 1, 1)
        probs(0)
        weighted(c - 1, 1)
        logits(c + 2, 0)
        probs(1)
        weighted(c, 0)
        return carry

    lax.fori_loop(0, (n_chunks + 1) // 2, attn_body, 0)
    weighted(n_chunks - 1, 1)

    heads = []
    for p in range(N_PAIRS):
        acc = acc_ref[p]
        o = acc[:HEAD_DIM] * (1.0 / acc[HEAD_DIM:HEAD_DIM + 1])
        heads += [o[:, :qb], o[:, qb:]]
    o_ref[...] = jnp.concatenate(heads, axis=0).T.astype(o_ref.dtype)


def _prompt_attn_call(q, qi, kiwi, kib, kb, vt, nb, t, qb, kc):
    nqb = t // qb
    n_sel = min(INDEX_TOPK, t // 4)
    blk = lambda b, i: (b * nqb + i, 0)
    seq = lambda b, i: (b, 0)
    kern = functools.partial(_prompt_attn_kernel, qb=qb, kc=kc, n_sel=n_sel, idx_bits=max(1, (t - 1).bit_length()))
    return pl.pallas_call(
        kern,
        grid=(nb, nqb),
        in_specs=[
            pl.BlockSpec((qb, Q_W), blk),
            pl.BlockSpec((qb, QI_W), blk),
            pl.BlockSpec((qb, LANES), blk),
            pl.BlockSpec((t, IDX_DIM), seq),
            pl.BlockSpec((t, KV_W), seq),
            pl.BlockSpec((t // kc, N_KV_HEADS * VT_ROWS, kc), lambda b, i: (b, 0, 0)),
        ],
        out_specs=pl.BlockSpec((qb, Q_W), blk),
        out_shape=jax.ShapeDtypeStruct((nb * t, Q_W), BF16),
        scratch_shapes=[
            pltpu.VMEM((N_PAIRS, KV_W, 2 * qb), BF16),
            pltpu.VMEM((IDX_DIM, IDX_HEADS * qb), BF16),
            pltpu.VMEM((t + kc, qb), I32),
            pltpu.VMEM((t + kc, qb), F32),
            pltpu.VMEM((1, qb), I32),
            pltpu.VMEM((N_PAIRS, kc, 2 * qb), F32),
            pltpu.VMEM((N_PAIRS, kc, 2 * qb), F32),
            pltpu.VMEM((N_PAIRS, 1, 2 * qb), F32),
            pltpu.VMEM((N_PAIRS, 1, 2 * qb), F32),
            pltpu.VMEM((N_PAIRS, kc, 2 * qb), BF16),
            pltpu.VMEM((N_PAIRS, kc, 2 * qb), BF16),
            pltpu.VMEM((N_PAIRS, 1, 2 * qb), F32),
            pltpu.VMEM((N_PAIRS, 1, 2 * qb), F32),
            pltpu.VMEM((N_PAIRS, 1, 2 * qb), F32),
            pltpu.VMEM((N_PAIRS, VT_ROWS, 2 * qb), F32),
        ],
        compiler_params=_cparams(("parallel", "arbitrary")),
    )(q, qi, kiwi, kib, kb, vt)


PAGES_PER_DOT = 4


def _sample_attn_kernel(pt_ref, q_ref, qi_ref, kiwi_ref, kn_ref, vn_ref, cki_hbm, ck_hbm, cv_hbm, o_ref,
                        kit_buf, kt_buf, vt_buf, sem, kin_s, knew_s, vnew_s, keys_ref, bias_ref, lg_ref, j0_ref,
                        *, n_pages, dq, n_sel, idx_bits):
    b = pl.program_id(0)
    nb = pl.num_programs(0)
    slot = b % 2
    past = n_pages * LANES
    lp = past + LANES
    streams = ((cki_hbm, kit_buf), (ck_hbm, kt_buf), (cv_hbm, vt_buf))

    def page_copy(a, seq, pg, sl):
        src, dst = streams[a]
        return pltpu.make_async_copy(src.at[pt_ref[seq, pg]], dst.at[sl, pg], sem.at[sl, a])

    def fetch(seq, sl):
        def body(pg, carry):
            for a in range(len(streams)):
                page_copy(a, seq, pg, sl).start()
            return carry

        lax.fori_loop(0, n_pages, body, 0)

    @pl.when(b == 0)
    def _():
        fetch(0, 0)

    @pl.when(b + 1 < nb)
    def _():
        fetch(b + 1, 1 - slot)

    def wait_body(pg, carry):
        for a in range(len(streams)):
            page_copy(a, b, pg, slot).wait()
        return carry

    lax.fori_loop(0, n_pages, wait_body, 0)

    def pages(buf, c):
        return jnp.concatenate([buf[slot, c * PAGES_PER_DOT + r] for r in range(PAGES_PER_DOT)], axis=1).astype(BF16)

    kc = PAGES_PER_DOT * LANES
    q = q_ref[0]
    qi = qi_ref[0]
    kiwi = kiwi_ref[0]
    qrow = lax.broadcasted_iota(I32, (dq, 1), 0)
    idx = lax.broadcasted_iota(I32, (1, lp), 1)
    rel = idx - past
    adm = (idx < past) | ((rel <= qrow) & (rel < dq))

    kin_s[...] = jnp.zeros(kin_s.shape, F32)
    knew_s[...] = jnp.zeros(knew_s.shape, F32)
    vnew_s[...] = jnp.zeros(vnew_s.shape, F32)
    kin_s[0:dq, :] = kiwi[:, :IDX_DIM]
    knew_s[0:dq, :] = kn_ref[0]
    vnew_s[0:dq, :] = vn_ref[0]

    nt = (((1,), (1,)), ((), ()))
    qi_rows = jnp.concatenate([qi[:, h * IDX_DIM:(h + 1) * IDX_DIM] for h in range(IDX_HEADS)], axis=0).astype(BF16)
    wi_col = jnp.concatenate([kiwi[:, IDX_DIM + h:IDX_DIM + h + 1] for h in range(IDX_HEADS)], axis=0)

    def scores(s):
        r = jnp.maximum(s, 0.0) * wi_col
        out = r[0:dq]
        for h in range(1, IDX_HEADS):
            out = out + r[h * dq:(h + 1) * dq]
        return out

    for c in range(n_pages // PAGES_PER_DOT):
        s = jnp.dot(qi_rows, pages(kit_buf, c), preferred_element_type=F32)
        keys_ref[:, c * kc:(c + 1) * kc] = _sortable(scores(s))
    s_new = scores(lax.dot_general(qi_rows, kin_s[...].astype(BF16), nt, preferred_element_type=F32))
    keys_ref[:, past:] = _sortable(jnp.where(adm[:, past:], s_new, -jnp.inf))

    def count(pred):
        return jnp.sum(jnp.where(pred(keys_ref[...]), 1, 0).astype(I32), axis=1, keepdims=True)

    thr = _kth_largest(lambda cand: count(lambda k: k >= cand), (dq, 1), n_sel)
    need = n_sel - count(lambda k: k > thr)
    tie = count(lambda k: k >= thr) > n_sel
    j0_ref[...] = jnp.full((dq, 1), NO_BOUND, I32)

    @pl.when(jnp.max(jnp.where(tie, 1, 0)) > 0)
    def _():
        j0 = _tie_bound(lambda cand: count(lambda k: (k == thr) & (idx < cand)), need, (dq, 1), idx_bits)
        j0_ref[...] = jnp.where(tie, j0, NO_BOUND)

    j0 = j0_ref[...]
    keys = keys_ref[...]
    sel = ((keys > thr) | ((keys == thr) & (idx <= j0))) & adm
    bias_ref[...] = jnp.where(sel, 0.0, NEG_BIG)

    zero = jnp.zeros((dq, HEAD_DIM), F32)
    rows = []
    for g in range(N_KV_HEADS):
        for r in range(KV_GROUP):
            h = g * KV_GROUP + r
            piece = q[:, h * HEAD_DIM:(h + 1) * HEAD_DIM]
            rows.append(jnp.concatenate([piece, zero] if g == 0 else [zero, piece], axis=1))
    qbd = jnp.concatenate(rows, axis=0).astype(BF16)

    for c in range(n_pages // PAGES_PER_DOT):
        lg_ref[:, c * kc:(c + 1) * kc] = jnp.dot(qbd, pages(kt_buf, c), preferred_element_type=F32)
    lg_ref[:, past:] = lax.dot_general(qbd, knew_s[...].astype(BF16), nt, preferred_element_type=F32)

    bias = bias_ref[...]
    lg = lg_ref[...] + jnp.concatenate([bias] * N_HEADS, axis=0)
    m = jnp.max(lg, axis=1, keepdims=True)
    p = jnp.exp2(lg - m)
    l = jnp.sum(p, axis=1, keepdims=True)
    lg_ref[...] = p
    o = jnp.dot(lg_ref[:, past:].astype(BF16), vnew_s[...].astype(BF16), preferred_element_type=F32)
    for c in range(n_pages // PAGES_PER_DOT):
        o = o + lax.dot_general(lg_ref[:, c * kc:(c + 1) * kc].astype(BF16), pages(vt_buf, c), nt,
                                preferred_element_type=F32)
    o_ref[0] = o * (1.0 / l)


def _sample_attn_call(page_table, q, qi, kiwi, kn, vn, cki_t, ck_t, cv_t):
    nb, dq, _ = q.shape
    n_pages = page_table.shape[1]
    assert cki_t.shape[2] == LANES and n_pages % PAGES_PER_DOT == 0
    lp = (n_pages + 1) * LANES
    n_sel = min(INDEX_TOPK, (n_pages * LANES + dq) // 4)
    b3 = lambda b, pt: (b, 0, 0)
    kern = functools.partial(_sample_attn_kernel, n_pages=n_pages, dq=dq, n_sel=n_sel,
                             idx_bits=max(1, (lp - 1).bit_length()))
    grid_spec = pltpu.PrefetchScalarGridSpec(
        num_scalar_prefetch=1,
        grid=(nb,),
        in_specs=[
            pl.BlockSpec((1, dq, Q_W), b3),
            pl.BlockSpec((1, dq, QI_W), b3),
            pl.BlockSpec((1, dq, LANES), b3),
            pl.BlockSpec((1, dq, KV_W), b3),
            pl.BlockSpec((1, dq, KV_W), b3),
            pl.BlockSpec(memory_space=pl.ANY),
            pl.BlockSpec(memory_space=pl.ANY),
            pl.BlockSpec(memory_space=pl.ANY),
        ],
        out_specs=pl.BlockSpec((1, N_HEADS * dq, KV_W), b3),
        scratch_shapes=[
            pltpu.VMEM((2, n_pages, IDX_DIM, LANES), F32),
            pltpu.VMEM((2, n_pages, KV_W, LANES), F32),
            pltpu.VMEM((2, n_pages, KV_W, LANES), F32),
            pltpu.SemaphoreType.DMA((2, 3)),
            pltpu.VMEM((LANES, IDX_DIM), F32),
            pltpu.VMEM((LANES, KV_W), F32),
            pltpu.VMEM((LANES, KV_W), F32),
            pltpu.VMEM((dq, lp), I32),
            pltpu.VMEM((dq, lp), F32),
            pltpu.VMEM((N_HEADS * dq, lp), F32),
            pltpu.VMEM((dq, 1), I32),
        ],
    )
    return pl.pallas_call(
        kern,
        grid_spec=grid_spec,
        out_shape=jax.ShapeDtypeStruct((nb, N_HEADS * dq, KV_W), F32),
        compiler_params=_cparams(("arbitrary",)),
    )(page_table, q, qi, kiwi, kn, vn, cki_t, ck_t, cv_t)


def _layer_norm(x, g, b):
    mu = jnp.mean(x, axis=-1, keepdims=True)
    xc = x - mu
    var = jnp.mean(xc * xc, axis=-1, keepdims=True)
    return xc * lax.rsqrt(var + LN_EPS) * g + b


HIST_ROWS = 32
HIST_PAD = HIST_ROWS - (CONV_WIDTH - 1)
CONV_ROWS = 64


def _mix_kernel(attn_ref, u_ref, uprev_ref, hist_ref, x_ref, cw_ref, cb_ref, cg_ref, cbb_ref, wo_ref, bo_ref,
                g1_ref, b1_ref, wr_ref, br_ref, h_ref, hb_ref, route_ref, xp_ref, conv_ref, *, tm, nseq, use_prev):
    rows = tm // nseq
    ch = u_ref.shape[1]

    if use_prev:
        first = pl.program_id(1) == 0

        @pl.when(first)
        def _():
            xp_ref[:, 0:HIST_ROWS, :] = hist_ref[...]

        @pl.when(jnp.logical_not(first))
        def _():
            xp_ref[0, 0:HIST_ROWS, :] = uprev_ref[...]
    else:
        xp_ref[:, 0:HIST_ROWS, :] = hist_ref[...]
    xp_ref[:, HIST_ROWS:, :] = u_ref[...].reshape(nseq, rows, ch)

    rs = min(rows, CONV_ROWS)

    def seq_body(s, carry):
        for r0 in range(0, rows, rs):
            acc = jnp.zeros((rs, ch), F32)
            for j in range(CONV_WIDTH):
                lo = r0 + HIST_PAD + j
                acc = acc + cw_ref[j:j + 1, :] * xp_ref[s, lo:lo + rs, :]
            conv_ref[pl.ds(pl.multiple_of(s * rows + r0, SUBLANES), rs), :] = acc
        return carry

    if nseq == 1:
        seq_body(0, 0)
    else:
        lax.fori_loop(0, nseq, seq_body, 0)

    y = _layer_norm(conv_ref[...] + cb_ref[...], cg_ref[...], cbb_ref[...])
    conv = (y * jax.nn.sigmoid(y)).astype(BF16)
    aw = attn_ref.shape[1]
    mixed = (jnp.dot(attn_ref[...].astype(BF16), wo_ref[0:aw, :], preferred_element_type=F32)
             + jnp.dot(conv, wo_ref[aw:, :], preferred_element_type=F32) + bo_ref[...])
    h = _layer_norm(DN_ALPHA * x_ref[...] + mixed, g1_ref[...], b1_ref[...])
    h_ref[...] = h
    hb = h.astype(BF16)
    hb_ref[...] = hb

    logits = jnp.dot(hb, wr_ref[...], preferred_element_type=F32) + br_ref[...]
    lane = lax.broadcasted_iota(I32, (1, LANES), 1)
    work = logits
    vals, ids = [], []
    for _ in range(TOP_K):
        mx = jnp.max(work, axis=1, keepdims=True)
        ix = jnp.min(jnp.where(work == mx, lane, LANES), axis=1, keepdims=True)
        vals.append(mx)
        ids.append(ix)
        work = jnp.where(lane == ix, -jnp.inf, work)
    ex = [jnp.exp(v - vals[0]) for v in vals]
    den = ex[0]
    for e in ex[1:]:
        den = den + e
    route = jnp.zeros((tm, LANES), F32)
    for k in range(TOP_K):
        route = jnp.where(lane == k, ex[k] / den, route)
        route = jnp.where(lane == TOP_K + k, ids[k].astype(F32), route)
    route_ref[...] = route


def _mix_call(attn, u, hist, x, weights, *, row_off, tm, nseq, grid):
    cw, cb, cg, cbb, wo, bo, g1, b1, wr, br = weights
    n_rows, d = x.shape
    ch = u.shape[1]
    use_prev = grid[1] > 1
    tiles_per_seq = grid[1]
    off_t = row_off // tm
    tile = lambda b, i: (b * tiles_per_seq + i, 0)
    otile = lambda b, i: (off_t + b * tiles_per_seq + i, 0)
    const = lambda b, i: (0, 0)
    if use_prev:
        per_tile = tm // HIST_ROWS
        prev = lambda b, i: (jnp.maximum((off_t + b * tiles_per_seq + i) * per_tile - 1, 0), 0)
    else:
        prev = const
    in_specs = [
        pl.BlockSpec((tm, attn.shape[1]), tile),
        pl.BlockSpec((tm, ch), otile),
        pl.BlockSpec((HIST_ROWS, ch), prev),
        pl.BlockSpec((nseq, HIST_ROWS, ch), lambda b, i: (b, 0, 0)),
        pl.BlockSpec((tm, d), tile),
        pl.BlockSpec(cw.shape, const),
        pl.BlockSpec(cb.shape, const),
        pl.BlockSpec(cg.shape, const),
        pl.BlockSpec(cbb.shape, const),
        pl.BlockSpec(wo.shape, const),
        pl.BlockSpec(bo.shape, const),
        pl.BlockSpec(g1.shape, const),
        pl.BlockSpec(b1.shape, const),
        pl.BlockSpec(wr.shape, const),
        pl.BlockSpec(br.shape, const),
    ]
    kern = functools.partial(_mix_kernel, tm=tm, nseq=nseq, use_prev=use_prev)
    return pl.pallas_call(
        kern,
        grid=grid,
        in_specs=in_specs,
        out_specs=(
            pl.BlockSpec((tm, d), tile),
            pl.BlockSpec((tm, d), tile),
            pl.BlockSpec((tm, LANES), tile),
        ),
        out_shape=(
            jax.ShapeDtypeStruct((n_rows, d), F32),
            jax.ShapeDtypeStruct((n_rows, d), BF16),
            jax.ShapeDtypeStruct((n_rows, LANES), F32),
        ),
        scratch_shapes=[
            pltpu.VMEM((nseq, HIST_ROWS + tm // nseq, ch), F32),
            pltpu.VMEM((tm, ch), F32),
        ],
        compiler_params=_cparams(("parallel", "arbitrary")),
    )(attn, u, u, hist, x, cw, cb, cg, cbb, wo, bo, g1, b1, wr, br)


CAST_ROWS = 128


def _moe_kernel(be_ref, nu_ref, x_ref, wgu_ref, bgu_ref, wd_ref, bd_ref, y_ref, wgu_s, wd_s):
    i = pl.program_id(0)
    used = i < nu_ref[0]
    changed = (i == 0) | (be_ref[i] != be_ref[jnp.maximum(i - 1, 0)])
    dff = wd_s.shape[0]

    @pl.when(used & changed)
    def _():
        def cast_gu(r, carry):
            r0 = pl.multiple_of(r * CAST_ROWS, CAST_ROWS)
            wgu_s[pl.ds(r0, CAST_ROWS), :] = wgu_ref[0, pl.ds(r0, CAST_ROWS), :].astype(BF16)
            return carry

        def cast_d(r, carry):
            r0 = pl.multiple_of(r * CAST_ROWS, CAST_ROWS)
            wd_s[pl.ds(r0, CAST_ROWS), :] = wd_ref[0, pl.ds(r0, CAST_ROWS), :].astype(BF16)
            return carry

        lax.fori_loop(0, wgu_s.shape[0] // CAST_ROWS, cast_gu, 0)
        lax.fori_loop(0, wd_s.shape[0] // CAST_ROWS, cast_d, 0)

    @pl.when(used)
    def _():
        gu = jnp.dot(x_ref[...], wgu_s[...], preferred_element_type=F32) + bgu_ref[0]
        gate = jnp.minimum(gu[:, :dff], SWIGLU_LIMIT)
        up = jnp.clip(gu[:, dff:], -SWIGLU_LIMIT, SWIGLU_LIMIT)
        act = (up + 1.0) * gate * jax.nn.sigmoid(SWIGLU_ALPHA * gate)
        y = jnp.dot(act.astype(BF16), wd_s[...], preferred_element_type=F32) + bd_ref[0]
        y_ref[...] = y.astype(y_ref.dtype)

    @pl.when(jnp.logical_not(used))
    def _():
        y_ref[...] = jnp.zeros(y_ref.shape, y_ref.dtype)


def _moe_call(block_e, n_used, x_pad, wgu, bgu, wd, bd, bm):
    p, d = x_pad.shape
    ne, _, dgu = wgu.shape
    dff = wd.shape[1]
    n_blocks = p // bm
    grid_spec = pltpu.PrefetchScalarGridSpec(
        num_scalar_prefetch=2,
        grid=(n_blocks,),
        in_specs=[
            pl.BlockSpec((bm, d), lambda i, be, nu: (i, 0)),
            pl.BlockSpec((1, d, dgu), lambda i, be, nu: (be[i], 0, 0)),
            pl.BlockSpec((1, 1, dgu), lambda i, be, nu: (be[i], 0, 0)),
            pl.BlockSpec((1, dff, d), lambda i, be, nu: (be[i], 0, 0)),
            pl.BlockSpec((1, 1, d), lambda i, be, nu: (be[i], 0, 0)),
        ],
        out_specs=pl.BlockSpec((bm, d), lambda i, be, nu: (i, 0)),
        scratch_shapes=[pltpu.VMEM((d, dgu), BF16), pltpu.VMEM((dff, d), BF16)],
    )
    return pl.pallas_call(
        _moe_kernel,
        grid_spec=grid_spec,
        out_shape=jax.ShapeDtypeStruct((p, d), BF16),
        compiler_params=_cparams(("arbitrary",)),
    )(block_e, n_used, x_pad, wgu, bgu.reshape(ne, 1, dgu), wd, bd.reshape(ne, 1, d))


def _dispatch_plan(route, bm):
    n = route.shape[0]
    nk = n * TOP_K
    experts = jnp.arange(N_EXPERTS, dtype=I32)
    e_flat = route[:, TOP_K:2 * TOP_K].astype(I32).reshape(-1)
    iota = jnp.arange(nk, dtype=I32)
    e_sorted, order = lax.sort((e_flat, iota), num_keys=1, is_stable=True)
    _, inverse = lax.sort((order, iota), num_keys=1)
    counts = jnp.sum((e_flat[:, None] == experts[None, :]).astype(I32), axis=0)
    padded = (counts + bm - 1) // bm * bm
    start = jnp.cumsum(counts) - counts
    pend = jnp.cumsum(padded)
    pstart = pend - padded
    slot_sorted = pstart[e_sorted] + iota - start[e_sorted]
    slot_orig = slot_sorted[inverse]
    n_blocks = -(-(nk + N_EXPERTS * (bm - 1)) // bm)
    block_start = jnp.arange(n_blocks, dtype=I32) * bm
    block_e = jnp.minimum(jnp.sum((pend[None, :] <= block_start[:, None]).astype(I32), axis=1), N_EXPERTS - 1)
    n_used = (pend[-1:] // bm).astype(I32)
    e_slot = jnp.repeat(block_e, bm)
    rank = jnp.arange(n_blocks * bm, dtype=I32) - pstart[e_slot]
    src = jnp.clip(start[e_slot] + rank, 0, nk - 1)
    slot_tok = jnp.where(rank < counts[e_slot], order[src] // TOP_K, 0)
    return slot_tok, block_e, n_used, slot_orig


def _final_kernel(h_ref, yg_ref, route_ref, pe_ref, g2_ref, b2_ref, wg_ref, bg_ref, wp_ref, g3_ref, b3_ref, o_ref):
    route = route_ref[...]
    y = route[:, 0:1] * yg_ref[0].astype(F32)
    for k in range(1, TOP_K):
        y = y + route[:, k:k + 1] * yg_ref[k].astype(F32)
    h2 = _layer_norm(DN_ALPHA * h_ref[...] + y, g2_ref[...], b2_ref[...])
    gate = jax.nn.sigmoid(jnp.dot(h2.astype(BF16), wg_ref[...], preferred_element_type=F32) + bg_ref[...])
    proj = jnp.dot(pe_ref[...].astype(BF16), wp_ref[...], preferred_element_type=F32)
    o_ref[...] = _layer_norm(DN_ALPHA * h2 + gate * proj, g3_ref[...], b3_ref[...])


def _final_call(h, yg, route, pe, weights, tm):
    g2, b2, wg, bg, wp, g3, b3 = weights
    n, pd = pe.shape
    d = h.shape[1]
    row = lambda i: (i, 0)
    const = lambda i: (0, 0)
    return pl.pallas_call(
        _final_kernel,
        grid=(n // tm,),
        in_specs=[
            pl.BlockSpec((tm, d), row),
            pl.BlockSpec((TOP_K, tm, d), lambda i: (0, i, 0)),
            pl.BlockSpec((tm, LANES), row),
            pl.BlockSpec((tm, pd), row),
            pl.BlockSpec(g2.shape, const),
            pl.BlockSpec(b2.shape, const),
            pl.BlockSpec(wg.shape, const),
            pl.BlockSpec(bg.shape, const),
            pl.BlockSpec(wp.shape, const),
            pl.BlockSpec(g3.shape, const),
            pl.BlockSpec(b3.shape, const),
        ],
        out_specs=pl.BlockSpec((tm, d), row),
        out_shape=jax.ShapeDtypeStruct((n, d), F32),
        compiler_params=_cparams(("parallel",)),
    )(h, yg, route, pe, g2, b2, wg, bg, wp, g3, b3)


TOKEN_TILE = 512
QUERY_BLOCK = 128
MOE_ROWS = 512
MOE_ROWS_SAMPLE = 128


def _rope_tables(pos):
    rd = 2 * ROPE_HALF
    inv_freq = ROPE_THETA ** (-jnp.arange(ROPE_HALF, dtype=F32) * 2.0 / rd)
    ang = pos.astype(F32)[:, None] * inv_freq[None, :]
    cos, sin = jnp.cos(ang), jnp.sin(ang)
    ones = jnp.ones((pos.shape[0], HEAD_DIM - rd), F32)
    cos_h = jnp.concatenate([cos, cos, ones], axis=1)
    sin_h = jnp.concatenate([-sin, sin, 0.0 * ones], axis=1)
    reps = LANES // HEAD_DIM
    return jnp.tile(cos_h, (1, reps)), jnp.tile(sin_h, (1, reps))


def _layer(x_prompt, x_sample, p_prompt, p_sample, cache_k, cache_v, cache_kidx, state_conv, page_table,
           w_in, b_in, w_o, b_o, ln1_g, ln1_b, conv_w, conv_b, conv_ln_g, conv_ln_b,
           w_router, b_router, w_gate_up, b_gate_up, w_down, b_down, ln2_g, ln2_b,
           w_ple_gate, b_ple_gate, w_ple_proj, ln3_g, ln3_b):
    nb, t, d = x_prompt.shape
    db, dq, _ = x_sample.shape
    n_pages, page = page_table.shape[1], cache_k.shape[1]
    past = n_pages * page
    ch = conv_w.shape[1]
    tm = min(TOKEN_TILE, t)
    np_rows, ns_rows = nb * t, db * dq
    n = np_rows + ns_rows
    assert t % tm == 0 and ns_rows % tm == 0 and tm % dq == 0 and dq == SUBLANES
    row2 = lambda a: a.reshape(1, -1)

    n_head = Q_W + 2 * KV_W + QI_W + IDX_DIM + IDX_HEADS
    pad = LANES - IDX_DIM - IDX_HEADS
    w_pad = jnp.concatenate([w_in[:, :n_head], jnp.zeros((d, pad), F32), w_in[:, n_head:]], axis=1).astype(BF16)
    b_pad = jnp.concatenate([b_in[:n_head], jnp.zeros((pad,), F32), b_in[n_head:]]).reshape(1, -1)
    pos = jnp.concatenate([jnp.arange(t, dtype=I32), past + (jnp.arange(tm, dtype=I32) % dq)])
    cos_tab, sin_tab = _rope_tables(pos)
    tiles_p = np_rows // tm
    tab_index = lambda i: jnp.where(i < tiles_p, i % (t // tm), t // tm)
    q, k_f, v_f, k_b, v_t, qi, kiwi, ki_b, u = _inproj_call(
        x_prompt.reshape(np_rows, d), x_sample.reshape(ns_rows, d), w_pad, b_pad, cos_tab, sin_tab, tab_index, tm)

    attn_p = _prompt_attn_call(q, qi, kiwi, ki_b, k_b, v_t, nb, t, min(QUERY_BLOCK, t), tm)

    wr_pad = jnp.concatenate([w_router, jnp.zeros((d, LANES - N_EXPERTS), F32)], axis=1).astype(BF16)
    br_pad = jnp.concatenate([b_router, jnp.full((LANES - N_EXPERTS,), -jnp.inf, F32)]).reshape(1, -1)
    mix_w = (conv_w, row2(conv_b), row2(conv_ln_g), row2(conv_ln_b), w_o.astype(BF16), row2(b_o),
             row2(ln1_g), row2(ln1_b), wr_pad, br_pad)
    hist_p = jnp.zeros((nb, HIST_ROWS, ch), F32)
    hist_s = jnp.concatenate([jnp.zeros((db, HIST_PAD, ch), F32), state_conv], axis=1)
    fin_w = (row2(ln2_g), row2(ln2_b), w_ple_gate.astype(BF16), row2(b_ple_gate), w_ple_proj.astype(BF16),
             row2(ln3_g), row2(ln3_b))

    def experts(h, h_b, route, bm):
        slot_tok, block_e, n_used, slot_orig = _dispatch_plan(route, bm)
        y_pad = _moe_call(block_e, n_used, h_b[slot_tok], w_gate_up, b_gate_up, w_down, b_down, bm)
        return y_pad[slot_orig.reshape(h.shape[0], TOP_K).T]

    h_p, hb_p, route_p = _mix_call(attn_p, u, hist_p, x_prompt.reshape(np_rows, d), mix_w, row_off=0, tm=tm,
                                   nseq=1, grid=(nb, t // tm))
    yg_p = experts(h_p, hb_p, route_p, MOE_ROWS)

    smp = lambda a: a[np_rows:].astype(F32).reshape(db, dq, -1)
    n_pool = cache_k.shape[0]
    kv_t = lambda pool: pool.transpose(0, 2, 3, 1).reshape(n_pool, KV_W, page)
    o_s = _sample_attn_call(page_table, smp(q), smp(qi), smp(kiwi), smp(k_f), smp(v_f),
                            cache_kidx.transpose(0, 2, 1), kv_t(cache_k), kv_t(cache_v))
    o_s = o_s.reshape(db, N_KV_HEADS, KV_GROUP, dq, N_KV_HEADS, HEAD_DIM)
    attn_s = jnp.stack([o_s[:, g, :, :, g, :] for g in range(N_KV_HEADS)], axis=1)
    attn_s = attn_s.transpose(0, 3, 1, 2, 4).reshape(ns_rows, Q_W)
    h_s, hb_s, route_s = _mix_call(attn_s, u, hist_s, x_sample.reshape(ns_rows, d), mix_w, row_off=np_rows, tm=tm,
                                   nseq=tm // dq, grid=(ns_rows // tm, 1))
    yg_s = experts(h_s, hb_s, route_s, MOE_ROWS_SAMPLE)
    y_p = _final_call(h_p, yg_p, route_p, p_prompt.reshape(np_rows, -1), fin_w, tm)
    y_s = _final_call(h_s, yg_s, route_s, p_sample.reshape(ns_rows, -1), fin_w, tm)

    u_p = u[:np_rows].reshape(nb, t, ch)
    u_s = u[np_rows:].reshape(db, dq, ch)
    keep = CONV_WIDTH - 1
    conv_p = jnp.concatenate([jnp.zeros((nb, keep, ch), F32), u_p], axis=1)[:, -keep:]
    conv_s = jnp.concatenate([state_conv, u_s], axis=1)[:, -keep:]
    kv = lambda a, lo, hi, b_, t_: a[lo:hi].reshape(b_, t_, N_KV_HEADS, HEAD_DIM)
    return (y_p.reshape(nb, t, d), y_s.reshape(db, dq, d),
            kv(k_f, 0, np_rows, nb, t), kv(v_f, 0, np_rows, nb, t),
            kiwi[:np_rows, :IDX_DIM].reshape(nb, t, IDX_DIM), conv_p,
            kv(k_f, np_rows, n, db, dq), kv(v_f, np_rows, n, db, dq),
            kiwi[np_rows:, :IDX_DIM].reshape(db, dq, IDX_DIM), conv_s)


def kernel(x_prompt, x_sample, p_prompt, p_sample, cache_k, cache_v, cache_kidx, state_conv, page_table, w_in, b_in, w_o, b_o, ln1_g, ln1_b, conv_w, conv_b, conv_ln_g, conv_ln_b, w_router, b_router, w_gate_up, b_gate_up, w_down, b_down, ln2_g, ln2_b, w_ple_gate, b_ple_gate, w_ple_proj, ln3_g, ln3_b):
    assert w_in.shape[0] == DEPTH
    outs = _layer(x_prompt, x_sample, p_prompt[0], p_sample[0], cache_k[0], cache_v[0], cache_kidx[0],
                  state_conv[0], page_table, w_in[0], b_in[0], w_o[0], b_o[0], ln1_g[0], ln1_b[0],
                  conv_w[0], conv_b[0], conv_ln_g[0], conv_ln_b[0], w_router[0], b_router[0],
                  w_gate_up[0], b_gate_up[0], w_down[0], b_down[0], ln2_g[0], ln2_b[0],
                  w_ple_gate[0], b_ple_gate[0], w_ple_proj[0], ln3_g[0], ln3_b[0])
    y_p, y_s = outs[0], outs[1]
    return (y_p, y_s) + tuple(o[None] for o in outs[2:])
```

```python
import functools
import math

import jax
import jax.numpy as jnp
from jax import lax
from jax.experimental import pallas as pl
from jax.experimental.pallas import tpu as pltpu

F32 = jnp.float32
BF16 = jnp.bfloat16
I32 = jnp.int32

HEAD_DIM = 64
N_HEADS = 8
N_KV_HEADS = 2
KV_GROUP = N_HEADS // N_KV_HEADS
IDX_HEADS = 4
IDX_DIM = 64
INDEX_TOPK = 256
ROPE_THETA = 500000.0
ROPE_HALF = HEAD_DIM // 8
CONV_WIDTH = 31
N_EXPERTS = 32
TOP_K = 4
SWIGLU_LIMIT = 7.0
SWIGLU_ALPHA = 1.702
LN_EPS = 1e-5
DEPTH = 1
DN_ALPHA = (2 * DEPTH) ** 0.25
ATTN_SCALE = HEAD_DIM ** -0.5
IDX_W_SCALE = (IDX_HEADS * IDX_DIM) ** -0.5
LOG2E = math.log2(math.e)

LANES = 128
SUBLANES = 8
VMEM_LIMIT = 56 * 1024 * 1024
NEG_BIG = -1e30
INT_MIN = -(2 ** 31)

Q_W = N_HEADS * HEAD_DIM
KV_W = N_KV_HEADS * HEAD_DIM
QI_W = IDX_HEADS * IDX_DIM
VT_ROWS = HEAD_DIM + 16


def _cparams(sem):
    return pltpu.CompilerParams(dimension_semantics=sem, vmem_limit_bytes=VMEM_LIMIT)


def _rope_slab(z, cos, sin):
    lane = lax.broadcasted_iota(I32, (1, LANES), 1)
    first = (lane % HEAD_DIM) < ROPE_HALF
    up = pltpu.roll(z, LANES - ROPE_HALF, axis=1)
    dn = pltpu.roll(z, ROPE_HALF, axis=1)
    return z * cos + jnp.where(first, up, dn) * sin


def _inproj_kernel(xp_ref, xs_ref, w_ref, b_ref, cos_ref, sin_ref,
                   q_ref, kf_ref, vf_ref, kb_ref, vt_ref, qi_ref, kiwi_ref, kib_ref, u_ref, *, conv_ch, tiles_p):
    x = jnp.where(pl.program_id(0) < tiles_p, xp_ref[...], xs_ref[...]).astype(BF16)
    cos = cos_ref[...]
    sin = sin_ref[...]

    def proj(lo, width):
        return jnp.dot(x, w_ref[:, lo:lo + width], preferred_element_type=F32) + b_ref[:, lo:lo + width]

    def rope(z):
        n = z.shape[1] // LANES
        slabs = [_rope_slab(z[:, s * LANES:(s + 1) * LANES], cos, sin) for s in range(n)]
        return slabs[0] if n == 1 else jnp.concatenate(slabs, axis=1)

    o = 0
    q = rope(proj(o, Q_W))
    q_ref[...] = (q * (ATTN_SCALE * LOG2E)).astype(BF16)
    o += Q_W
    k = rope(proj(o, KV_W))
    kf_ref[...] = k
    kb_ref[...] = k.astype(BF16)
    o += KV_W
    v = proj(o, KV_W)
    vf_ref[...] = v
    vt = v.T
    ones = jnp.ones((VT_ROWS - HEAD_DIM, vt.shape[1]), F32)
    vt_ref[0] = jnp.concatenate(
        [piece for g in range(N_KV_HEADS) for piece in (vt[g * HEAD_DIM:(g + 1) * HEAD_DIM], ones)],
        axis=0).astype(BF16)
    o += KV_W
    qi_ref[...] = rope(proj(o, QI_W)).astype(BF16)
    o += QI_W
    lane = lax.broadcasted_iota(I32, (1, LANES), 1)
    is_ki = lane < IDX_DIM
    kiwi = _rope_slab(proj(o, LANES), jnp.where(is_ki, cos, 1.0), jnp.where(is_ki, sin, 0.0))
    kiwi = kiwi * jnp.where((lane >= IDX_DIM) & (lane < IDX_DIM + IDX_HEADS), IDX_W_SCALE, 1.0)
    kiwi_ref[...] = kiwi
    kib_ref[...] = kiwi[:, :IDX_DIM].astype(BF16)
    o += LANES
    a = proj(o, conv_ch)
    g = proj(o + conv_ch, conv_ch)
    u_ref[...] = a * jax.nn.sigmoid(g)


def _inproj_call(x_p, x_s, w_pad, b_pad, cos_tab, sin_tab, tab_index, tm):
    d = x_p.shape[1]
    tiles_p, tiles_s = x_p.shape[0] // tm, x_s.shape[0] // tm
    nt = tiles_p + tiles_s
    n = nt * tm
    nw = w_pad.shape[1]
    conv_ch = (nw - (Q_W + 2 * KV_W + QI_W + LANES)) // 2
    row = lambda i: (i, 0)
    const = lambda i: (0, 0)
    out_shapes = (
        jax.ShapeDtypeStruct((n, Q_W), BF16),
        jax.ShapeDtypeStruct((n, KV_W), F32),
        jax.ShapeDtypeStruct((n, KV_W), F32),
        jax.ShapeDtypeStruct((n, KV_W), BF16),
        jax.ShapeDtypeStruct((nt, N_KV_HEADS * VT_ROWS, tm), BF16),
        jax.ShapeDtypeStruct((n, QI_W), BF16),
        jax.ShapeDtypeStruct((n, LANES), F32),
        jax.ShapeDtypeStruct((n, IDX_DIM), BF16),
        jax.ShapeDtypeStruct((n, conv_ch), F32),
    )
    out_specs = (
        pl.BlockSpec((tm, Q_W), row),
        pl.BlockSpec((tm, KV_W), row),
        pl.BlockSpec((tm, KV_W), row),
        pl.BlockSpec((tm, KV_W), row),
        pl.BlockSpec((1, N_KV_HEADS * VT_ROWS, tm), lambda i: (i, 0, 0)),
        pl.BlockSpec((tm, QI_W), row),
        pl.BlockSpec((tm, LANES), row),
        pl.BlockSpec((tm, IDX_DIM), row),
        pl.BlockSpec((tm, conv_ch), row),
    )
    return pl.pallas_call(
        functools.partial(_inproj_kernel, conv_ch=conv_ch, tiles_p=tiles_p),
        grid=(nt,),
        in_specs=[
            pl.BlockSpec((tm, d), lambda i: (jnp.minimum(i, tiles_p - 1), 0)),
            pl.BlockSpec((tm, d), lambda i: (jnp.maximum(i - tiles_p, 0), 0)),
            pl.BlockSpec((d, nw), const),
            pl.BlockSpec((1, nw), const),
            pl.BlockSpec((tm, LANES), lambda i: (tab_index(i), 0)),
            pl.BlockSpec((tm, LANES), lambda i: (tab_index(i), 0)),
        ],
        out_specs=out_specs,
        out_shape=out_shapes,
        compiler_params=_cparams(("parallel",)),
    )(x_p, x_s, w_pad, b_pad, cos_tab, sin_tab)


def _sortable(score):
    bits = lax.bitcast_convert_type(score, I32)
    key = bits ^ ((bits >> 31) & 0x7FFFFFFF)
    return jnp.where(score == 0.0, 0, key)


def _kth_largest(count_ge, shape, n_sel):
    def bit_body(t, thr):
        cand = thr + jnp.left_shift(jnp.int32(1), 31 - t)
        return jnp.where(count_ge(cand) >= n_sel, cand, thr)

    return lax.fori_loop(0, 32, bit_body, jnp.full(shape, INT_MIN, I32))


def _tie_bound(count_eq_below, need, shape, idx_bits):
    def bit_body(t, j0):
        cand = j0 + jnp.left_shift(jnp.int32(1), idx_bits - 1 - t)
        return jnp.where(count_eq_below(cand) < need, cand, j0)

    return lax.fori_loop(0, idx_bits, bit_body, jnp.zeros(shape, I32))


NO_BOUND = 2 ** 30
PLANE_KEYS = 32 * SUBLANES


def _bit_transpose32(rows):
    a = list(rows[::-1])
    j, m = 16, 0x0000FFFF
    while j:
        k = 0
        while k < 32:
            t = (a[k] ^ (a[k + j] >> j)) & m
            a[k] = a[k] ^ t
            a[k + j] = a[k + j] ^ (t << j)
            k = (k + j + 1) & ~j
        j >>= 1
        m = (m ^ (m << j)) & 0xFFFFFFFF
    return a[::-1]


def _plane_order(n):
    p = jnp.arange(n, dtype=I32)
    within = p % PLANE_KEYS
    return p - within + (within % SUBLANES) * 32 + within // SUBLANES


N_PAIRS = N_HEADS // 2
PAIRS_PER_GROUP = KV_GROUP // 2


def _prompt_attn_kernel(q_ref, qi_ref, kiwi_ref, kib_ref, koff_ref, kb_ref, vt_ref, o_ref,
                        qst_ref, qit_ref, pln_ref, cand_ref, top_ref, sel_ref, bias_ref, j0_ref,
                        lg0_ref, lg1_ref, mx0_ref, mx1_ref, pr0_ref, pr1_ref, al0_ref, al1_ref, m_ref, acc_ref,
                        *, qb, kc, n_sel, idx_bits):
    i = pl.program_id(1)
    n_chunks = (i * qb + qb + kc - 1) // kc
    q_pos = i * qb + lax.broadcasted_iota(I32, (1, qb), 1)

    def key_pos(k0):
        return k0 + lax.broadcasted_iota(I32, (kc, 1), 0)

    qt = q_ref[...].astype(F32).T
    zero = jnp.zeros((HEAD_DIM, 2 * qb), F32)
    for p in range(N_PAIRS):
        g = p // PAIRS_PER_GROUP
        pair = jnp.concatenate([qt[(2 * p) * HEAD_DIM:(2 * p + 1) * HEAD_DIM],
                                qt[(2 * p + 1) * HEAD_DIM:(2 * p + 2) * HEAD_DIM]], axis=1)
        qst_ref[p] = jnp.concatenate([pair if gg == g else zero for gg in range(N_KV_HEADS)], axis=0).astype(BF16)
    qit = qi_ref[...].astype(F32).T
    qit_ref[...] = jnp.concatenate(
        [qit[h * IDX_DIM:(h + 1) * IDX_DIM] for h in range(IDX_HEADS)], axis=1).astype(BF16)
    wit = kiwi_ref[...].T[IDX_DIM:IDX_DIM + SUBLANES]

    planes_per_chunk = kc // PLANE_KEYS

    @pl.when(i == 0)
    def _():
        pln_ref[...] = jnp.zeros(pln_ref.shape, I32)

    def score_body(c, carry):
        k0 = pl.multiple_of(c * kc, kc)
        s = jnp.dot(kib_ref[pl.ds(k0, kc), :], qit_ref[...], preferred_element_type=F32)
        score = jnp.zeros((kc, qb), F32)
        for h in range(IDX_HEADS):
            score = score + wit[h:h + 1, :] * jnp.maximum(s[:, h * qb:(h + 1) * qb], 0.0)
        score = jnp.where(k0 + koff_ref[...] <= q_pos, score, -jnp.inf)
        key = _sortable(score)
        for hb in range(planes_per_chunk):
            base = hb * PLANE_KEYS
            planes = _bit_transpose32([key[base + j * SUBLANES:base + (j + 1) * SUBLANES] for j in range(32)])
            planes[31] = ~planes[31]
            r0 = pl.multiple_of((c * planes_per_chunk + hb) * SUBLANES, SUBLANES)
            for b in range(32):
                pln_ref[b, pl.ds(r0, SUBLANES), :] = planes[b]
        return carry

    lax.fori_loop(0, n_chunks, score_body, 0)

    n_rows = pln_ref.shape[1]
    row = lax.broadcasted_iota(I32, (n_rows, 1), 0)
    cand_ref[...] = jnp.broadcast_to(jnp.where(row < n_chunks * (kc // 32), -1, 0), (n_rows, qb))
    top_ref[...] = jnp.zeros((n_rows, qb), I32)

    def bit_count(words):
        cnt = lax.population_count(words)
        return jnp.sum(jnp.sum(cnt.reshape(n_rows // SUBLANES, SUBLANES, qb), axis=0), axis=0, keepdims=True)

    def bit_body(tb, need):
        plane = pln_ref[31 - tb]
        cand = cand_ref[...]
        ones = cand & plane
        cnt = bit_count(ones)
        take = cnt >= need
        skip = jnp.where(take, 0, -1)
        top_ref[...] = top_ref[...] | (ones & skip)
        cand_ref[...] = cand & (plane ^ skip)
        return jnp.where(take, need, need - cnt)

    need = lax.fori_loop(0, 32, bit_body, jnp.full((1, qb), n_sel, I32))
    tie = bit_count(cand_ref[...]) > need

    def low_mask(bound):
        d = bound - row * 32
        return jnp.where(d >= 32, -1, jnp.left_shift(1, jnp.clip(d, 0, 31)) - 1)

    j0_ref[...] = jnp.full((1, qb), NO_BOUND, I32)

    @pl.when(jnp.max(jnp.where(tie, 1, 0)) > 0)
    def _():
        j0 = _tie_bound(lambda bound: bit_count(cand_ref[...] & low_mask(bound)), need, (1, qb), idx_bits)
        j0_ref[...] = jnp.where(tie, j0 + 1, NO_BOUND)

    sel_ref[...] = top_ref[...] | (cand_ref[...] & low_mask(j0_ref[...]))

    words_per_chunk = kc // 32

    def bias_body(c, carry):
        k0 = pl.multiple_of(c * kc, kc)
        words = sel_ref[pl.ds(pl.multiple_of(c * words_per_chunk, words_per_chunk), words_per_chunk), :]
        sub = lax.broadcasted_iota(I32, (SUBLANES, 1), 0)
        pieces = []
        for v in range(kc // SUBLANES):
            word = jnp.broadcast_to(words[v // 4:v // 4 + 1, :], (SUBLANES, qb))
            pieces.append(lax.shift_right_logical(word, (v % 4) * SUBLANES + sub) & 1)
        chosen = jnp.concatenate(pieces, axis=0) == 1
        bias_ref[pl.ds(k0, kc), :] = jnp.where(chosen & (key_pos(k0) <= q_pos), 0.0, NEG_BIG)
        return carry

    lax.fori_loop(0, n_chunks, bias_body, 0)

    m_ref[...] = jnp.full(m_ref.shape, NEG_BIG, F32)
    acc_ref[...] = jnp.zeros(acc_ref.shape, F32)
    lgs, mxs, prs, als = (lg0_ref, lg1_ref), (mx0_ref, mx1_ref), (pr0_ref, pr1_ref), (al0_ref, al1_ref)

    bias_ref[pl.ds(pl.multiple_of(n_chunks * kc, kc), kc), :] = jnp.full((kc, qb), NEG_BIG, F32)

    def logits(c, slot):
        k0 = pl.multiple_of(jnp.minimum(c, n_chunks - 1) * kc, kc)
        b0 = pl.multiple_of(jnp.minimum(c, n_chunks) * kc, kc)
        for p in range(N_PAIRS):
            lg = jnp.dot(kb_ref[pl.ds(k0, kc), :], qst_ref[p], preferred_element_type=F32)
            halves = [lg[:, h * qb:(h + 1) * qb] + bias_ref[pl.ds(b0, kc), :] for h in range(2)]
            for h in range(2):
                lgs[slot][p, :, h * qb:(h + 1) * qb] = halves[h]
            mxs[slot][p] = jnp.concatenate([jnp.max(x, axis=0, keepdims=True) for x in halves], axis=1)

    def probs(slot):
        for p in range(N_PAIRS):
            m_old = m_ref[p]
            m_new = jnp.maximum(m_old, mxs[slot][p])
            als[slot][p] = jnp.exp2(m_old - m_new)
            prs[slot][p] = jnp.exp2(lgs[slot][p] - m_new).astype(BF16)
            m_ref[p] = m_new

    def weighted(c, slot):
        cc = jnp.clip(c, 0, n_chunks - 1)
        for p in range(N_PAIRS):
            g = p // PAIRS_PER_GROUP
            vt = vt_ref[cc, g * VT_ROWS:(g + 1) * VT_ROWS, :]
            acc_ref[p] = als[slot][p] * acc_ref[p] + jnp.dot(vt, prs[slot][p], preferred_element_type=F32)

    logits(0, 0)
    pr1_ref[...] = jnp.zeros(pr1_ref.shape, BF16)
    al1_ref[...] = jnp.ones(al1_ref.shape, F32)

    def attn_body(j, carry):
        c = 2 * j
        logits(c + 1, 1)
        probs(0)
        weighted(c - 1, 1)
        logits(c + 2, 0)
        probs(1)
        weighted(c, 0)
        return carry

    lax.fori_loop(0, (n_chunks + 1) // 2, attn_body, 0)
    weighted(n_chunks - 1, 1)

    heads = []
    for p in range(N_PAIRS):
        acc = acc_ref[p]
        o = acc[:HEAD_DIM] * (1.0 / acc[HEAD_DIM:HEAD_DIM + 1])
        heads += [o[:, :qb], o[:, qb:]]
    o_ref[...] = jnp.concatenate(heads, axis=0).T.astype(o_ref.dtype)


def _prompt_attn_call(q, qi, kiwi, kib, kb, vt, nb, t, qb, kc):
    assert kc % PLANE_KEYS == 0 and t % kc == 0
    nqb = t // qb
    n_sel = min(INDEX_TOPK, t // 4)
    blk = lambda b, i: (b * nqb + i, 0)
    seq = lambda b, i: (b, 0)
    koff = jnp.broadcast_to(_plane_order(kc)[:, None], (kc, qb))
    kern = functools.partial(_prompt_attn_kernel, qb=qb, kc=kc, n_sel=n_sel, idx_bits=max(1, (t - 1).bit_length()))
    return pl.pallas_call(
        kern,
        grid=(nb, nqb),
        in_specs=[
            pl.BlockSpec((qb, Q_W), blk),
            pl.BlockSpec((qb, QI_W), blk),
            pl.BlockSpec((qb, LANES), blk),
            pl.BlockSpec((t, IDX_DIM), seq),
            pl.BlockSpec((kc, qb), lambda b, i: (0, 0)),
            pl.BlockSpec((t, KV_W), seq),
            pl.BlockSpec((t // kc, N_KV_HEADS * VT_ROWS, kc), lambda b, i: (b, 0, 0)),
        ],
        out_specs=pl.BlockSpec((qb, Q_W), blk),
        out_shape=jax.ShapeDtypeStruct((nb * t, Q_W), BF16),
        scratch_shapes=[
            pltpu.VMEM((N_PAIRS, KV_W, 2 * qb), BF16),
            pltpu.VMEM((IDX_DIM, IDX_HEADS * qb), BF16),
            pltpu.VMEM((32, t // 32, qb), I32),
            pltpu.VMEM((t // 32, qb), I32),
            pltpu.VMEM((t // 32, qb), I32),
            pltpu.VMEM((t // 32, qb), I32),
            pltpu.VMEM((t + kc, qb), F32),
            pltpu.VMEM((1, qb), I32),
            pltpu.VMEM((N_PAIRS, kc, 2 * qb), F32),
            pltpu.VMEM((N_PAIRS, kc, 2 * qb), F32),
            pltpu.VMEM((N_PAIRS, 1, 2 * qb), F32),
            pltpu.VMEM((N_PAIRS, 1, 2 * qb), F32),
            pltpu.VMEM((N_PAIRS, kc, 2 * qb), BF16),
            pltpu.VMEM((N_PAIRS, kc, 2 * qb), BF16),
            pltpu.VMEM((N_PAIRS, 1, 2 * qb), F32),
            pltpu.VMEM((N_PAIRS, 1, 2 * qb), F32),
            pltpu.VMEM((N_PAIRS, 1, 2 * qb), F32),
            pltpu.VMEM((N_PAIRS, VT_ROWS, 2 * qb), F32),
        ],
        compiler_params=_cparams(("parallel", "arbitrary")),
    )(q, qi, kiwi, kib, koff, kb, vt)


PAGES_PER_DOT = 4


def _sample_attn_kernel(pt_ref, q_ref, qi_ref, kiwi_ref, kn_ref, vn_ref, cki_hbm, ck_hbm, cv_hbm, o_ref,
                        kit_buf, kt_buf, vt_buf, sem, kin_s, knew_s, vnew_s, keys_ref, bias_ref, lg_ref, j0_ref,
                        *, n_pages, dq, n_sel, idx_bits):
    b = pl.program_id(0)
    nb = pl.num_programs(0)
    slot = b % 2
    past = n_pages * LANES
    lp = past + LANES
    streams = ((cki_hbm, kit_buf), (ck_hbm, kt_buf), (cv_hbm, vt_buf))

    def page_copy(a, seq, pg, sl):
        src, dst = streams[a]
        return pltpu.make_async_copy(src.at[pt_ref[seq, pg]], dst.at[sl, pg], sem.at[sl, a])

    def fetch(seq, sl):
        def body(pg, carry):
            for a in range(len(streams)):
                page_copy(a, seq, pg, sl).start()
            return carry

        lax.fori_loop(0, n_pages, body, 0)

    @pl.when(b == 0)
    def _():
        fetch(0, 0)

    @pl.when(b + 1 < nb)
    def _():
        fetch(b + 1, 1 - slot)

    def wait_body(pg, carry):
        for a in range(len(streams)):
            page_copy(a, b, pg, slot).wait()
        return carry

    lax.fori_loop(0, n_pages, wait_body, 0)

    def pages(buf, c):
        return jnp.concatenate([buf[slot, c * PAGES_PER_DOT + r] for r in range(PAGES_PER_DOT)], axis=1).astype(BF16)

    kc = PAGES_PER_DOT * LANES
    q = q_ref[0]
    qi = qi_ref[0]
    kiwi = kiwi_ref[0]
    qrow = lax.broadcasted_iota(I32, (dq, 1), 0)
    idx = lax.broadcasted_iota(I32, (1, lp), 1)
    rel = idx - past
    adm = (idx < past) | ((rel <= qrow) & (rel < dq))

    kin_s[...] = jnp.zeros(kin_s.shape, F32)
    knew_s[...] = jnp.zeros(knew_s.shape, F32)
    vnew_s[...] = jnp.zeros(vnew_s.shape, F32)
    kin_s[0:dq, :] = kiwi[:, :IDX_DIM]
    knew_s[0:dq, :] = kn_ref[0]
    vnew_s[0:dq, :] = vn_ref[0]

    nt = (((1,), (1,)), ((), ()))
    qi_rows = jnp.concatenate([qi[:, h * IDX_DIM:(h + 1) * IDX_DIM] for h in range(IDX_HEADS)], axis=0).astype(BF16)
    wi_col = jnp.concatenate([kiwi[:, IDX_DIM + h:IDX_DIM + h + 1] for h in range(IDX_HEADS)], axis=0)

    def scores(s):
        r = jnp.maximum(s, 0.0) * wi_col
        out = r[0:dq]
        for h in range(1, IDX_HEADS):
            out = out + r[h * dq:(h + 1) * dq]
        return out

    for c in range(n_pages // PAGES_PER_DOT):
        s = jnp.dot(qi_rows, pages(kit_buf, c), preferred_element_type=F32)
        keys_ref[:, c * kc:(c + 1) * kc] = _sortable(scores(s))
    s_new = scores(lax.dot_general(qi_rows, kin_s[...].astype(BF16), nt, preferred_element_type=F32))
    keys_ref[:, past:] = _sortable(jnp.where(adm[:, past:], s_new, -jnp.inf))

    def count(pred):
        return jnp.sum(jnp.where(pred(keys_ref[...]), 1, 0).astype(I32), axis=1, keepdims=True)

    thr = _kth_largest(lambda cand: count(lambda k: k >= cand), (dq, 1), n_sel)
    need = n_sel - count(lambda k: k > thr)
    tie = count(lambda k: k >= thr) > n_sel
    j0_ref[...] = jnp.full((dq, 1), NO_BOUND, I32)

    @pl.when(jnp.max(jnp.where(tie, 1, 0)) > 0)
    def _():
        j0 = _tie_bound(lambda cand: count(lambda k: (k == thr) & (idx < cand)), need, (dq, 1), idx_bits)
        j0_ref[...] = jnp.where(tie, j0, NO_BOUND)

    j0 = j0_ref[...]
    keys = keys_ref[...]
    sel = ((keys > thr) | ((keys == thr) & (idx <= j0))) & adm
    bias_ref[...] = jnp.where(sel, 0.0, NEG_BIG)

    zero = jnp.zeros((dq, HEAD_DIM), F32)
    rows = []
    for g in range(N_KV_HEADS):
        for r in range(KV_GROUP):
            h = g * KV_GROUP + r
            piece = q[:, h * HEAD_DIM:(h + 1) * HEAD_DIM]
            rows.append(jnp.concatenate([piece, zero] if g == 0 else [zero, piece], axis=1))
    qbd = jnp.concatenate(rows, axis=0).astype(BF16)

    for c in range(n_pages // PAGES_PER_DOT):
        lg_ref[:, c * kc:(c + 1) * kc] = jnp.dot(qbd, pages(kt_buf, c), preferred_element_type=F32)
    lg_ref[:, past:] = lax.dot_general(qbd, knew_s[...].astype(BF16), nt, preferred_element_type=F32)

    bias = bias_ref[...]
    lg = lg_ref[...] + jnp.concatenate([bias] * N_HEADS, axis=0)
    m = jnp.max(lg, axis=1, keepdims=True)
    p = jnp.exp2(lg - m)
    l = jnp.sum(p, axis=1, keepdims=True)
    lg_ref[...] = p
    o = jnp.dot(lg_ref[:, past:].astype(BF16), vnew_s[...].astype(BF16), preferred_element_type=F32)
    for c in range(n_pages // PAGES_PER_DOT):
        o = o + lax.dot_general(lg_ref[:, c * kc:(c + 1) * kc].astype(BF16), pages(vt_buf, c), nt,
                                preferred_element_type=F32)
    o_ref[0] = o * (1.0 / l)


def _sample_attn_call(page_table, q, qi, kiwi, kn, vn, cki_t, ck_t, cv_t):
    nb, dq, _ = q.shape
    n_pages = page_table.shape[1]
    assert cki_t.shape[2] == LANES and n_pages % PAGES_PER_DOT == 0
    lp = (n_pages + 1) * LANES
    n_sel = min(INDEX_TOPK, (n_pages * LANES + dq) // 4)
    b3 = lambda b, pt: (b, 0, 0)
    kern = functools.partial(_sample_attn_kernel, n_pages=n_pages, dq=dq, n_sel=n_sel,
                             idx_bits=max(1, (lp - 1).bit_length()))
    grid_spec = pltpu.PrefetchScalarGridSpec(
        num_scalar_prefetch=1,
        grid=(nb,),
        in_specs=[
            pl.BlockSpec((1, dq, Q_W), b3),
            pl.BlockSpec((1, dq, QI_W), b3),
            pl.BlockSpec((1, dq, LANES), b3),
            pl.BlockSpec((1, dq, KV_W), b3),
            pl.BlockSpec((1, dq, KV_W), b3),
            pl.BlockSpec(memory_space=pl.ANY),
            pl.BlockSpec(memory_space=pl.ANY),
            pl.BlockSpec(memory_space=pl.ANY),
        ],
        out_specs=pl.BlockSpec((1, N_HEADS * dq, KV_W), b3),
        scratch_shapes=[
            pltpu.VMEM((2, n_pages, IDX_DIM, LANES), F32),
            pltpu.VMEM((2, n_pages, KV_W, LANES), F32),
            pltpu.VMEM((2, n_pages, KV_W, LANES), F32),
            pltpu.SemaphoreType.DMA((2, 3)),
            pltpu.VMEM((LANES, IDX_DIM), F32),
            pltpu.VMEM((LANES, KV_W), F32),
            pltpu.VMEM((LANES, KV_W), F32),
            pltpu.VMEM((dq, lp), I32),
            pltpu.VMEM((dq, lp), F32),
            pltpu.VMEM((N_HEADS * dq, lp), F32),
            pltpu.VMEM((dq, 1), I32),
        ],
    )
    return pl.pallas_call(
        kern,
        grid_spec=grid_spec,
        out_shape=jax.ShapeDtypeStruct((nb, N_HEADS * dq, KV_W), F32),
        compiler_params=_cparams(("arbitrary",)),
    )(page_table, q, qi, kiwi, kn, vn, cki_t, ck_t, cv_t)


def _layer_norm(x, g, b):
    mu = jnp.mean(x, axis=-1, keepdims=True)
    xc = x - mu
    var = jnp.mean(xc * xc, axis=-1, keepdims=True)
    return xc * lax.rsqrt(var + LN_EPS) * g + b


HIST_ROWS = 32
HIST_PAD = HIST_ROWS - (CONV_WIDTH - 1)
CONV_ROWS = 64


def _mix_kernel(attn_ref, u_ref, uprev_ref, hist_ref, x_ref, cw_ref, cb_ref, cg_ref, cbb_ref, wo_ref, bo_ref,
                g1_ref, b1_ref, wr_ref, br_ref, h_ref, hb_ref, route_ref, xp_ref, conv_ref, *, tm, nseq, use_prev):
    rows = tm // nseq
    ch = u_ref.shape[1]

    if use_prev:
        first = pl.program_id(1) == 0

        @pl.when(first)
        def _():
            xp_ref[:, 0:HIST_ROWS, :] = hist_ref[...]

        @pl.when(jnp.logical_not(first))
        def _():
            xp_ref[0, 0:HIST_ROWS, :] = uprev_ref[...]
    else:
        xp_ref[:, 0:HIST_ROWS, :] = hist_ref[...]
    xp_ref[:, HIST_ROWS:, :] = u_ref[...].reshape(nseq, rows, ch)

    rs = min(rows, CONV_ROWS)

    def seq_body(s, carry):
        for r0 in range(0, rows, rs):
            acc = jnp.zeros((rs, ch), F32)
            for j in range(CONV_WIDTH):
                lo = r0 + HIST_PAD + j
                acc = acc + cw_ref[j:j + 1, :] * xp_ref[s, lo:lo + rs, :]
            conv_ref[pl.ds(pl.multiple_of(s * rows + r0, SUBLANES), rs), :] = acc
        return carry

    if nseq == 1:
        seq_body(0, 0)
    else:
        lax.fori_loop(0, nseq, seq_body, 0)

    y = _layer_norm(conv_ref[...] + cb_ref[...], cg_ref[...], cbb_ref[...])
    conv = (y * jax.nn.sigmoid(y)).astype(BF16)
    aw = attn_ref.shape[1]
    mixed = (jnp.dot(attn_ref[...].astype(BF16), wo_ref[0:aw, :], preferred_element_type=F32)
             + jnp.dot(conv, wo_ref[aw:, :], preferred_element_type=F32) + bo_ref[...])
    h = _layer_norm(DN_ALPHA * x_ref[...] + mixed, g1_ref[...], b1_ref[...])
    h_ref[...] = h
    hb = h.astype(BF16)
    hb_ref[...] = hb

    logits = jnp.dot(hb, wr_ref[...], preferred_element_type=F32) + br_ref[...]
    lane = lax.broadcasted_iota(I32, (1, LANES), 1)
    work = logits
    vals, ids = [], []
    for _ in range(TOP_K):
        mx = jnp.max(work, axis=1, keepdims=True)
        ix = jnp.min(jnp.where(work == mx, lane, LANES), axis=1, keepdims=True)
        vals.append(mx)
        ids.append(ix)
        work = jnp.where(lane == ix, -jnp.inf, work)
    ex = [jnp.exp(v - vals[0]) for v in vals]
    den = ex[0]
    for e in ex[1:]:
        den = den + e
    route = jnp.zeros((tm, LANES), F32)
    for k in range(TOP_K):
        route = jnp.where(lane == k, ex[k] / den, route)
        route = jnp.where(lane == TOP_K + k, ids[k].astype(F32), route)
    route_ref[...] = route


def _mix_call(attn, u, hist, x, weights, *, row_off, tm, nseq, grid):
    cw, cb, cg, cbb, wo, bo, g1, b1, wr, br = weights
    n_rows, d = x.shape
    ch = u.shape[1]
    use_prev = grid[1] > 1
    tiles_per_seq = grid[1]
    off_t = row_off // tm
    tile = lambda b, i: (b * tiles_per_seq + i, 0)
    otile = lambda b, i: (off_t + b * tiles_per_seq + i, 0)
    const = lambda b, i: (0, 0)
    if use_prev:
        per_tile = tm // HIST_ROWS
        prev = lambda b, i: (jnp.maximum((off_t + b * tiles_per_seq + i) * per_tile - 1, 0), 0)
    else:
        prev = const
    in_specs = [
        pl.BlockSpec((tm, attn.shape[1]), tile),
        pl.BlockSpec((tm, ch), otile),
        pl.BlockSpec((HIST_ROWS, ch), prev),
        pl.BlockSpec((nseq, HIST_ROWS, ch), lambda b, i: (b, 0, 0)),
        pl.BlockSpec((tm, d), tile),
        pl.BlockSpec(cw.shape, const),
        pl.BlockSpec(cb.shape, const),
        pl.BlockSpec(cg.shape, const),
        pl.BlockSpec(cbb.shape, const),
        pl.BlockSpec(wo.shape, const),
        pl.BlockSpec(bo.shape, const),
        pl.BlockSpec(g1.shape, const),
        pl.BlockSpec(b1.shape, const),
        pl.BlockSpec(wr.shape, const),
        pl.BlockSpec(br.shape, const),
    ]
    kern = functools.partial(_mix_kernel, tm=tm, nseq=nseq, use_prev=use_prev)
    return pl.pallas_call(
        kern,
        grid=grid,
        in_specs=in_specs,
        out_specs=(
            pl.BlockSpec((tm, d), tile),
            pl.BlockSpec((tm, d), tile),
            pl.BlockSpec((tm, LANES), tile),
        ),
        out_shape=(
            jax.ShapeDtypeStruct((n_rows, d), F32),
            jax.ShapeDtypeStruct((n_rows, d), BF16),
            jax.ShapeDtypeStruct((n_rows, LANES), F32),
        ),
        scratch_shapes=[
            pltpu.VMEM((nseq, HIST_ROWS + tm // nseq, ch), F32),
            pltpu.VMEM((tm, ch), F32),
        ],
        compiler_params=_cparams(("parallel", "arbitrary")),
    )(attn, u, u, hist, x, cw, cb, cg, cbb, wo, bo, g1, b1, wr, br)


CAST_ROWS = 128


def _moe_kernel(be_ref, nu_ref, x_ref, wgu_ref, bgu_ref, wd_ref, bd_ref, y_ref, wgu_s, wd_s):
    i = pl.program_id(0)
    used = i < nu_ref[0]
    changed = (i == 0) | (be_ref[i] != be_ref[jnp.maximum(i - 1, 0)])
    dff = wd_s.shape[0]

    @pl.when(used & changed)
    def _():
        def cast_gu(r, carry):
            r0 = pl.multiple_of(r * CAST_ROWS, CAST_ROWS)
            wgu_s[pl.ds(r0, CAST_ROWS), :] = wgu_ref[0, pl.ds(r0, CAST_ROWS), :].astype(BF16)
            return carry

        def cast_d(r, carry):
            r0 = pl.multiple_of(r * CAST_ROWS, CAST_ROWS)
            wd_s[pl.ds(r0, CAST_ROWS), :] = wd_ref[0, pl.ds(r0, CAST_ROWS), :].astype(BF16)
            return carry

        lax.fori_loop(0, wgu_s.shape[0] // CAST_ROWS, cast_gu, 0)
        lax.fori_loop(0, wd_s.shape[0] // CAST_ROWS, cast_d, 0)

    @pl.when(used)
    def _():
        gu = jnp.dot(x_ref[...], wgu_s[...], preferred_element_type=F32) + bgu_ref[0]
        gate = jnp.minimum(gu[:, :dff], SWIGLU_LIMIT)
        up = jnp.clip(gu[:, dff:], -SWIGLU_LIMIT, SWIGLU_LIMIT)
        act = (up + 1.0) * gate * jax.nn.sigmoid(SWIGLU_ALPHA * gate)
        y = jnp.dot(act.astype(BF16), wd_s[...], preferred_element_type=F32) + bd_ref[0]
        y_ref[...] = y.astype(y_ref.dtype)

    @pl.when(jnp.logical_not(used))
    def _():
        y_ref[...] = jnp.zeros(y_ref.shape, y_ref.dtype)


def _moe_call(block_e, n_used, x_pad, wgu, bgu, wd, bd, bm):
    p, d = x_pad.shape
    ne, _, dgu = wgu.shape
    dff = wd.shape[1]
    n_blocks = p // bm
    grid_spec = pltpu.PrefetchScalarGridSpec(
        num_scalar_prefetch=2,
        grid=(n_blocks,),
        in_specs=[
            pl.BlockSpec((bm, d), lambda i, be, nu: (i, 0)),
            pl.BlockSpec((1, d, dgu), lambda i, be, nu: (be[i], 0, 0)),
            pl.BlockSpec((1, 1, dgu), lambda i, be, nu: (be[i], 0, 0)),
            pl.BlockSpec((1, dff, d), lambda i, be, nu: (be[i], 0, 0)),
            pl.BlockSpec((1, 1, d), lambda i, be, nu: (be[i], 0, 0)),
        ],
        out_specs=pl.BlockSpec((bm, d), lambda i, be, nu: (i, 0)),
        scratch_shapes=[pltpu.VMEM((d, dgu), BF16), pltpu.VMEM((dff, d), BF16)],
    )
    return pl.pallas_call(
        _moe_kernel,
        grid_spec=grid_spec,
        out_shape=jax.ShapeDtypeStruct((p, d), BF16),
        compiler_params=_cparams(("arbitrary",)),
    )(block_e, n_used, x_pad, wgu, bgu.reshape(ne, 1, dgu), wd, bd.reshape(ne, 1, d))


def _dispatch_plan(route, bm):
    n = route.shape[0]
    nk = n * TOP_K
    experts = jnp.arange(N_EXPERTS, dtype=I32)
    e_flat = route[:, TOP_K:2 * TOP_K].astype(I32).reshape(-1)
    iota = jnp.arange(nk, dtype=I32)
    e_sorted, order = lax.sort((e_flat, iota), num_keys=1, is_stable=True)
    _, inverse = lax.sort((order, iota), num_keys=1)
    counts = jnp.sum((e_flat[:, None] == experts[None, :]).astype(I32), axis=0)
    padded = (counts + bm - 1) // bm * bm
    start = jnp.cumsum(counts) - counts
    pend = jnp.cumsum(padded)
    pstart = pend - padded
    slot_sorted = pstart[e_sorted] + iota - start[e_sorted]
    slot_orig = slot_sorted[inverse]
    n_blocks = -(-(nk + N_EXPERTS * (bm - 1)) // bm)
    block_start = jnp.arange(n_blocks, dtype=I32) * bm
    block_e = jnp.minimum(jnp.sum((pend[None, :] <= block_start[:, None]).astype(I32), axis=1), N_EXPERTS - 1)
    n_used = (pend[-1:] // bm).astype(I32)
    e_slot = jnp.repeat(block_e, bm)
    rank = jnp.arange(n_blocks * bm, dtype=I32) - pstart[e_slot]
    src = jnp.clip(start[e_slot] + rank, 0, nk - 1)
    slot_tok = jnp.where(rank < counts[e_slot], order[src] // TOP_K, 0)
    return slot_tok, block_e, n_used, slot_orig


def _final_kernel(h_ref, yg_ref, route_ref, pe_ref, g2_ref, b2_ref, wg_ref, bg_ref, wp_ref, g3_ref, b3_ref, o_ref):
    route = route_ref[...]
    y = route[:, 0:1] * yg_ref[0].astype(F32)
    for k in range(1, TOP_K):
        y = y + route[:, k:k + 1] * yg_ref[k].astype(F32)
    h2 = _layer_norm(DN_ALPHA * h_ref[...] + y, g2_ref[...], b2_ref[...])
    gate = jax.nn.sigmoid(jnp.dot(h2.astype(BF16), wg_ref[...], preferred_element_type=F32) + bg_ref[...])
    proj = jnp.dot(pe_ref[...].astype(BF16), wp_ref[...], preferred_element_type=F32)
    o_ref[...] = _layer_norm(DN_ALPHA * h2 + gate * proj, g3_ref[...], b3_ref[...])


def _final_call(h, yg, route, pe, weights, tm):
    g2, b2, wg, bg, wp, g3, b3 = weights
    n, pd = pe.shape
    d = h.shape[1]
    row = lambda i: (i, 0)
    const = lambda i: (0, 0)
    return pl.pallas_call(
        _final_kernel,
        grid=(n // tm,),
        in_specs=[
            pl.BlockSpec((tm, d), row),
            pl.BlockSpec((TOP_K, tm, d), lambda i: (0, i, 0)),
            pl.BlockSpec((tm, LANES), row),
            pl.BlockSpec((tm, pd), row),
            pl.BlockSpec(g2.shape, const),
            pl.BlockSpec(b2.shape, const),
            pl.BlockSpec(wg.shape, const),
            pl.BlockSpec(bg.shape, const),
            pl.BlockSpec(wp.shape, const),
            pl.BlockSpec(g3.shape, const),
            pl.BlockSpec(b3.shape, const),
        ],
        out_specs=pl.BlockSpec((tm, d), row),
        out_shape=jax.ShapeDtypeStruct((n, d), F32),
        compiler_params=_cparams(("parallel",)),
    )(h, yg, route, pe, g2, b2, wg, bg, wp, g3, b3)


TOKEN_TILE = 512
QUERY_BLOCK = 128
MOE_ROWS = 512
MOE_ROWS_SAMPLE = 128


def _rope_tables(pos):
    rd = 2 * ROPE_HALF
    inv_freq = ROPE_THETA ** (-jnp.arange(ROPE_HALF, dtype=F32) * 2.0 / rd)
    ang = pos.astype(F32)[:, None] * inv_freq[None, :]
    cos, sin = jnp.cos(ang), jnp.sin(ang)
    ones = jnp.ones((pos.shape[0], HEAD_DIM - rd), F32)
    cos_h = jnp.concatenate([cos, cos, ones], axis=1)
    sin_h = jnp.concatenate([-sin, sin, 0.0 * ones], axis=1)
    reps = LANES // HEAD_DIM
    return jnp.tile(cos_h, (1, reps)), jnp.tile(sin_h, (1, reps))


def _layer(x_prompt, x_sample, p_prompt, p_sample, cache_k, cache_v, cache_kidx, state_conv, page_table,
           w_in, b_in, w_o, b_o, ln1_g, ln1_b, conv_w, conv_b, conv_ln_g, conv_ln_b,
           w_router, b_router, w_gate_up, b_gate_up, w_down, b_down, ln2_g, ln2_b,
           w_ple_gate, b_ple_gate, w_ple_proj, ln3_g, ln3_b):
    nb, t, d = x_prompt.shape
    db, dq, _ = x_sample.shape
    n_pages, page = page_table.shape[1], cache_k.shape[1]
    past = n_pages * page
    ch = conv_w.shape[1]
    tm = min(TOKEN_TILE, t)
    np_rows, ns_rows = nb * t, db * dq
    n = np_rows + ns_rows
    assert t % tm == 0 and ns_rows % tm == 0 and tm % dq == 0 and dq == SUBLANES
    row2 = lambda a: a.reshape(1, -1)

    n_head = Q_W + 2 * KV_W + QI_W + IDX_DIM + IDX_HEADS
    pad = LANES - IDX_DIM - IDX_HEADS
    w_pad = jnp.concatenate([w_in[:, :n_head], jnp.zeros((d, pad), F32), w_in[:, n_head:]], axis=1).astype(BF16)
    b_pad = jnp.concatenate([b_in[:n_head], jnp.zeros((pad,), F32), b_in[n_head:]]).reshape(1, -1)
    pos = jnp.concatenate([jnp.arange(t, dtype=I32), past + (jnp.arange(tm, dtype=I32) % dq)])
    cos_tab, sin_tab = _rope_tables(pos)
    tiles_p = np_rows // tm
    tab_index = lambda i: jnp.where(i < tiles_p, i % (t // tm), t // tm)
    q, k_f, v_f, k_b, v_t, qi, kiwi, ki_b, u = _inproj_call(
        x_prompt.reshape(np_rows, d), x_sample.reshape(ns_rows, d), w_pad, b_pad, cos_tab, sin_tab, tab_index, tm)

    ki_planes = ki_b[:np_rows].reshape(np_rows // PLANE_KEYS, SUBLANES, 32, IDX_DIM)
    ki_planes = ki_planes.transpose(0, 2, 1, 3).reshape(np_rows, IDX_DIM)
    attn_p = _prompt_attn_call(q, qi, kiwi, ki_planes, k_b, v_t, nb, t, min(QUERY_BLOCK, t), tm)

    wr_pad = jnp.concatenate([w_router, jnp.zeros((d, LANES - N_EXPERTS), F32)], axis=1).astype(BF16)
    br_pad = jnp.concatenate([b_router, jnp.full((LANES - N_EXPERTS,), -jnp.inf, F32)]).reshape(1, -1)
    mix_w = (conv_w, row2(conv_b), row2(conv_ln_g), row2(conv_ln_b), w_o.astype(BF16), row2(b_o),
             row2(ln1_g), row2(ln1_b), wr_pad, br_pad)
    hist_p = jnp.zeros((nb, HIST_ROWS, ch), F32)
    hist_s = jnp.concatenate([jnp.zeros((db, HIST_PAD, ch), F32), state_conv], axis=1)
    fin_w = (row2(ln2_g), row2(ln2_b), w_ple_gate.astype(BF16), row2(b_ple_gate), w_ple_proj.astype(BF16),
             row2(ln3_g), row2(ln3_b))

    def experts(h, h_b, route, bm):
        slot_tok, block_e, n_used, slot_orig = _dispatch_plan(route, bm)
        y_pad = _moe_call(block_e, n_used, h_b[slot_tok], w_gate_up, b_gate_up, w_down, b_down, bm)
        return y_pad[slot_orig.reshape(h.shape[0], TOP_K).T]

    h_p, hb_p, route_p = _mix_call(attn_p, u, hist_p, x_prompt.reshape(np_rows, d), mix_w, row_off=0, tm=tm,
                                   nseq=1, grid=(nb, t // tm))
    yg_p = experts(h_p, hb_p, route_p, MOE_ROWS)

    smp = lambda a: a[np_rows:].astype(F32).reshape(db, dq, -1)
    n_pool = cache_k.shape[0]
    kv_t = lambda pool: pool.transpose(0, 2, 3, 1).reshape(n_pool, KV_W, page)
    o_s = _sample_attn_call(page_table, smp(q), smp(qi), smp(kiwi), smp(k_f), smp(v_f),
                            cache_kidx.transpose(0, 2, 1), kv_t(cache_k), kv_t(cache_v))
    o_s = o_s.reshape(db, N_KV_HEADS, KV_GROUP, dq, N_KV_HEADS, HEAD_DIM)
    attn_s = jnp.stack([o_s[:, g, :, :, g, :] for g in range(N_KV_HEADS)], axis=1)
    attn_s = attn_s.transpose(0, 3, 1, 2, 4).reshape(ns_rows, Q_W)
    h_s, hb_s, route_s = _mix_call(attn_s, u, hist_s, x_sample.reshape(ns_rows, d), mix_w, row_off=np_rows, tm=tm,
                                   nseq=tm // dq, grid=(ns_rows // tm, 1))
    yg_s = experts(h_s, hb_s, route_s, MOE_ROWS_SAMPLE)
    y_p = _final_call(h_p, yg_p, route_p, p_prompt.reshape(np_rows, -1), fin_w, tm)
    y_s = _final_call(h_s, yg_s, route_s, p_sample.reshape(ns_rows, -1), fin_w, tm)

    u_p = u[:np_rows].reshape(nb, t, ch)
    u_s = u[np_rows:].reshape(db, dq, ch)
    keep = CONV_WIDTH - 1
    conv_p = jnp.concatenate([jnp.zeros((nb, keep, ch), F32), u_p], axis=1)[:, -keep:]
    conv_s = jnp.concatenate([state_conv, u_s], axis=1)[:, -keep:]
    kv = lambda a, lo, hi, b_, t_: a[lo:hi].reshape(b_, t_, N_KV_HEADS, HEAD_DIM)
    return (y_p.reshape(nb, t, d), y_s.reshape(db, dq, d),
            kv(k_f, 0, np_rows, nb, t), kv(v_f, 0, np_rows, nb, t),
            kiwi[:np_rows, :IDX_DIM].reshape(nb, t, IDX_DIM), conv_p,
            kv(k_f, np_rows, n, db, dq), kv(v_f, np_rows, n, db, dq),
            kiwi[np_rows:, :IDX_DIM].reshape(db, dq, IDX_DIM), conv_s)


def kernel(x_prompt, x_sample, p_prompt, p_sample, cache_k, cache_v, cache_kidx, state_conv, page_table, w_in, b_in, w_o, b_o, ln1_g, ln1_b, conv_w, conv_b, conv_ln_g, conv_ln_b, w_router, b_router, w_gate_up, b_gate_up, w_down, b_down, ln2_g, ln2_b, w_ple_gate, b_ple_gate, w_ple_proj, ln3_g, ln3_b):
    assert w_in.shape[0] == DEPTH
    outs = _layer(x_prompt, x_sample, p_prompt[0], p_sample[0], cache_k[0], cache_v[0], cache_kidx[0],
                  state_conv[0], page_table, w_in[0], b_in[0], w_o[0], b_o[0], ln1_g[0], ln1_b[0],
                  conv_w[0], conv_b[0], conv_ln_g[0], conv_ln_b[0], w_router[0], b_router[0],
                  w_gate_up[0], b_gate_up[0], w_down[0], b_down[0], ln2_g[0], ln2_b[0],
                  w_ple_gate[0], b_ple_gate[0], w_ple_proj[0], ln3_g[0], ln3_b[0])
    y_p, y_s = outs[0], outs[1]
    return (y_p, y_s) + tuple(o[None] for o in outs[2:])
```

```python
import functools
import math

import jax
import jax.numpy as jnp
from jax import lax
from jax.experimental import pallas as pl
from jax.experimental.pallas import tpu as pltpu

F32 = jnp.float32
BF16 = jnp.bfloat16
I32 = jnp.int32

HEAD_DIM = 64
N_HEADS = 8
N_KV_HEADS = 2
KV_GROUP = N_HEADS // N_KV_HEADS
IDX_HEADS = 4
IDX_DIM = 64
INDEX_TOPK = 256
ROPE_THETA = 500000.0
ROPE_HALF = HEAD_DIM // 8
CONV_WIDTH = 31
N_EXPERTS = 32
TOP_K = 4
SWIGLU_LIMIT = 7.0
SWIGLU_ALPHA = 1.702
LN_EPS = 1e-5
DEPTH = 1
DN_ALPHA = (2 * DEPTH) ** 0.25
ATTN_SCALE = HEAD_DIM ** -0.5
IDX_W_SCALE = (IDX_HEADS * IDX_DIM) ** -0.5
LOG2E = math.log2(math.e)

LANES = 128
SUBLANES = 8
VMEM_LIMIT = 56 * 1024 * 1024
NEG_BIG = -1e30
INT_MIN = -(2 ** 31)

Q_W = N_HEADS * HEAD_DIM
KV_W = N_KV_HEADS * HEAD_DIM
QI_W = IDX_HEADS * IDX_DIM
VT_ROWS = HEAD_DIM + 16


def _cparams(sem):
    return pltpu.CompilerParams(dimension_semantics=sem, vmem_limit_bytes=VMEM_LIMIT)


def _rope_slab(z, cos, sin):
    lane = lax.broadcasted_iota(I32, (1, LANES), 1)
    first = (lane % HEAD_DIM) < ROPE_HALF
    up = pltpu.roll(z, LANES - ROPE_HALF, axis=1)
    dn = pltpu.roll(z, ROPE_HALF, axis=1)
    return z * cos + jnp.where(first, up, dn) * sin


def _inproj_kernel(xp_ref, xs_ref, w_ref, b_ref, cos_ref, sin_ref,
                   q_ref, kf_ref, vf_ref, kb_ref, vt_ref, qi_ref, kiwi_ref, kib_ref, u_ref, *, conv_ch, tiles_p):
    x = jnp.where(pl.program_id(0) < tiles_p, xp_ref[...], xs_ref[...]).astype(BF16)
    cos = cos_ref[...]
    sin = sin_ref[...]

    def proj(lo, width):
        return jnp.dot(x, w_ref[:, lo:lo + width], preferred_element_type=F32) + b_ref[:, lo:lo + width]

    def rope(z):
        n = z.shape[1] // LANES
        slabs = [_rope_slab(z[:, s * LANES:(s + 1) * LANES], cos, sin) for s in range(n)]
        return slabs[0] if n == 1 else jnp.concatenate(slabs, axis=1)

    o = 0
    q = rope(proj(o, Q_W))
    q_ref[...] = (q * (ATTN_SCALE * LOG2E)).astype(BF16)
    o += Q_W
    k = rope(proj(o, KV_W))
    kf_ref[...] = k
    kb_ref[...] = k.astype(BF16)
    o += KV_W
    v = proj(o, KV_W)
    vf_ref[...] = v
    vt = v.T
    ones = jnp.ones((VT_ROWS - HEAD_DIM, vt.shape[1]), F32)
    vt_ref[0] = jnp.concatenate(
        [piece for g in range(N_KV_HEADS) for piece in (vt[g * HEAD_DIM:(g + 1) * HEAD_DIM], ones)],
        axis=0).astype(BF16)
    o += KV_W
    qi_ref[...] = rope(proj(o, QI_W)).astype(BF16)
    o += QI_W
    lane = lax.broadcasted_iota(I32, (1, LANES), 1)
    is_ki = lane < IDX_DIM
    kiwi = _rope_slab(proj(o, LANES), jnp.where(is_ki, cos, 1.0), jnp.where(is_ki, sin, 0.0))
    kiwi = kiwi * jnp.where((lane >= IDX_DIM) & (lane < IDX_DIM + IDX_HEADS), IDX_W_SCALE, 1.0)
    kiwi_ref[...] = kiwi
    kib_ref[...] = kiwi[:, :IDX_DIM].astype(BF16)
    o += LANES
    a = proj(o, conv_ch)
    g = proj(o + conv_ch, conv_ch)
    u_ref[...] = a * jax.nn.sigmoid(g)


def _inproj_call(x_p, x_s, w_pad, b_pad, cos_tab, sin_tab, tab_index, tm):
    d = x_p.shape[1]
    tiles_p, tiles_s = x_p.shape[0] // tm, x_s.shape[0] // tm
    nt = tiles_p + tiles_s
    n = nt * tm
    nw = w_pad.shape[1]
    conv_ch = (nw - (Q_W + 2 * KV_W + QI_W + LANES)) // 2
    row = lambda i: (i, 0)
    const = lambda i: (0, 0)
    out_shapes = (
        jax.ShapeDtypeStruct((n, Q_W), BF16),
        jax.ShapeDtypeStruct((n, KV_W), F32),
        jax.ShapeDtypeStruct((n, KV_W), F32),
        jax.ShapeDtypeStruct((n, KV_W), BF16),
        jax.ShapeDtypeStruct((nt, N_KV_HEADS * VT_ROWS, tm), BF16),
        jax.ShapeDtypeStruct((n, QI_W), BF16),
        jax.ShapeDtypeStruct((n, LANES), F32),
        jax.ShapeDtypeStruct((n, IDX_DIM), BF16),
        jax.ShapeDtypeStruct((n, conv_ch), F32),
    )
    out_specs = (
        pl.BlockSpec((tm, Q_W), row),
        pl.BlockSpec((tm, KV_W), row),
        pl.BlockSpec((tm, KV_W), row),
        pl.BlockSpec((tm, KV_W), row),
        pl.BlockSpec((1, N_KV_HEADS * VT_ROWS, tm), lambda i: (i, 0, 0)),
        pl.BlockSpec((tm, QI_W), row),
        pl.BlockSpec((tm, LANES), row),
        pl.BlockSpec((tm, IDX_DIM), row),
        pl.BlockSpec((tm, conv_ch), row),
    )
    return pl.pallas_call(
        functools.partial(_inproj_kernel, conv_ch=conv_ch, tiles_p=tiles_p),
        grid=(nt,),
        in_specs=[
            pl.BlockSpec((tm, d), lambda i: (jnp.minimum(i, tiles_p - 1), 0)),
            pl.BlockSpec((tm, d), lambda i: (jnp.maximum(i - tiles_p, 0), 0)),
            pl.BlockSpec((d, nw), const),
            pl.BlockSpec((1, nw), const),
            pl.BlockSpec((tm, LANES), lambda i: (tab_index(i), 0)),
            pl.BlockSpec((tm, LANES), lambda i: (tab_index(i), 0)),
        ],
        out_specs=out_specs,
        out_shape=out_shapes,
        compiler_params=_cparams(("parallel",)),
    )(x_p, x_s, w_pad, b_pad, cos_tab, sin_tab)


def _sortable(score):
    bits = lax.bitcast_convert_type(score, I32)
    key = bits ^ ((bits >> 31) & 0x7FFFFFFF)
    return jnp.where(score == 0.0, 0, key)


def _kth_largest(count_ge, shape, n_sel):
    def bit_body(t, thr):
        cand = thr + jnp.left_shift(jnp.int32(1), 31 - t)
        return jnp.where(count_ge(cand) >= n_sel, cand, thr)

    return lax.fori_loop(0, 32, bit_body, jnp.full(shape, INT_MIN, I32))


def _tie_bound(count_eq_below, need, shape, idx_bits):
    def bit_body(t, j0):
        cand = j0 + jnp.left_shift(jnp.int32(1), idx_bits - 1 - t)
        return jnp.where(count_eq_below(cand) < need, cand, j0)

    return lax.fori_loop(0, idx_bits, bit_body, jnp.zeros(shape, I32))


NO_BOUND = 2 ** 30
PLANE_KEYS = 32 * SUBLANES


def _bit_transpose32(rows):
    a = list(rows[::-1])
    j, m = 16, 0x0000FFFF
    while j:
        k = 0
        while k < 32:
            t = (a[k] ^ (a[k + j] >> j)) & m
            a[k] = a[k] ^ t
            a[k + j] = a[k + j] ^ (t << j)
            k = (k + j + 1) & ~j
        j >>= 1
        m = (m ^ (m << j)) & 0xFFFFFFFF
    return a[::-1]


def _plane_order(n):
    p = jnp.arange(n, dtype=I32)
    within = p % PLANE_KEYS
    return p - within + (within % SUBLANES) * 32 + within // SUBLANES


N_PAIRS = N_HEADS // 2
PAIRS_PER_GROUP = KV_GROUP // 2


def _prompt_attn_kernel(q_ref, qi_ref, kiwi_ref, kib_ref, koff_ref, kb_ref, vt_ref, o_ref,
                        qst_ref, qit_ref, pln_ref, cand_ref, top_ref, sel_ref, bias_ref, j0_ref,
                        lg0_ref, lg1_ref, mx0_ref, mx1_ref, pr0_ref, pr1_ref, al0_ref, al1_ref, m_ref, acc_ref,
                        *, qb, kc, n_sel, idx_bits):
    i = pl.program_id(1)
    n_chunks = (i * qb + qb + kc - 1) // kc
    q_pos = i * qb + lax.broadcasted_iota(I32, (1, qb), 1)

    def key_pos(k0):
        return k0 + lax.broadcasted_iota(I32, (kc, 1), 0)

    qt = q_ref[...].astype(F32).T
    zero = jnp.zeros((HEAD_DIM, 2 * qb), F32)
    for p in range(N_PAIRS):
        g = p // PAIRS_PER_GROUP
        pair = jnp.concatenate([qt[(2 * p) * HEAD_DIM:(2 * p + 1) * HEAD_DIM],
                                qt[(2 * p + 1) * HEAD_DIM:(2 * p + 2) * HEAD_DIM]], axis=1)
        qst_ref[p] = jnp.concatenate([pair if gg == g else zero for gg in range(N_KV_HEADS)], axis=0).astype(BF16)
    qit = qi_ref[...].astype(F32).T
    qit_ref[...] = jnp.concatenate(
        [qit[h * IDX_DIM:(h + 1) * IDX_DIM] for h in range(IDX_HEADS)], axis=1).astype(BF16)
    wit = kiwi_ref[...].T[IDX_DIM:IDX_DIM + SUBLANES]

    planes_per_chunk = kc // PLANE_KEYS

    @pl.when(i == 0)
    def _():
        pln_ref[...] = jnp.zeros(pln_ref.shape, I32)

    def score_body(c, carry):
        k0 = pl.multiple_of(c * kc, kc)
        s = jnp.dot(kib_ref[pl.ds(k0, kc), :], qit_ref[...], preferred_element_type=F32)
        score = jnp.zeros((kc, qb), F32)
        for h in range(IDX_HEADS):
            score = score + wit[h:h + 1, :] * jnp.maximum(s[:, h * qb:(h + 1) * qb], 0.0)
        score = jnp.where(k0 + koff_ref[...] <= q_pos, score, -jnp.inf)
        key = _sortable(score)
        for hb in range(planes_per_chunk):
            base = hb * PLANE_KEYS
            planes = _bit_transpose32([key[base + j * SUBLANES:base + (j + 1) * SUBLANES] for j in range(32)])
            planes[31] = ~planes[31]
            r0 = pl.multiple_of((c * planes_per_chunk + hb) * SUBLANES, SUBLANES)
            for b in range(32):
                pln_ref[b, pl.ds(r0, SUBLANES), :] = planes[b]
        return carry

    lax.fori_loop(0, n_chunks, score_body, 0)

    n_rows = pln_ref.shape[1]
    row = lax.broadcasted_iota(I32, (n_rows, 1), 0)
    cand_ref[...] = jnp.broadcast_to(jnp.where(row < n_chunks * (kc // 32), -1, 0), (n_rows, qb))
    top_ref[...] = jnp.zeros((n_rows, qb), I32)

    def bit_count(words):
        cnt = lax.population_count(words)
        return jnp.sum(jnp.sum(cnt.reshape(n_rows // SUBLANES, SUBLANES, qb), axis=0), axis=0, keepdims=True)

    def bit_body(tb, need):
        plane = pln_ref[31 - tb]
        cand = cand_ref[...]
        ones = cand & plane
        cnt = bit_count(ones)
        take = cnt >= need
        skip = jnp.where(take, 0, -1)
        top_ref[...] = top_ref[...] | (ones & skip)
        cand_ref[...] = cand & (plane ^ skip)
        return jnp.where(take, need, need - cnt)

    need = lax.fori_loop(0, 32, bit_body, jnp.full((1, qb), n_sel, I32))
    tie = bit_count(cand_ref[...]) > need

    def low_mask(bound):
        d = bound - row * 32
        return jnp.where(d >= 32, -1, jnp.left_shift(1, jnp.clip(d, 0, 31)) - 1)

    j0_ref[...] = jnp.full((1, qb), NO_BOUND, I32)

    @pl.when(jnp.max(jnp.where(tie, 1, 0)) > 0)
    def _():
        j0 = _tie_bound(lambda bound: bit_count(cand_ref[...] & low_mask(bound)), need, (1, qb), idx_bits)
        j0_ref[...] = jnp.where(tie, j0 + 1, NO_BOUND)

    sel_ref[...] = top_ref[...] | (cand_ref[...] & low_mask(j0_ref[...]))

    words_per_chunk = kc // 32

    def bias_body(c, carry):
        k0 = pl.multiple_of(c * kc, kc)
        words = sel_ref[pl.ds(pl.multiple_of(c * words_per_chunk, words_per_chunk), words_per_chunk), :]
        sub = lax.broadcasted_iota(I32, (SUBLANES, 1), 0)
        pieces = []
        for v in range(kc // SUBLANES):
            word = jnp.broadcast_to(words[v // 4:v // 4 + 1, :], (SUBLANES, qb))
            pieces.append(lax.shift_right_logical(word, (v % 4) * SUBLANES + sub) & 1)
        chosen = jnp.concatenate(pieces, axis=0) == 1
        bias_ref[pl.ds(k0, kc), :] = jnp.where(chosen & (key_pos(k0) <= q_pos), 0.0, NEG_BIG)
        return carry

    lax.fori_loop(0, n_chunks, bias_body, 0)

    m_ref[...] = jnp.full(m_ref.shape, NEG_BIG, F32)
    acc_ref[...] = jnp.zeros(acc_ref.shape, F32)
    lgs, mxs, prs, als = (lg0_ref, lg1_ref), (mx0_ref, mx1_ref), (pr0_ref, pr1_ref), (al0_ref, al1_ref)

    bias_ref[pl.ds(pl.multiple_of(n_chunks * kc, kc), kc), :] = jnp.full((kc, qb), NEG_BIG, F32)

    def logits(c, slot):
        k0 = pl.multiple_of(jnp.minimum(c, n_chunks - 1) * kc, kc)
        b0 = pl.multiple_of(jnp.minimum(c, n_chunks) * kc, kc)
        for p in range(N_PAIRS):
            lg = jnp.dot(kb_ref[pl.ds(k0, kc), :], qst_ref[p], preferred_element_type=F32)
            halves = [lg[:, h * qb:(h + 1) * qb] + bias_ref[pl.ds(b0, kc), :] for h in range(2)]
            for h in range(2):
                lgs[slot][p, :, h * qb:(h + 1) * qb] = halves[h]
            mxs[slot][p] = jnp.concatenate([jnp.max(x, axis=0, keepdims=True) for x in halves], axis=1)

    def probs(slot):
        for p in range(N_PAIRS):
            m_old = m_ref[p]
            m_new = jnp.maximum(m_old, mxs[slot][p])
            als[slot][p] = jnp.exp2(m_old - m_new)
            prs[slot][p] = jnp.exp2(lgs[slot][p] - m_new).astype(BF16)
            m_ref[p] = m_new

    def weighted(c, slot):
        cc = jnp.clip(c, 0, n_chunks - 1)
        for p in range(N_PAIRS):
            g = p // PAIRS_PER_GROUP
            vt = vt_ref[cc, g * VT_ROWS:(g + 1) * VT_ROWS, :]
            acc_ref[p] = als[slot][p] * acc_ref[p] + jnp.dot(vt, prs[slot][p], preferred_element_type=F32)

    logits(0, 0)
    pr1_ref[...] = jnp.zeros(pr1_ref.shape, BF16)
    al1_ref[...] = jnp.ones(al1_ref.shape, F32)

    def attn_body(j, carry):
        c = 2 * j
        logits(c + 1, 1)
        probs(0)
        weighted(c - 1, 1)
        logits(c + 2, 0)
        probs(1)
        weighted(c, 0)
        return carry

    lax.fori_loop(0, (n_chunks + 1) // 2, attn_body, 0)
    weighted(n_chunks - 1, 1)

    heads = []
    for p in range(N_PAIRS):
        acc = acc_ref[p]
        o = acc[:HEAD_DIM] * (1.0 / acc[HEAD_DIM:HEAD_DIM + 1])
        heads += [o[:, :qb], o[:, qb:]]
    o_ref[...] = jnp.concatenate(heads, axis=0).T.astype(o_ref.dtype)


def _prompt_attn_call(q, qi, kiwi, kib, kb, vt, nb, t, qb, kc):
    assert kc % PLANE_KEYS == 0 and t % kc == 0
    nqb = t // qb
    n_sel = min(INDEX_TOPK, t // 4)
    blk = lambda b, i: (b * nqb + i, 0)
    seq = lambda b, i: (b, 0)
    koff = jnp.broadcast_to(_plane_order(kc)[:, None], (kc, qb))
    kern = functools.partial(_prompt_attn_kernel, qb=qb, kc=kc, n_sel=n_sel, idx_bits=max(1, (t - 1).bit_length()))
    return pl.pallas_call(
        kern,
        grid=(nb, nqb),
        in_specs=[
            pl.BlockSpec((qb, Q_W), blk),
            pl.BlockSpec((qb, QI_W), blk),
            pl.BlockSpec((qb, LANES), blk),
            pl.BlockSpec((t, IDX_DIM), seq),
            pl.BlockSpec((kc, qb), lambda b, i: (0, 0)),
            pl.BlockSpec((t, KV_W), seq),
            pl.BlockSpec((t // kc, N_KV_HEADS * VT_ROWS, kc), lambda b, i: (b, 0, 0)),
        ],
        out_specs=pl.BlockSpec((qb, Q_W), blk),
        out_shape=jax.ShapeDtypeStruct((nb * t, Q_W), BF16),
        scratch_shapes=[
            pltpu.VMEM((N_PAIRS, KV_W, 2 * qb), BF16),
            pltpu.VMEM((IDX_DIM, IDX_HEADS * qb), BF16),
            pltpu.VMEM((32, t // 32, qb), I32),
            pltpu.VMEM((t // 32, qb), I32),
            pltpu.VMEM((t // 32, qb), I32),
            pltpu.VMEM((t // 32, qb), I32),
            pltpu.VMEM((t + kc, qb), F32),
            pltpu.VMEM((1, qb), I32),
            pltpu.VMEM((N_PAIRS, kc, 2 * qb), F32),
            pltpu.VMEM((N_PAIRS, kc, 2 * qb), F32),
            pltpu.VMEM((N_PAIRS, 1, 2 * qb), F32),
            pltpu.VMEM((N_PAIRS, 1, 2 * qb), F32),
            pltpu.VMEM((N_PAIRS, kc, 2 * qb), BF16),
            pltpu.VMEM((N_PAIRS, kc, 2 * qb), BF16),
            pltpu.VMEM((N_PAIRS, 1, 2 * qb), F32),
            pltpu.VMEM((N_PAIRS, 1, 2 * qb), F32),
            pltpu.VMEM((N_PAIRS, 1, 2 * qb), F32),
            pltpu.VMEM((N_PAIRS, VT_ROWS, 2 * qb), F32),
        ],
        compiler_params=_cparams(("parallel", "arbitrary")),
    )(q, qi, kiwi, kib, koff, kb, vt)


PAGES_PER_DOT = 4


def _sample_attn_kernel(pt_ref, q_ref, qi_ref, kiwi_ref, kn_ref, vn_ref, cki_hbm, ck_hbm, cv_hbm, o_ref,
                        kit_buf, kt_buf, vt_buf, sem, kin_s, knew_s, vnew_s, keys_ref, bias_ref, lg_ref, j0_ref,
                        *, n_pages, dq, n_sel, idx_bits):
    b = pl.program_id(0)
    nb = pl.num_programs(0)
    slot = b % 2
    past = n_pages * LANES
    lp = past + LANES
    streams = ((cki_hbm, kit_buf), (ck_hbm, kt_buf), (cv_hbm, vt_buf))

    def page_copy(a, seq, pg, sl):
        src, dst = streams[a]
        return pltpu.make_async_copy(src.at[pt_ref[seq, pg]], dst.at[sl, pg], sem.at[sl, a])

    def fetch(seq, sl):
        def body(pg, carry):
            for a in range(len(streams)):
                page_copy(a, seq, pg, sl).start()
            return carry

        lax.fori_loop(0, n_pages, body, 0)

    @pl.when(b == 0)
    def _():
        fetch(0, 0)

    @pl.when(b + 1 < nb)
    def _():
        fetch(b + 1, 1 - slot)

    def wait_body(pg, carry):
        for a in range(len(streams)):
            page_copy(a, b, pg, slot).wait()
        return carry

    lax.fori_loop(0, n_pages, wait_body, 0)

    def pages(buf, c):
        return jnp.concatenate([buf[slot, c * PAGES_PER_DOT + r] for r in range(PAGES_PER_DOT)], axis=1).astype(BF16)

    kc = PAGES_PER_DOT * LANES
    q = q_ref[0]
    qi = qi_ref[0]
    kiwi = kiwi_ref[0]
    qrow = lax.broadcasted_iota(I32, (dq, 1), 0)
    idx = lax.broadcasted_iota(I32, (1, lp), 1)
    rel = idx - past
    adm = (idx < past) | ((rel <= qrow) & (rel < dq))

    kin_s[...] = jnp.zeros(kin_s.shape, F32)
    knew_s[...] = jnp.zeros(knew_s.shape, F32)
    vnew_s[...] = jnp.zeros(vnew_s.shape, F32)
    kin_s[0:dq, :] = kiwi[:, :IDX_DIM]
    knew_s[0:dq, :] = kn_ref[0]
    vnew_s[0:dq, :] = vn_ref[0]

    nt = (((1,), (1,)), ((), ()))
    qi_rows = jnp.concatenate([qi[:, h * IDX_DIM:(h + 1) * IDX_DIM] for h in range(IDX_HEADS)], axis=0).astype(BF16)
    wi_col = jnp.concatenate([kiwi[:, IDX_DIM + h:IDX_DIM + h + 1] for h in range(IDX_HEADS)], axis=0)

    def scores(s):
        r = jnp.maximum(s, 0.0) * wi_col
        out = r[0:dq]
        for h in range(1, IDX_HEADS):
            out = out + r[h * dq:(h + 1) * dq]
        return out

    for c in range(n_pages // PAGES_PER_DOT):
        s = jnp.dot(qi_rows, pages(kit_buf, c), preferred_element_type=F32)
        keys_ref[:, c * kc:(c + 1) * kc] = _sortable(scores(s))
    s_new = scores(lax.dot_general(qi_rows, kin_s[...].astype(BF16), nt, preferred_element_type=F32))
    keys_ref[:, past:] = _sortable(jnp.where(adm[:, past:], s_new, -jnp.inf))

    def count(pred):
        return jnp.sum(jnp.where(pred(keys_ref[...]), 1, 0).astype(I32), axis=1, keepdims=True)

    thr = _kth_largest(lambda cand: count(lambda k: k >= cand), (dq, 1), n_sel)
    need = n_sel - count(lambda k: k > thr)
    tie = count(lambda k: k >= thr) > n_sel
    j0_ref[...] = jnp.full((dq, 1), NO_BOUND, I32)

    @pl.when(jnp.max(jnp.where(tie, 1, 0)) > 0)
    def _():
        j0 = _tie_bound(lambda cand: count(lambda k: (k == thr) & (idx < cand)), need, (dq, 1), idx_bits)
        j0_ref[...] = jnp.where(tie, j0, NO_BOUND)

    j0 = j0_ref[...]
    keys = keys_ref[...]
    sel = ((keys > thr) | ((keys == thr) & (idx <= j0))) & adm
    bias_ref[...] = jnp.where(sel, 0.0, NEG_BIG)

    zero = jnp.zeros((dq, HEAD_DIM), F32)
    rows = []
    for g in range(N_KV_HEADS):
        for r in range(KV_GROUP):
            h = g * KV_GROUP + r
            piece = q[:, h * HEAD_DIM:(h + 1) * HEAD_DIM]
            rows.append(jnp.concatenate([piece, zero] if g == 0 else [zero, piece], axis=1))
    qbd = jnp.concatenate(rows, axis=0).astype(BF16)

    for c in range(n_pages // PAGES_PER_DOT):
        lg_ref[:, c * kc:(c + 1) * kc] = jnp.dot(qbd, pages(kt_buf, c), preferred_element_type=F32)
    lg_ref[:, past:] = lax.dot_general(qbd, knew_s[...].astype(BF16), nt, preferred_element_type=F32)

    bias = bias_ref[...]
    lg = lg_ref[...] + jnp.concatenate([bias] * N_HEADS, axis=0)
    m = jnp.max(lg, axis=1, keepdims=True)
    p = jnp.exp2(lg - m)
    l = jnp.sum(p, axis=1, keepdims=True)
    lg_ref[...] = p
    o = jnp.dot(lg_ref[:, past:].astype(BF16), vnew_s[...].astype(BF16), preferred_element_type=F32)
    for c in range(n_pages // PAGES_PER_DOT):
        o = o + lax.dot_general(lg_ref[:, c * kc:(c + 1) * kc].astype(BF16), pages(vt_buf, c), nt,
                                preferred_element_type=F32)
    o_ref[0] = o * (1.0 / l)


def _sample_attn_call(page_table, q, qi, kiwi, kn, vn, cki_t, ck_t, cv_t):
    nb, dq, _ = q.shape
    n_pages = page_table.shape[1]
    assert cki_t.shape[2] == LANES and n_pages % PAGES_PER_DOT == 0
    lp = (n_pages + 1) * LANES
    n_sel = min(INDEX_TOPK, (n_pages * LANES + dq) // 4)
    b3 = lambda b, pt: (b, 0, 0)
    kern = functools.partial(_sample_attn_kernel, n_pages=n_pages, dq=dq, n_sel=n_sel,
                             idx_bits=max(1, (lp - 1).bit_length()))
    grid_spec = pltpu.PrefetchScalarGridSpec(
        num_scalar_prefetch=1,
        grid=(nb,),
        in_specs=[
            pl.BlockSpec((1, dq, Q_W), b3),
            pl.BlockSpec((1, dq, QI_W), b3),
            pl.BlockSpec((1, dq, LANES), b3),
            pl.BlockSpec((1, dq, KV_W), b3),
            pl.BlockSpec((1, dq, KV_W), b3),
            pl.BlockSpec(memory_space=pl.ANY),
            pl.BlockSpec(memory_space=pl.ANY),
            pl.BlockSpec(memory_space=pl.ANY),
        ],
        out_specs=pl.BlockSpec((1, N_HEADS * dq, KV_W), b3),
        scratch_shapes=[
            pltpu.VMEM((2, n_pages, IDX_DIM, LANES), F32),
            pltpu.VMEM((2, n_pages, KV_W, LANES), F32),
            pltpu.VMEM((2, n_pages, KV_W, LANES), F32),
            pltpu.SemaphoreType.DMA((2, 3)),
            pltpu.VMEM((LANES, IDX_DIM), F32),
            pltpu.VMEM((LANES, KV_W), F32),
            pltpu.VMEM((LANES, KV_W), F32),
            pltpu.VMEM((dq, lp), I32),
            pltpu.VMEM((dq, lp), F32),
            pltpu.VMEM((N_HEADS * dq, lp), F32),
            pltpu.VMEM((dq, 1), I32),
        ],
    )
    return pl.pallas_call(
        kern,
        grid_spec=grid_spec,
        out_shape=jax.ShapeDtypeStruct((nb, N_HEADS * dq, KV_W), F32),
        compiler_params=_cparams(("arbitrary",)),
    )(page_table, q, qi, kiwi, kn, vn, cki_t, ck_t, cv_t)


def _layer_norm(x, g, b):
    mu = jnp.mean(x, axis=-1, keepdims=True)
    xc = x - mu
    var = jnp.mean(xc * xc, axis=-1, keepdims=True)
    return xc * lax.rsqrt(var + LN_EPS) * g + b


HIST_ROWS = 32
HIST_PAD = HIST_ROWS - (CONV_WIDTH - 1)
CONV_ROWS = 64


def _mix_kernel(attn_ref, u_ref, uprev_ref, hist_ref, x_ref, cw_ref, cb_ref, cg_ref, cbb_ref, wo_ref, bo_ref,
                g1_ref, b1_ref, wr_ref, br_ref, h_ref, hb_ref, route_ref, xp_ref, xs_ref, conv_ref,
                *, tm, nseq, use_prev):
    rows = tm // nseq
    ch = u_ref.shape[1]

    if use_prev:
        first = pl.program_id(1) == 0

        @pl.when(first)
        def _():
            xp_ref[:, 0:HIST_ROWS, :] = hist_ref[...]

        @pl.when(jnp.logical_not(first))
        def _():
            xp_ref[0, 0:HIST_ROWS, :] = uprev_ref[...]
    else:
        xp_ref[:, 0:HIST_ROWS, :] = hist_ref[...]
    xp_ref[:, HIST_ROWS:, :] = u_ref[...].reshape(nseq, rows, ch)

    rs = min(rows, CONV_ROWS)
    span = HIST_ROWS + rows - SUBLANES

    def seq_body(s, carry):
        for r in range(1, SUBLANES):
            xs_ref[r - 1] = xp_ref[s, r:r + span, :]
        for r0 in range(0, rows, rs):
            acc = jnp.zeros((rs, ch), F32)
            for j in range(CONV_WIDTH):
                r = (HIST_PAD + j) % SUBLANES
                lo = r0 + HIST_PAD + j - r
                win = xp_ref[s, lo:lo + rs, :] if r == 0 else xs_ref[r - 1, lo:lo + rs, :]
                acc = acc + cw_ref[j:j + 1, :] * win
            conv_ref[pl.ds(pl.multiple_of(s * rows + r0, SUBLANES), rs), :] = acc
        return carry

    if nseq == 1:
        seq_body(0, 0)
    else:
        lax.fori_loop(0, nseq, seq_body, 0)

    y = _layer_norm(conv_ref[...] + cb_ref[...], cg_ref[...], cbb_ref[...])
    conv = (y * jax.nn.sigmoid(y)).astype(BF16)
    aw = attn_ref.shape[1]
    mixed = (jnp.dot(attn_ref[...].astype(BF16), wo_ref[0:aw, :], preferred_element_type=F32)
             + jnp.dot(conv, wo_ref[aw:, :], preferred_element_type=F32) + bo_ref[...])
    h = _layer_norm(DN_ALPHA * x_ref[...] + mixed, g1_ref[...], b1_ref[...])
    h_ref[...] = h
    hb = h.astype(BF16)
    hb_ref[...] = hb

    logits = jnp.dot(hb, wr_ref[...], preferred_element_type=F32) + br_ref[...]
    lane = lax.broadcasted_iota(I32, (1, LANES), 1)
    work = logits
    vals, ids = [], []
    for _ in range(TOP_K):
        mx = jnp.max(work, axis=1, keepdims=True)
        ix = jnp.min(jnp.where(work == mx, lane, LANES), axis=1, keepdims=True)
        vals.append(mx)
        ids.append(ix)
        work = jnp.where(lane == ix, -jnp.inf, work)
    ex = [jnp.exp(v - vals[0]) for v in vals]
    den = ex[0]
    for e in ex[1:]:
        den = den + e
    route = jnp.zeros((tm, LANES), F32)
    for k in range(TOP_K):
        route = jnp.where(lane == k, ex[k] / den, route)
        route = jnp.where(lane == TOP_K + k, ids[k].astype(F32), route)
    route_ref[...] = route


def _mix_call(attn, u, hist, x, weights, *, row_off, tm, nseq, grid):
    cw, cb, cg, cbb, wo, bo, g1, b1, wr, br = weights
    n_rows, d = x.shape
    ch = u.shape[1]
    use_prev = grid[1] > 1
    tiles_per_seq = grid[1]
    off_t = row_off // tm
    tile = lambda b, i: (b * tiles_per_seq + i, 0)
    otile = lambda b, i: (off_t + b * tiles_per_seq + i, 0)
    const = lambda b, i: (0, 0)
    if use_prev:
        per_tile = tm // HIST_ROWS
        prev = lambda b, i: (jnp.maximum((off_t + b * tiles_per_seq + i) * per_tile - 1, 0), 0)
    else:
        prev = const
    in_specs = [
        pl.BlockSpec((tm, attn.shape[1]), tile),
        pl.BlockSpec((tm, ch), otile),
        pl.BlockSpec((HIST_ROWS, ch), prev),
        pl.BlockSpec((nseq, HIST_ROWS, ch), lambda b, i: (b, 0, 0)),
        pl.BlockSpec((tm, d), tile),
        pl.BlockSpec(cw.shape, const),
        pl.BlockSpec(cb.shape, const),
        pl.BlockSpec(cg.shape, const),
        pl.BlockSpec(cbb.shape, const),
        pl.BlockSpec(wo.shape, const),
        pl.BlockSpec(bo.shape, const),
        pl.BlockSpec(g1.shape, const),
        pl.BlockSpec(b1.shape, const),
        pl.BlockSpec(wr.shape, const),
        pl.BlockSpec(br.shape, const),
    ]
    kern = functools.partial(_mix_kernel, tm=tm, nseq=nseq, use_prev=use_prev)
    return pl.pallas_call(
        kern,
        grid=grid,
        in_specs=in_specs,
        out_specs=(
            pl.BlockSpec((tm, d), tile),
            pl.BlockSpec((tm, d), tile),
            pl.BlockSpec((tm, LANES), tile),
        ),
        out_shape=(
            jax.ShapeDtypeStruct((n_rows, d), F32),
            jax.ShapeDtypeStruct((n_rows, d), BF16),
            jax.ShapeDtypeStruct((n_rows, LANES), F32),
        ),
        scratch_shapes=[
            pltpu.VMEM((nseq, HIST_ROWS + tm // nseq, ch), F32),
            pltpu.VMEM((SUBLANES - 1, HIST_ROWS + tm // nseq - SUBLANES, ch), F32),
            pltpu.VMEM((tm, ch), F32),
        ],
        compiler_params=_cparams(("parallel", "arbitrary")),
    )(attn, u, u, hist, x, cw, cb, cg, cbb, wo, bo, g1, b1, wr, br)


CAST_ROWS = 128


def _moe_kernel(be_ref, nu_ref, x_ref, wgu_ref, bgu_ref, wd_ref, bd_ref, y_ref, wgu_s, wd_s):
    i = pl.program_id(0)
    used = i < nu_ref[0]
    changed = (i == 0) | (be_ref[i] != be_ref[jnp.maximum(i - 1, 0)])
    dff = wd_s.shape[0]

    @pl.when(used & changed)
    def _():
        def cast_gu(r, carry):
            r0 = pl.multiple_of(r * CAST_ROWS, CAST_ROWS)
            wgu_s[pl.ds(r0, CAST_ROWS), :] = wgu_ref[0, pl.ds(r0, CAST_ROWS), :].astype(BF16)
            return carry

        def cast_d(r, carry):
            r0 = pl.multiple_of(r * CAST_ROWS, CAST_ROWS)
            wd_s[pl.ds(r0, CAST_ROWS), :] = wd_ref[0, pl.ds(r0, CAST_ROWS), :].astype(BF16)
            return carry

        lax.fori_loop(0, wgu_s.shape[0] // CAST_ROWS, cast_gu, 0)
        lax.fori_loop(0, wd_s.shape[0] // CAST_ROWS, cast_d, 0)

    @pl.when(used)
    def _():
        gu = jnp.dot(x_ref[...], wgu_s[...], preferred_element_type=F32) + bgu_ref[0]
        gate = jnp.minimum(gu[:, :dff], SWIGLU_LIMIT)
        up = jnp.clip(gu[:, dff:], -SWIGLU_LIMIT, SWIGLU_LIMIT)
        act = (up + 1.0) * gate * jax.nn.sigmoid(SWIGLU_ALPHA * gate)
        y = jnp.dot(act.astype(BF16), wd_s[...], preferred_element_type=F32) + bd_ref[0]
        y_ref[...] = y.astype(y_ref.dtype)

    @pl.when(jnp.logical_not(used))
    def _():
        y_ref[...] = jnp.zeros(y_ref.shape, y_ref.dtype)


def _moe_call(block_e, n_used, x_pad, wgu, bgu, wd, bd, bm):
    p, d = x_pad.shape
    ne, _, dgu = wgu.shape
    dff = wd.shape[1]
    n_blocks = p // bm
    grid_spec = pltpu.PrefetchScalarGridSpec(
        num_scalar_prefetch=2,
        grid=(n_blocks,),
        in_specs=[
            pl.BlockSpec((bm, d), lambda i, be, nu: (i, 0)),
            pl.BlockSpec((1, d, dgu), lambda i, be, nu: (be[i], 0, 0)),
            pl.BlockSpec((1, 1, dgu), lambda i, be, nu: (be[i], 0, 0)),
            pl.BlockSpec((1, dff, d), lambda i, be, nu: (be[i], 0, 0)),
            pl.BlockSpec((1, 1, d), lambda i, be, nu: (be[i], 0, 0)),
        ],
        out_specs=pl.BlockSpec((bm, d), lambda i, be, nu: (i, 0)),
        scratch_shapes=[pltpu.VMEM((d, dgu), BF16), pltpu.VMEM((dff, d), BF16)],
    )
    return pl.pallas_call(
        _moe_kernel,
        grid_spec=grid_spec,
        out_shape=jax.ShapeDtypeStruct((p, d), BF16),
        compiler_params=_cparams(("arbitrary",)),
    )(block_e, n_used, x_pad, wgu, bgu.reshape(ne, 1, dgu), wd, bd.reshape(ne, 1, d))


def _dispatch_plan(route, bm):
    n = route.shape[0]
    nk = n * TOP_K
    experts = jnp.arange(N_EXPERTS, dtype=I32)
    e_flat = route[:, TOP_K:2 * TOP_K].astype(I32).reshape(-1)
    iota = jnp.arange(nk, dtype=I32)
    e_sorted, order = lax.sort((e_flat, iota), num_keys=1, is_stable=True)
    _, inverse = lax.sort((order, iota), num_keys=1)
    counts = jnp.sum((e_flat[:, None] == experts[None, :]).astype(I32), axis=0)
    padded = (counts + bm - 1) // bm * bm
    start = jnp.cumsum(counts) - counts
    pend = jnp.cumsum(padded)
    pstart = pend - padded
    slot_sorted = pstart[e_sorted] + iota - start[e_sorted]
    slot_orig = slot_sorted[inverse]
    n_blocks = -(-(nk + N_EXPERTS * (bm - 1)) // bm)
    block_start = jnp.arange(n_blocks, dtype=I32) * bm
    block_e = jnp.minimum(jnp.sum((pend[None, :] <= block_start[:, None]).astype(I32), axis=1), N_EXPERTS - 1)
    n_used = (pend[-1:] // bm).astype(I32)
    e_slot = jnp.repeat(block_e, bm)
    rank = jnp.arange(n_blocks * bm, dtype=I32) - pstart[e_slot]
    src = jnp.clip(start[e_slot] + rank, 0, nk - 1)
    slot_tok = jnp.where(rank < counts[e_slot], order[src] // TOP_K, 0)
    return slot_tok, block_e, n_used, slot_orig


def _final_kernel(h_ref, yg_ref, route_ref, pe_ref, g2_ref, b2_ref, wg_ref, bg_ref, wp_ref, g3_ref, b3_ref, o_ref):
    route = route_ref[...]
    y = route[:, 0:1] * yg_ref[0].astype(F32)
    for k in range(1, TOP_K):
        y = y + route[:, k:k + 1] * yg_ref[k].astype(F32)
    h2 = _layer_norm(DN_ALPHA * h_ref[...] + y, g2_ref[...], b2_ref[...])
    gate = jax.nn.sigmoid(jnp.dot(h2.astype(BF16), wg_ref[...], preferred_element_type=F32) + bg_ref[...])
    proj = jnp.dot(pe_ref[...].astype(BF16), wp_ref[...], preferred_element_type=F32)
    o_ref[...] = _layer_norm(DN_ALPHA * h2 + gate * proj, g3_ref[...], b3_ref[...])


def _final_call(h, yg, route, pe, weights, tm):
    g2, b2, wg, bg, wp, g3, b3 = weights
    n, pd = pe.shape
    d = h.shape[1]
    row = lambda i: (i, 0)
    const = lambda i: (0, 0)
    return pl.pallas_call(
        _final_kernel,
        grid=(n // tm,),
        in_specs=[
            pl.BlockSpec((tm, d), row),
            pl.BlockSpec((TOP_K, tm, d), lambda i: (0, i, 0)),
            pl.BlockSpec((tm, LANES), row),
            pl.BlockSpec((tm, pd), row),
            pl.BlockSpec(g2.shape, const),
            pl.BlockSpec(b2.shape, const),
            pl.BlockSpec(wg.shape, const),
            pl.BlockSpec(bg.shape, const),
            pl.BlockSpec(wp.shape, const),
            pl.BlockSpec(g3.shape, const),
            pl.BlockSpec(b3.shape, const),
        ],
        out_specs=pl.BlockSpec((tm, d), row),
        out_shape=jax.ShapeDtypeStruct((n, d), F32),
        compiler_params=_cparams(("parallel",)),
    )(h, yg, route, pe, g2, b2, wg, bg, wp, g3, b3)


TOKEN_TILE = 512
QUERY_BLOCK = 128
MOE_ROWS = 512
MOE_ROWS_SAMPLE = 128


def _rope_tables(pos):
    rd = 2 * ROPE_HALF
    inv_freq = ROPE_THETA ** (-jnp.arange(ROPE_HALF, dtype=F32) * 2.0 / rd)
    ang = pos.astype(F32)[:, None] * inv_freq[None, :]
    cos, sin = jnp.cos(ang), jnp.sin(ang)
    ones = jnp.ones((pos.shape[0], HEAD_DIM - rd), F32)
    cos_h = jnp.concatenate([cos, cos, ones], axis=1)
    sin_h = jnp.concatenate([-sin, sin, 0.0 * ones], axis=1)
    reps = LANES // HEAD_DIM
    return jnp.tile(cos_h, (1, reps)), jnp.tile(sin_h, (1, reps))


def _layer(x_prompt, x_sample, p_prompt, p_sample, cache_k, cache_v, cache_kidx, state_conv, page_table,
           w_in, b_in, w_o, b_o, ln1_g, ln1_b, conv_w, conv_b, conv_ln_g, conv_ln_b,
           w_router, b_router, w_gate_up, b_gate_up, w_down, b_down, ln2_g, ln2_b,
           w_ple_gate, b_ple_gate, w_ple_proj, ln3_g, ln3_b):
    nb, t, d = x_prompt.shape
    db, dq, _ = x_sample.shape
    n_pages, page = page_table.shape[1], cache_k.shape[1]
    past = n_pages * page
    ch = conv_w.shape[1]
    tm = min(TOKEN_TILE, t)
    np_rows, ns_rows = nb * t, db * dq
    n = np_rows + ns_rows
    assert t % tm == 0 and ns_rows % tm == 0 and tm % dq == 0 and dq == SUBLANES
    row2 = lambda a: a.reshape(1, -1)

    n_head = Q_W + 2 * KV_W + QI_W + IDX_DIM + IDX_HEADS
    pad = LANES - IDX_DIM - IDX_HEADS
    w_pad = jnp.concatenate([w_in[:, :n_head], jnp.zeros((d, pad), F32), w_in[:, n_head:]], axis=1).astype(BF16)
    b_pad = jnp.concatenate([b_in[:n_head], jnp.zeros((pad,), F32), b_in[n_head:]]).reshape(1, -1)
    pos = jnp.concatenate([jnp.arange(t, dtype=I32), past + (jnp.arange(tm, dtype=I32) % dq)])
    cos_tab, sin_tab = _rope_tables(pos)
    tiles_p = np_rows // tm
    tab_index = lambda i: jnp.where(i < tiles_p, i % (t // tm), t // tm)
    q, k_f, v_f, k_b, v_t, qi, kiwi, ki_b, u = _inproj_call(
        x_prompt.reshape(np_rows, d), x_sample.reshape(ns_rows, d), w_pad, b_pad, cos_tab, sin_tab, tab_index, tm)

    ki_planes = ki_b[:np_rows].reshape(np_rows // PLANE_KEYS, SUBLANES, 32, IDX_DIM)
    ki_planes = ki_planes.transpose(0, 2, 1, 3).reshape(np_rows, IDX_DIM)
    attn_p = _prompt_attn_call(q, qi, kiwi, ki_planes, k_b, v_t, nb, t, min(QUERY_BLOCK, t), tm)

    wr_pad = jnp.concatenate([w_router, jnp.zeros((d, LANES - N_EXPERTS), F32)], axis=1).astype(BF16)
    br_pad = jnp.concatenate([b_router, jnp.full((LANES - N_EXPERTS,), -jnp.inf, F32)]).reshape(1, -1)
    mix_w = (conv_w, row2(conv_b), row2(conv_ln_g), row2(conv_ln_b), w_o.astype(BF16), row2(b_o),
             row2(ln1_g), row2(ln1_b), wr_pad, br_pad)
    hist_p = jnp.zeros((nb, HIST_ROWS, ch), F32)
    hist_s = jnp.concatenate([jnp.zeros((db, HIST_PAD, ch), F32), state_conv], axis=1)
    fin_w = (row2(ln2_g), row2(ln2_b), w_ple_gate.astype(BF16), row2(b_ple_gate), w_ple_proj.astype(BF16),
             row2(ln3_g), row2(ln3_b))

    def experts(h, h_b, route, bm):
        slot_tok, block_e, n_used, slot_orig = _dispatch_plan(route, bm)
        y_pad = _moe_call(block_e, n_used, h_b[slot_tok], w_gate_up, b_gate_up, w_down, b_down, bm)
        return y_pad[slot_orig.reshape(h.shape[0], TOP_K).T]

    h_p, hb_p, route_p = _mix_call(attn_p, u, hist_p, x_prompt.reshape(np_rows, d), mix_w, row_off=0, tm=tm,
                                   nseq=1, grid=(nb, t // tm))
    yg_p = experts(h_p, hb_p, route_p, MOE_ROWS)

    smp = lambda a: a[np_rows:].astype(F32).reshape(db, dq, -1)
    n_pool = cache_k.shape[0]
    kv_t = lambda pool: pool.transpose(0, 2, 3, 1).reshape(n_pool, KV_W, page)
    o_s = _sample_attn_call(page_table, smp(q), smp(qi), smp(kiwi), smp(k_f), smp(v_f),
                            cache_kidx.transpose(0, 2, 1), kv_t(cache_k), kv_t(cache_v))
    o_s = o_s.reshape(db, N_KV_HEADS, KV_GROUP, dq, N_KV_HEADS, HEAD_DIM)
    attn_s = jnp.stack([o_s[:, g, :, :, g, :] for g in range(N_KV_HEADS)], axis=1)
    attn_s = attn_s.transpose(0, 3, 1, 2, 4).reshape(ns_rows, Q_W)
    h_s, hb_s, route_s = _mix_call(attn_s, u, hist_s, x_sample.reshape(ns_rows, d), mix_w, row_off=np_rows, tm=tm,
                                   nseq=tm // dq, grid=(ns_rows // tm, 1))
    yg_s = experts(h_s, hb_s, route_s, MOE_ROWS_SAMPLE)
    y_p = _final_call(h_p, yg_p, route_p, p_prompt.reshape(np_rows, -1), fin_w, tm)
    y_s = _final_call(h_s, yg_s, route_s, p_sample.reshape(ns_rows, -1), fin_w, tm)

    u_p = u[:np_rows].reshape(nb, t, ch)
    u_s = u[np_rows:].reshape(db, dq, ch)
    keep = CONV_WIDTH - 1
    conv_p = jnp.concatenate([jnp.zeros((nb, keep, ch), F32), u_p], axis=1)[:, -keep:]
    conv_s = jnp.concatenate([state_conv, u_s], axis=1)[:, -keep:]
    kv = lambda a, lo, hi, b_, t_: a[lo:hi].reshape(b_, t_, N_KV_HEADS, HEAD_DIM)
    return (y_p.reshape(nb, t, d), y_s.reshape(db, dq, d),
            kv(k_f, 0, np_rows, nb, t), kv(v_f, 0, np_rows, nb, t),
            kiwi[:np_rows, :IDX_DIM].reshape(nb, t, IDX_DIM), conv_p,
            kv(k_f, np_rows, n, db, dq), kv(v_f, np_rows, n, db, dq),
            kiwi[np_rows:, :IDX_DIM].reshape(db, dq, IDX_DIM), conv_s)


def kernel(x_prompt, x_sample, p_prompt, p_sample, cache_k, cache_v, cache_kidx, state_conv, page_table, w_in, b_in, w_o, b_o, ln1_g, ln1_b, conv_w, conv_b, conv_ln_g, conv_ln_b, w_router, b_router, w_gate_up, b_gate_up, w_down, b_down, ln2_g, ln2_b, w_ple_gate, b_ple_gate, w_ple_proj, ln3_g, ln3_b):
    assert w_in.shape[0] == DEPTH
    outs = _layer(x_prompt, x_sample, p_prompt[0], p_sample[0], cache_k[0], cache_v[0], cache_kidx[0],
                  state_conv[0], page_table, w_in[0], b_in[0], w_o[0], b_o[0], ln1_g[0], ln1_b[0],
                  conv_w[0], conv_b[0], conv_ln_g[0], conv_ln_b[0], w_router[0], b_router[0],
                  w_gate_up[0], b_gate_up[0], w_down[0], b_down[0], ln2_g[0], ln2_b[0],
                  w_ple_gate[0], b_ple_gate[0], w_ple_proj[0], ln3_g[0], ln3_b[0])
    y_p, y_s = outs[0], outs[1]
    return (y_p, y_s) + tuple(o[None] for o in outs[2:])
```

```python
import functools
import math

import jax
import jax.numpy as jnp
from jax import lax
from jax.experimental import pallas as pl
from jax.experimental.pallas import tpu as pltpu

F32 = jnp.float32
BF16 = jnp.bfloat16
I32 = jnp.int32

HEAD_DIM = 64
N_HEADS = 8
N_KV_HEADS = 2
KV_GROUP = N_HEADS // N_KV_HEADS
IDX_HEADS = 4
IDX_DIM = 64
INDEX_TOPK = 256
ROPE_THETA = 500000.0
ROPE_HALF = HEAD_DIM // 8
CONV_WIDTH = 31
N_EXPERTS = 32
TOP_K = 4
SWIGLU_LIMIT = 7.0
SWIGLU_ALPHA = 1.702
LN_EPS = 1e-5
DEPTH = 1
DN_ALPHA = (2 * DEPTH) ** 0.25
ATTN_SCALE = HEAD_DIM ** -0.5
IDX_W_SCALE = (IDX_HEADS * IDX_DIM) ** -0.5
LOG2E = math.log2(math.e)

LANES = 128
SUBLANES = 8
VMEM_LIMIT = 56 * 1024 * 1024
NEG_BIG = -1e30
INT_MIN = -(2 ** 31)

Q_W = N_HEADS * HEAD_DIM
KV_W = N_KV_HEADS * HEAD_DIM
QI_W = IDX_HEADS * IDX_DIM
VT_ROWS = HEAD_DIM + 16


def _cparams(sem):
    return pltpu.CompilerParams(dimension_semantics=sem, vmem_limit_bytes=VMEM_LIMIT)


def _rope_slab(z, cos, sin):
    lane = lax.broadcasted_iota(I32, (1, LANES), 1)
    first = (lane % HEAD_DIM) < ROPE_HALF
    up = pltpu.roll(z, LANES - ROPE_HALF, axis=1)
    dn = pltpu.roll(z, ROPE_HALF, axis=1)
    return z * cos + jnp.where(first, up, dn) * sin


def _inproj_kernel(xp_ref, xs_ref, w_ref, b_ref, cos_ref, sin_ref,
                   q_ref, kf_ref, vf_ref, kb_ref, vt_ref, qi_ref, kiwi_ref, kib_ref, u_ref,
                   ktf_ref, vtf_ref, kitf_ref, *, conv_ch, tiles_p):
    x = jnp.where(pl.program_id(0) < tiles_p, xp_ref[...], xs_ref[...]).astype(BF16)
    cos = cos_ref[...]
    sin = sin_ref[...]

    def proj(lo, width):
        return jnp.dot(x, w_ref[:, lo:lo + width], preferred_element_type=F32) + b_ref[:, lo:lo + width]

    def rope(z):
        n = z.shape[1] // LANES
        slabs = [_rope_slab(z[:, s * LANES:(s + 1) * LANES], cos, sin) for s in range(n)]
        return slabs[0] if n == 1 else jnp.concatenate(slabs, axis=1)

    o = 0
    q = rope(proj(o, Q_W))
    q_ref[...] = (q * (ATTN_SCALE * LOG2E)).astype(BF16)
    o += Q_W
    k = rope(proj(o, KV_W))
    kf_ref[...] = k
    kb_ref[...] = k.astype(BF16)
    o += KV_W
    v = proj(o, KV_W)
    vf_ref[...] = v
    vt = v.T
    ones = jnp.ones((VT_ROWS - HEAD_DIM, vt.shape[1]), F32)
    vt_ref[0] = jnp.concatenate(
        [piece for g in range(N_KV_HEADS) for piece in (vt[g * HEAD_DIM:(g + 1) * HEAD_DIM], ones)],
        axis=0).astype(BF16)
    o += KV_W
    qi_ref[...] = rope(proj(o, QI_W)).astype(BF16)
    o += QI_W
    lane = lax.broadcasted_iota(I32, (1, LANES), 1)
    is_ki = lane < IDX_DIM
    kiwi = _rope_slab(proj(o, LANES), jnp.where(is_ki, cos, 1.0), jnp.where(is_ki, sin, 0.0))
    kiwi = kiwi * jnp.where((lane >= IDX_DIM) & (lane < IDX_DIM + IDX_HEADS), IDX_W_SCALE, 1.0)
    kiwi_ref[...] = kiwi
    kib_ref[...] = kiwi[:, :IDX_DIM].astype(BF16)

    @pl.when(pl.program_id(0) < tiles_p)
    def _():
        ktf_ref[0] = k.T
        vtf_ref[0] = vt
        kitf_ref[0] = kiwi.T[:IDX_DIM]
    o += LANES
    a = proj(o, conv_ch)
    g = proj(o + conv_ch, conv_ch)
    u_ref[...] = a * jax.nn.sigmoid(g)


def _inproj_call(x_p, x_s, w_pad, b_pad, cos_tab, sin_tab, tab_index, tm, seq_tiles):
    d = x_p.shape[1]
    tiles_p, tiles_s = x_p.shape[0] // tm, x_s.shape[0] // tm
    nt = tiles_p + tiles_s
    n = nt * tm
    nw = w_pad.shape[1]
    conv_ch = (nw - (Q_W + 2 * KV_W + QI_W + LANES)) // 2
    row = lambda i: (i, 0)
    const = lambda i: (0, 0)
    out_shapes = (
        jax.ShapeDtypeStruct((n, Q_W), BF16),
        jax.ShapeDtypeStruct((n, KV_W), F32),
        jax.ShapeDtypeStruct((n, KV_W), F32),
        jax.ShapeDtypeStruct((n, KV_W), BF16),
        jax.ShapeDtypeStruct((nt, N_KV_HEADS * VT_ROWS, tm), BF16),
        jax.ShapeDtypeStruct((n, QI_W), BF16),
        jax.ShapeDtypeStruct((n, LANES), F32),
        jax.ShapeDtypeStruct((n, IDX_DIM), BF16),
        jax.ShapeDtypeStruct((n, conv_ch), F32),
        jax.ShapeDtypeStruct((tiles_p // seq_tiles, KV_W, seq_tiles * tm), F32),
        jax.ShapeDtypeStruct((tiles_p // seq_tiles, KV_W, seq_tiles * tm), F32),
        jax.ShapeDtypeStruct((tiles_p // seq_tiles, IDX_DIM, seq_tiles * tm), F32),
    )
    last_p = tiles_p - 1
    seq_blk = lambda i: (jnp.minimum(i, last_p) // seq_tiles, 0, jnp.minimum(i, last_p) % seq_tiles)
    out_specs = (
        pl.BlockSpec((tm, Q_W), row),
        pl.BlockSpec((tm, KV_W), row),
        pl.BlockSpec((tm, KV_W), row),
        pl.BlockSpec((tm, KV_W), row),
        pl.BlockSpec((1, N_KV_HEADS * VT_ROWS, tm), lambda i: (i, 0, 0)),
        pl.BlockSpec((tm, QI_W), row),
        pl.BlockSpec((tm, LANES), row),
        pl.BlockSpec((tm, IDX_DIM), row),
        pl.BlockSpec((tm, conv_ch), row),
        pl.BlockSpec((1, KV_W, tm), seq_blk),
        pl.BlockSpec((1, KV_W, tm), seq_blk),
        pl.BlockSpec((1, IDX_DIM, tm), seq_blk),
    )
    return pl.pallas_call(
        functools.partial(_inproj_kernel, conv_ch=conv_ch, tiles_p=tiles_p),
        grid=(nt,),
        in_specs=[
            pl.BlockSpec((tm, d), lambda i: (jnp.minimum(i, tiles_p - 1), 0)),
            pl.BlockSpec((tm, d), lambda i: (jnp.maximum(i - tiles_p, 0), 0)),
            pl.BlockSpec((d, nw), const),
            pl.BlockSpec((1, nw), const),
            pl.BlockSpec((tm, LANES), lambda i: (tab_index(i), 0)),
            pl.BlockSpec((tm, LANES), lambda i: (tab_index(i), 0)),
        ],
        out_specs=out_specs,
        out_shape=out_shapes,
        compiler_params=_cparams(("arbitrary",)),
    )(x_p, x_s, w_pad, b_pad, cos_tab, sin_tab)


def _sortable(score):
    bits = lax.bitcast_convert_type(score, I32)
    key = bits ^ ((bits >> 31) & 0x7FFFFFFF)
    return jnp.where(score == 0.0, 0, key)


def _kth_largest(count_ge, shape, n_sel):
    def bit_body(t, thr):
        cand = thr + jnp.left_shift(jnp.int32(1), 31 - t)
        return jnp.where(count_ge(cand) >= n_sel, cand, thr)

    return lax.fori_loop(0, 32, bit_body, jnp.full(shape, INT_MIN, I32))


def _tie_bound(count_eq_below, need, shape, idx_bits):
    def bit_body(t, j0):
        cand = j0 + jnp.left_shift(jnp.int32(1), idx_bits - 1 - t)
        return jnp.where(count_eq_below(cand) < need, cand, j0)

    return lax.fori_loop(0, idx_bits, bit_body, jnp.zeros(shape, I32))


NO_BOUND = 2 ** 30
PLANE_KEYS = 32 * SUBLANES


def _bit_transpose32(rows):
    a = list(rows[::-1])
    j, m = 16, 0x0000FFFF
    while j:
        k = 0
        while k < 32:
            t = (a[k] ^ (a[k + j] >> j)) & m
            a[k] = a[k] ^ t
            a[k + j] = a[k + j] ^ (t << j)
            k = (k + j + 1) & ~j
        j >>= 1
        m = (m ^ (m << j)) & 0xFFFFFFFF
    return a[::-1]


def _plane_order(n):
    p = jnp.arange(n, dtype=I32)
    within = p % PLANE_KEYS
    return p - within + (within % SUBLANES) * 32 + within // SUBLANES


N_PAIRS = N_HEADS // 2
PAIRS_PER_GROUP = KV_GROUP // 2


def _prompt_attn_kernel(q_ref, qi_ref, kiwi_ref, kib_ref, koff_ref, kb_ref, vt_ref, o_ref,
                        qst_ref, qit_ref, pln_ref, cand_ref, top_ref, sel_ref, bias_ref, j0_ref,
                        lg0_ref, lg1_ref, mx0_ref, mx1_ref, pr0_ref, pr1_ref, al0_ref, al1_ref, m_ref, acc_ref,
                        *, qb, kc, n_sel, idx_bits):
    i = pl.program_id(1)
    n_chunks = (i * qb + qb + kc - 1) // kc
    q_pos = i * qb + lax.broadcasted_iota(I32, (1, qb), 1)

    def key_pos(k0):
        return k0 + lax.broadcasted_iota(I32, (kc, 1), 0)

    qt = q_ref[...].astype(F32).T
    zero = jnp.zeros((HEAD_DIM, 2 * qb), F32)
    for p in range(N_PAIRS):
        g = p // PAIRS_PER_GROUP
        pair = jnp.concatenate([qt[(2 * p) * HEAD_DIM:(2 * p + 1) * HEAD_DIM],
                                qt[(2 * p + 1) * HEAD_DIM:(2 * p + 2) * HEAD_DIM]], axis=1)
        qst_ref[p] = jnp.concatenate([pair if gg == g else zero for gg in range(N_KV_HEADS)], axis=0).astype(BF16)
    qit = qi_ref[...].astype(F32).T
    qit_ref[...] = jnp.concatenate(
        [qit[h * IDX_DIM:(h + 1) * IDX_DIM] for h in range(IDX_HEADS)], axis=1).astype(BF16)
    wit = kiwi_ref[...].T[IDX_DIM:IDX_DIM + SUBLANES]

    planes_per_chunk = kc // PLANE_KEYS

    @pl.when(i == 0)
    def _():
        pln_ref[...] = jnp.zeros(pln_ref.shape, I32)

    def score_body(c, carry):
        k0 = pl.multiple_of(c * kc, kc)
        s = jnp.dot(kib_ref[pl.ds(k0, kc), :], qit_ref[...], preferred_element_type=F32)
        score = jnp.zeros((kc, qb), F32)
        for h in range(IDX_HEADS):
            score = score + wit[h:h + 1, :] * jnp.maximum(s[:, h * qb:(h + 1) * qb], 0.0)
        score = jnp.where(k0 + koff_ref[...] <= q_pos, score, -jnp.inf)
        key = _sortable(score)
        for hb in range(planes_per_chunk):
            base = hb * PLANE_KEYS
            planes = _bit_transpose32([key[base + j * SUBLANES:base + (j + 1) * SUBLANES] for j in range(32)])
            planes[31] = ~planes[31]
            r0 = pl.multiple_of((c * planes_per_chunk + hb) * SUBLANES, SUBLANES)
            for b in range(32):
                pln_ref[b, pl.ds(r0, SUBLANES), :] = planes[b]
        return carry

    lax.fori_loop(0, n_chunks, score_body, 0)

    n_rows = pln_ref.shape[1]
    row = lax.broadcasted_iota(I32, (n_rows, 1), 0)
    cand_ref[...] = jnp.broadcast_to(jnp.where(row < n_chunks * (kc // 32), -1, 0), (n_rows, qb))
    top_ref[...] = jnp.zeros((n_rows, qb), I32)

    def bit_count(words):
        cnt = lax.population_count(words)
        return jnp.sum(jnp.sum(cnt.reshape(n_rows // SUBLANES, SUBLANES, qb), axis=0), axis=0, keepdims=True)

    def bit_body(tb, need):
        plane = pln_ref[31 - tb]
        cand = cand_ref[...]
        ones = cand & plane
        cnt = bit_count(ones)
        take = cnt >= need
        skip = jnp.where(take, 0, -1)
        top_ref[...] = top_ref[...] | (ones & skip)
        cand_ref[...] = cand & (plane ^ skip)
        return jnp.where(take, need, need - cnt)

    need = lax.fori_loop(0, 32, bit_body, jnp.full((1, qb), n_sel, I32))
    tie = bit_count(cand_ref[...]) > need

    def low_mask(bound):
        d = bound - row * 32
        return jnp.where(d >= 32, -1, jnp.left_shift(1, jnp.clip(d, 0, 31)) - 1)

    j0_ref[...] = jnp.full((1, qb), NO_BOUND, I32)

    @pl.when(jnp.max(jnp.where(tie, 1, 0)) > 0)
    def _():
        j0 = _tie_bound(lambda bound: bit_count(cand_ref[...] & low_mask(bound)), need, (1, qb), idx_bits)
        j0_ref[...] = jnp.where(tie, j0 + 1, NO_BOUND)

    sel_ref[...] = top_ref[...] | (cand_ref[...] & low_mask(j0_ref[...]))

    words_per_chunk = kc // 32

    def bias_body(c, carry):
        k0 = pl.multiple_of(c * kc, kc)
        words = sel_ref[pl.ds(pl.multiple_of(c * words_per_chunk, words_per_chunk), words_per_chunk), :]
        sub = lax.broadcasted_iota(I32, (SUBLANES, 1), 0)
        pieces = []
        for v in range(kc // SUBLANES):
            word = jnp.broadcast_to(words[v // 4:v // 4 + 1, :], (SUBLANES, qb))
            pieces.append(lax.shift_right_logical(word, (v % 4) * SUBLANES + sub) & 1)
        chosen = jnp.concatenate(pieces, axis=0) == 1
        bias_ref[pl.ds(k0, kc), :] = jnp.where(chosen & (key_pos(k0) <= q_pos), 0.0, NEG_BIG)
        return carry

    lax.fori_loop(0, n_chunks, bias_body, 0)

    m_ref[...] = jnp.full(m_ref.shape, NEG_BIG, F32)
    acc_ref[...] = jnp.zeros(acc_ref.shape, F32)
    lgs, mxs, prs, als = (lg0_ref, lg1_ref), (mx0_ref, mx1_ref), (pr0_ref, pr1_ref), (al0_ref, al1_ref)

    bias_ref[pl.ds(pl.multiple_of(n_chunks * kc, kc), kc), :] = jnp.full((kc, qb), NEG_BIG, F32)

    def logits(c, slot):
        k0 = pl.multiple_of(jnp.minimum(c, n_chunks - 1) * kc, kc)
        b0 = pl.multiple_of(jnp.minimum(c, n_chunks) * kc, kc)
        for p in range(N_PAIRS):
            lg = jnp.dot(kb_ref[pl.ds(k0, kc), :], qst_ref[p], preferred_element_type=F32)
            halves = [lg[:, h * qb:(h + 1) * qb] + bias_ref[pl.ds(b0, kc), :] for h in range(2)]
            for h in range(2):
                lgs[slot][p, :, h * qb:(h + 1) * qb] = halves[h]
            mxs[slot][p] = jnp.concatenate([jnp.max(x, axis=0, keepdims=True) for x in halves], axis=1)

    def probs(slot):
        for p in range(N_PAIRS):
            m_old = m_ref[p]
            m_new = jnp.maximum(m_old, mxs[slot][p])
            als[slot][p] = jnp.exp2(m_old - m_new)
            prs[slot][p] = jnp.exp2(lgs[slot][p] - m_new).astype(BF16)
            m_ref[p] = m_new

    def weighted(c, slot):
        cc = jnp.clip(c, 0, n_chunks - 1)
        for p in range(N_PAIRS):
            g = p // PAIRS_PER_GROUP
            vt = vt_ref[cc, g * VT_ROWS:(g + 1) * VT_ROWS, :]
            acc_ref[p] = als[slot][p] * acc_ref[p] + jnp.dot(vt, prs[slot][p], preferred_element_type=F32)

    logits(0, 0)
    pr1_ref[...] = jnp.zeros(pr1_ref.shape, BF16)
    al1_ref[...] = jnp.ones(al1_ref.shape, F32)

    def attn_body(j, carry):
        c = 2 * j
        logits(c + 1, 1)
        probs(0)
        weighted(c - 1, 1)
        logits(c + 2, 0)
        probs(1)
        weighted(c, 0)
        return carry

    lax.fori_loop(0, (n_chunks + 1) // 2, attn_body, 0)
    weighted(n_chunks - 1, 1)

    heads = []
    for p in range(N_PAIRS):
        acc = acc_ref[p]
        o = acc[:HEAD_DIM] * (1.0 / acc[HEAD_DIM:HEAD_DIM + 1])
        heads += [o[:, :qb], o[:, qb:]]
    o_ref[...] = jnp.concatenate(heads, axis=0).T.astype(o_ref.dtype)


def _prompt_attn_call(q, qi, kiwi, kib, kb, vt, nb, t, qb, kc):
    assert kc % PLANE_KEYS == 0 and t % kc == 0
    nqb = t // qb
    n_sel = min(INDEX_TOPK, t // 4)
    blk = lambda b, i: (b * nqb + i, 0)
    seq = lambda b, i: (b, 0)
    koff = jnp.broadcast_to(_plane_order(kc)[:, None], (kc, qb))
    kern = functools.partial(_prompt_attn_kernel, qb=qb, kc=kc, n_sel=n_sel, idx_bits=max(1, (t - 1).bit_length()))
    return pl.pallas_call(
        kern,
        grid=(nb, nqb),
        in_specs=[
            pl.BlockSpec((qb, Q_W), blk),
            pl.BlockSpec((qb, QI_W), blk),
            pl.BlockSpec((qb, LANES), blk),
            pl.BlockSpec((t, IDX_DIM), seq),
            pl.BlockSpec((kc, qb), lambda b, i: (0, 0)),
            pl.BlockSpec((t, KV_W), seq),
            pl.BlockSpec((t // kc, N_KV_HEADS * VT_ROWS, kc), lambda b, i: (b, 0, 0)),
        ],
        out_specs=pl.BlockSpec((qb, Q_W), blk),
        out_shape=jax.ShapeDtypeStruct((nb * t, Q_W), BF16),
        scratch_shapes=[
            pltpu.VMEM((N_PAIRS, KV_W, 2 * qb), BF16),
            pltpu.VMEM((IDX_DIM, IDX_HEADS * qb), BF16),
            pltpu.VMEM((32, t // 32, qb), I32),
            pltpu.VMEM((t // 32, qb), I32),
            pltpu.VMEM((t // 32, qb), I32),
            pltpu.VMEM((t // 32, qb), I32),
            pltpu.VMEM((t + kc, qb), F32),
            pltpu.VMEM((1, qb), I32),
            pltpu.VMEM((N_PAIRS, kc, 2 * qb), F32),
            pltpu.VMEM((N_PAIRS, kc, 2 * qb), F32),
            pltpu.VMEM((N_PAIRS, 1, 2 * qb), F32),
            pltpu.VMEM((N_PAIRS, 1, 2 * qb), F32),
            pltpu.VMEM((N_PAIRS, kc, 2 * qb), BF16),
            pltpu.VMEM((N_PAIRS, kc, 2 * qb), BF16),
            pltpu.VMEM((N_PAIRS, 1, 2 * qb), F32),
            pltpu.VMEM((N_PAIRS, 1, 2 * qb), F32),
            pltpu.VMEM((N_PAIRS, 1, 2 * qb), F32),
            pltpu.VMEM((N_PAIRS, VT_ROWS, 2 * qb), F32),
        ],
        compiler_params=_cparams(("parallel", "arbitrary")),
    )(q, qi, kiwi, kib, koff, kb, vt)


PAGES_PER_DOT = 4


def _sample_attn_kernel(pt_ref, q_ref, qi_ref, kiwi_ref, kn_ref, vn_ref, cki_hbm, ck_hbm, cv_hbm, o_ref,
                        kit_buf, kt_buf, vt_buf, sem, kin_s, knew_s, vnew_s, keys_ref, bias_ref, lg_ref, j0_ref,
                        *, n_pages, dq, n_sel, idx_bits):
    b = pl.program_id(0)
    nb = pl.num_programs(0)
    slot = b % 2
    past = n_pages * LANES
    lp = past + LANES
    streams = ((cki_hbm, kit_buf), (ck_hbm, kt_buf), (cv_hbm, vt_buf))

    def page_copy(a, seq, pg, sl):
        src, dst = streams[a]
        return pltpu.make_async_copy(src.at[pt_ref[seq, pg]], dst.at[sl, pg], sem.at[sl, a])

    def fetch(seq, sl):
        def body(pg, carry):
            for a in range(len(streams)):
                page_copy(a, seq, pg, sl).start()
            return carry

        lax.fori_loop(0, n_pages, body, 0)

    @pl.when(b == 0)
    def _():
        fetch(0, 0)

    @pl.when(b + 1 < nb)
    def _():
        fetch(b + 1, 1 - slot)

    def wait_body(pg, carry):
        for a in range(len(streams)):
            page_copy(a, b, pg, slot).wait()
        return carry

    lax.fori_loop(0, n_pages, wait_body, 0)

    def pages(buf, c):
        return jnp.concatenate([buf[slot, c * PAGES_PER_DOT + r] for r in range(PAGES_PER_DOT)], axis=1).astype(BF16)

    kc = PAGES_PER_DOT * LANES
    q = q_ref[0]
    qi = qi_ref[0]
    kiwi = kiwi_ref[0]
    qrow = lax.broadcasted_iota(I32, (dq, 1), 0)
    idx = lax.broadcasted_iota(I32, (1, lp), 1)
    rel = idx - past
    adm = (idx < past) | ((rel <= qrow) & (rel < dq))

    kin_s[...] = jnp.zeros(kin_s.shape, F32)
    knew_s[...] = jnp.zeros(knew_s.shape, F32)
    vnew_s[...] = jnp.zeros(vnew_s.shape, F32)
    kin_s[0:dq, :] = kiwi[:, :IDX_DIM]
    knew_s[0:dq, :] = kn_ref[0]
    vnew_s[0:dq, :] = vn_ref[0]

    nt = (((1,), (1,)), ((), ()))
    qi_rows = jnp.concatenate([qi[:, h * IDX_DIM:(h + 1) * IDX_DIM] for h in range(IDX_HEADS)], axis=0).astype(BF16)
    wi_col = jnp.concatenate([kiwi[:, IDX_DIM + h:IDX_DIM + h + 1] for h in range(IDX_HEADS)], axis=0)

    def scores(s):
        r = jnp.maximum(s, 0.0) * wi_col
        out = r[0:dq]
        for h in range(1, IDX_HEADS):
            out = out + r[h * dq:(h + 1) * dq]
        return out

    for c in range(n_pages // PAGES_PER_DOT):
        s = jnp.dot(qi_rows, pages(kit_buf, c), preferred_element_type=F32)
        keys_ref[:, c * kc:(c + 1) * kc] = _sortable(scores(s))
    s_new = scores(lax.dot_general(qi_rows, kin_s[...].astype(BF16), nt, preferred_element_type=F32))
    keys_ref[:, past:] = _sortable(jnp.where(adm[:, past:], s_new, -jnp.inf))

    def count(pred):
        return jnp.sum(jnp.where(pred(keys_ref[...]), 1, 0).astype(I32), axis=1, keepdims=True)

    thr = _kth_largest(lambda cand: count(lambda k: k >= cand), (dq, 1), n_sel)
    need = n_sel - count(lambda k: k > thr)
    tie = count(lambda k: k >= thr) > n_sel
    j0_ref[...] = jnp.full((dq, 1), NO_BOUND, I32)

    @pl.when(jnp.max(jnp.where(tie, 1, 0)) > 0)
    def _():
        j0 = _tie_bound(lambda cand: count(lambda k: (k == thr) & (idx < cand)), need, (dq, 1), idx_bits)
        j0_ref[...] = jnp.where(tie, j0, NO_BOUND)

    j0 = j0_ref[...]
    keys = keys_ref[...]
    sel = ((keys > thr) | ((keys == thr) & (idx <= j0))) & adm
    bias_ref[...] = jnp.where(sel, 0.0, NEG_BIG)

    zero = jnp.zeros((dq, HEAD_DIM), F32)
    rows = []
    for g in range(N_KV_HEADS):
        for r in range(KV_GROUP):
            h = g * KV_GROUP + r
            piece = q[:, h * HEAD_DIM:(h + 1) * HEAD_DIM]
            rows.append(jnp.concatenate([piece, zero] if g == 0 else [zero, piece], axis=1))
    qbd = jnp.concatenate(rows, axis=0).astype(BF16)

    for c in range(n_pages // PAGES_PER_DOT):
        lg_ref[:, c * kc:(c + 1) * kc] = jnp.dot(qbd, pages(kt_buf, c), preferred_element_type=F32)
    lg_ref[:, past:] = lax.dot_general(qbd, knew_s[...].astype(BF16), nt, preferred_element_type=F32)

    bias = bias_ref[...]
    lg = lg_ref[...] + jnp.concatenate([bias] * N_HEADS, axis=0)
    m = jnp.max(lg, axis=1, keepdims=True)
    p = jnp.exp2(lg - m)
    l = jnp.sum(p, axis=1, keepdims=True)
    lg_ref[...] = p
    o = jnp.dot(lg_ref[:, past:].astype(BF16), vnew_s[...].astype(BF16), preferred_element_type=F32)
    for c in range(n_pages // PAGES_PER_DOT):
        o = o + lax.dot_general(lg_ref[:, c * kc:(c + 1) * kc].astype(BF16), pages(vt_buf, c), nt,
                                preferred_element_type=F32)
    o_ref[0] = o * (1.0 / l)


def _sample_attn_call(page_table, q, qi, kiwi, kn, vn, cki_t, ck_t, cv_t):
    nb, dq, _ = q.shape
    n_pages = page_table.shape[1]
    assert cki_t.shape[2] == LANES and n_pages % PAGES_PER_DOT == 0
    lp = (n_pages + 1) * LANES
    n_sel = min(INDEX_TOPK, (n_pages * LANES + dq) // 4)
    b3 = lambda b, pt: (b, 0, 0)
    kern = functools.partial(_sample_attn_kernel, n_pages=n_pages, dq=dq, n_sel=n_sel,
                             idx_bits=max(1, (lp - 1).bit_length()))
    grid_spec = pltpu.PrefetchScalarGridSpec(
        num_scalar_prefetch=1,
        grid=(nb,),
        in_specs=[
            pl.BlockSpec((1, dq, Q_W), b3),
            pl.BlockSpec((1, dq, QI_W), b3),
            pl.BlockSpec((1, dq, LANES), b3),
            pl.BlockSpec((1, dq, KV_W), b3),
            pl.BlockSpec((1, dq, KV_W), b3),
            pl.BlockSpec(memory_space=pl.ANY),
            pl.BlockSpec(memory_space=pl.ANY),
            pl.BlockSpec(memory_space=pl.ANY),
        ],
        out_specs=pl.BlockSpec((1, N_HEADS * dq, KV_W), b3),
        scratch_shapes=[
            pltpu.VMEM((2, n_pages, IDX_DIM, LANES), F32),
            pltpu.VMEM((2, n_pages, KV_W, LANES), F32),
            pltpu.VMEM((2, n_pages, KV_W, LANES), F32),
            pltpu.SemaphoreType.DMA((2, 3)),
            pltpu.VMEM((LANES, IDX_DIM), F32),
            pltpu.VMEM((LANES, KV_W), F32),
            pltpu.VMEM((LANES, KV_W), F32),
            pltpu.VMEM((dq, lp), I32),
            pltpu.VMEM((dq, lp), F32),
            pltpu.VMEM((N_HEADS * dq, lp), F32),
            pltpu.VMEM((dq, 1), I32),
        ],
    )
    return pl.pallas_call(
        kern,
        grid_spec=grid_spec,
        out_shape=jax.ShapeDtypeStruct((nb, N_HEADS * dq, KV_W), F32),
        compiler_params=_cparams(("arbitrary",)),
    )(page_table, q, qi, kiwi, kn, vn, cki_t, ck_t, cv_t)


def _layer_norm(x, g, b):
    mu = jnp.mean(x, axis=-1, keepdims=True)
    xc = x - mu
    var = jnp.mean(xc * xc, axis=-1, keepdims=True)
    return xc * lax.rsqrt(var + LN_EPS) * g + b


HIST_ROWS = 32
HIST_PAD = HIST_ROWS - (CONV_WIDTH - 1)
CONV_ROWS = 64


def _mix_kernel(attn_ref, u_ref, uprev_ref, hist_ref, x_ref, cw_ref, cb_ref, cg_ref, cbb_ref, wo_ref, bo_ref,
                g1_ref, b1_ref, wr_ref, br_ref, h_ref, hb_ref, route_ref, xp_ref, xs_ref, conv_ref,
                *, tm, nseq, use_prev):
    rows = tm // nseq
    ch = u_ref.shape[1]

    if use_prev:
        first = pl.program_id(1) == 0

        @pl.when(first)
        def _():
            xp_ref[:, 0:HIST_ROWS, :] = hist_ref[...]

        @pl.when(jnp.logical_not(first))
        def _():
            xp_ref[0, 0:HIST_ROWS, :] = uprev_ref[...]
    else:
        xp_ref[:, 0:HIST_ROWS, :] = hist_ref[...]
    xp_ref[:, HIST_ROWS:, :] = u_ref[...].reshape(nseq, rows, ch)

    rs = min(rows, CONV_ROWS)
    span = HIST_ROWS + rows - SUBLANES

    def seq_body(s, carry):
        for r in range(1, SUBLANES):
            xs_ref[r - 1] = xp_ref[s, r:r + span, :]
        for r0 in range(0, rows, rs):
            acc = jnp.zeros((rs, ch), F32)
            for j in range(CONV_WIDTH):
                r = (HIST_PAD + j) % SUBLANES
                lo = r0 + HIST_PAD + j - r
                win = xp_ref[s, lo:lo + rs, :] if r == 0 else xs_ref[r - 1, lo:lo + rs, :]
                acc = acc + cw_ref[j:j + 1, :] * win
            conv_ref[pl.ds(pl.multiple_of(s * rows + r0, SUBLANES), rs), :] = acc
        return carry

    if nseq == 1:
        seq_body(0, 0)
    else:
        lax.fori_loop(0, nseq, seq_body, 0)

    y = _layer_norm(conv_ref[...] + cb_ref[...], cg_ref[...], cbb_ref[...])
    conv = (y * jax.nn.sigmoid(y)).astype(BF16)
    aw = attn_ref.shape[1]
    mixed = (jnp.dot(attn_ref[...].astype(BF16), wo_ref[0:aw, :], preferred_element_type=F32)
             + jnp.dot(conv, wo_ref[aw:, :], preferred_element_type=F32) + bo_ref[...])
    h = _layer_norm(DN_ALPHA * x_ref[...] + mixed, g1_ref[...], b1_ref[...])
    h_ref[...] = h
    hb = h.astype(BF16)
    hb_ref[...] = hb

    logits = jnp.dot(hb, wr_ref[...], preferred_element_type=F32) + br_ref[...]
    lane = lax.broadcasted_iota(I32, (1, LANES), 1)
    work = logits
    vals, ids = [], []
    for _ in range(TOP_K):
        mx = jnp.max(work, axis=1, keepdims=True)
        ix = jnp.min(jnp.where(work == mx, lane, LANES), axis=1, keepdims=True)
        vals.append(mx)
        ids.append(ix)
        work = jnp.where(lane == ix, -jnp.inf, work)
    ex = [jnp.exp(v - vals[0]) for v in vals]
    den = ex[0]
    for e in ex[1:]:
        den = den + e
    route = jnp.zeros((tm, LANES), F32)
    for k in range(TOP_K):
        route = jnp.where(lane == k, ex[k] / den, route)
        route = jnp.where(lane == TOP_K + k, ids[k].astype(F32), route)
    route_ref[...] = route


def _mix_call(attn, u, hist, x, weights, *, row_off, tm, nseq, grid):
    cw, cb, cg, cbb, wo, bo, g1, b1, wr, br = weights
    n_rows, d = x.shape
    ch = u.shape[1]
    use_prev = grid[1] > 1
    tiles_per_seq = grid[1]
    off_t = row_off // tm
    tile = lambda b, i: (b * tiles_per_seq + i, 0)
    otile = lambda b, i: (off_t + b * tiles_per_seq + i, 0)
    const = lambda b, i: (0, 0)
    if use_prev:
        per_tile = tm // HIST_ROWS
        prev = lambda b, i: (jnp.maximum((off_t + b * tiles_per_seq + i) * per_tile - 1, 0), 0)
    else:
        prev = const
    in_specs = [
        pl.BlockSpec((tm, attn.shape[1]), tile),
        pl.BlockSpec((tm, ch), otile),
        pl.BlockSpec((HIST_ROWS, ch), prev),
        pl.BlockSpec((nseq, HIST_ROWS, ch), lambda b, i: (b, 0, 0)),
        pl.BlockSpec((tm, d), tile),
        pl.BlockSpec(cw.shape, const),
        pl.BlockSpec(cb.shape, const),
        pl.BlockSpec(cg.shape, const),
        pl.BlockSpec(cbb.shape, const),
        pl.BlockSpec(wo.shape, const),
        pl.BlockSpec(bo.shape, const),
        pl.BlockSpec(g1.shape, const),
        pl.BlockSpec(b1.shape, const),
        pl.BlockSpec(wr.shape, const),
        pl.BlockSpec(br.shape, const),
    ]
    kern = functools.partial(_mix_kernel, tm=tm, nseq=nseq, use_prev=use_prev)
    return pl.pallas_call(
        kern,
        grid=grid,
        in_specs=in_specs,
        out_specs=(
            pl.BlockSpec((tm, d), tile),
            pl.BlockSpec((tm, d), tile),
            pl.BlockSpec((tm, LANES), tile),
        ),
        out_shape=(
            jax.ShapeDtypeStruct((n_rows, d), F32),
            jax.ShapeDtypeStruct((n_rows, d), BF16),
            jax.ShapeDtypeStruct((n_rows, LANES), F32),
        ),
        scratch_shapes=[
            pltpu.VMEM((nseq, HIST_ROWS + tm // nseq, ch), F32),
            pltpu.VMEM((SUBLANES - 1, HIST_ROWS + tm // nseq - SUBLANES, ch), F32),
            pltpu.VMEM((tm, ch), F32),
        ],
        compiler_params=_cparams(("parallel", "arbitrary")),
    )(attn, u, u, hist, x, cw, cb, cg, cbb, wo, bo, g1, b1, wr, br)


CAST_ROWS = 128


def _moe_kernel(be_ref, nu_ref, x_ref, wgu_ref, bgu_ref, wd_ref, bd_ref, y_ref, wgu_s, wd_s):
    i = pl.program_id(0)
    used = i < nu_ref[0]
    changed = (i == 0) | (be_ref[i] != be_ref[jnp.maximum(i - 1, 0)])
    dff = wd_s.shape[0]

    @pl.when(used & changed)
    def _():
        def cast_gu(r, carry):
            r0 = pl.multiple_of(r * CAST_ROWS, CAST_ROWS)
            wgu_s[pl.ds(r0, CAST_ROWS), :] = wgu_ref[0, pl.ds(r0, CAST_ROWS), :].astype(BF16)
            return carry

        def cast_d(r, carry):
            r0 = pl.multiple_of(r * CAST_ROWS, CAST_ROWS)
            wd_s[pl.ds(r0, CAST_ROWS), :] = wd_ref[0, pl.ds(r0, CAST_ROWS), :].astype(BF16)
            return carry

        lax.fori_loop(0, wgu_s.shape[0] // CAST_ROWS, cast_gu, 0)
        lax.fori_loop(0, wd_s.shape[0] // CAST_ROWS, cast_d, 0)

    @pl.when(used)
    def _():
        gu = jnp.dot(x_ref[...], wgu_s[...], preferred_element_type=F32) + bgu_ref[0]
        gate = jnp.minimum(gu[:, :dff], SWIGLU_LIMIT)
        up = jnp.clip(gu[:, dff:], -SWIGLU_LIMIT, SWIGLU_LIMIT)
        act = (up + 1.0) * gate * jax.nn.sigmoid(SWIGLU_ALPHA * gate)
        y = jnp.dot(act.astype(BF16), wd_s[...], preferred_element_type=F32) + bd_ref[0]
        y_ref[...] = y.astype(y_ref.dtype)

    @pl.when(jnp.logical_not(used))
    def _():
        y_ref[...] = jnp.zeros(y_ref.shape, y_ref.dtype)


def _moe_call(block_e, n_used, x_pad, wgu, bgu, wd, bd, bm):
    p, d = x_pad.shape
    ne, _, dgu = wgu.shape
    dff = wd.shape[1]
    n_blocks = p // bm
    grid_spec = pltpu.PrefetchScalarGridSpec(
        num_scalar_prefetch=2,
        grid=(n_blocks,),
        in_specs=[
            pl.BlockSpec((bm, d), lambda i, be, nu: (i, 0)),
            pl.BlockSpec((1, d, dgu), lambda i, be, nu: (be[i], 0, 0)),
            pl.BlockSpec((1, 1, dgu), lambda i, be, nu: (be[i], 0, 0)),
            pl.BlockSpec((1, dff, d), lambda i, be, nu: (be[i], 0, 0)),
            pl.BlockSpec((1, 1, d), lambda i, be, nu: (be[i], 0, 0)),
        ],
        out_specs=pl.BlockSpec((bm, d), lambda i, be, nu: (i, 0)),
        scratch_shapes=[pltpu.VMEM((d, dgu), BF16), pltpu.VMEM((dff, d), BF16)],
    )
    return pl.pallas_call(
        _moe_kernel,
        grid_spec=grid_spec,
        out_shape=jax.ShapeDtypeStruct((p, d), BF16),
        compiler_params=_cparams(("arbitrary",)),
    )(block_e, n_used, x_pad, wgu, bgu.reshape(ne, 1, dgu), wd, bd.reshape(ne, 1, d))


def _dispatch_plan(route, bm):
    n = route.shape[0]
    nk = n * TOP_K
    experts = jnp.arange(N_EXPERTS, dtype=I32)
    e_flat = route[:, TOP_K:2 * TOP_K].astype(I32).reshape(-1)
    iota = jnp.arange(nk, dtype=I32)
    e_sorted, order = lax.sort((e_flat, iota), num_keys=1, is_stable=True)
    _, inverse = lax.sort((order, iota), num_keys=1)
    counts = jnp.sum((e_flat[:, None] == experts[None, :]).astype(I32), axis=0)
    padded = (counts + bm - 1) // bm * bm
    start = jnp.cumsum(counts) - counts
    pend = jnp.cumsum(padded)
    pstart = pend - padded
    slot_sorted = pstart[e_sorted] + iota - start[e_sorted]
    slot_orig = slot_sorted[inverse]
    n_blocks = -(-(nk + N_EXPERTS * (bm - 1)) // bm)
    block_start = jnp.arange(n_blocks, dtype=I32) * bm
    block_e = jnp.minimum(jnp.sum((pend[None, :] <= block_start[:, None]).astype(I32), axis=1), N_EXPERTS - 1)
    n_used = (pend[-1:] // bm).astype(I32)
    e_slot = jnp.repeat(block_e, bm)
    rank = jnp.arange(n_blocks * bm, dtype=I32) - pstart[e_slot]
    src = jnp.clip(start[e_slot] + rank, 0, nk - 1)
    slot_tok = jnp.where(rank < counts[e_slot], order[src] // TOP_K, 0)
    return slot_tok, block_e, n_used, slot_orig


def _final_kernel(h_ref, yg_ref, route_ref, pe_ref, g2_ref, b2_ref, wg_ref, bg_ref, wp_ref, g3_ref, b3_ref, o_ref):
    route = route_ref[...]
    y = route[:, 0:1] * yg_ref[0].astype(F32)
    for k in range(1, TOP_K):
        y = y + route[:, k:k + 1] * yg_ref[k].astype(F32)
    h2 = _layer_norm(DN_ALPHA * h_ref[...] + y, g2_ref[...], b2_ref[...])
    gate = jax.nn.sigmoid(jnp.dot(h2.astype(BF16), wg_ref[...], preferred_element_type=F32) + bg_ref[...])
    proj = jnp.dot(pe_ref[...].astype(BF16), wp_ref[...], preferred_element_type=F32)
    o_ref[...] = _layer_norm(DN_ALPHA * h2 + gate * proj, g3_ref[...], b3_ref[...])


def _final_call(h, yg, route, pe, weights, tm):
    g2, b2, wg, bg, wp, g3, b3 = weights
    n, pd = pe.shape
    d = h.shape[1]
    row = lambda i: (i, 0)
    const = lambda i: (0, 0)
    return pl.pallas_call(
        _final_kernel,
        grid=(n // tm,),
        in_specs=[
            pl.BlockSpec((tm, d), row),
            pl.BlockSpec((TOP_K, tm, d), lambda i: (0, i, 0)),
            pl.BlockSpec((tm, LANES), row),
            pl.BlockSpec((tm, pd), row),
            pl.BlockSpec(g2.shape, const),
            pl.BlockSpec(b2.shape, const),
            pl.BlockSpec(wg.shape, const),
            pl.BlockSpec(bg.shape, const),
            pl.BlockSpec(wp.shape, const),
            pl.BlockSpec(g3.shape, const),
            pl.BlockSpec(b3.shape, const),
        ],
        out_specs=pl.BlockSpec((tm, d), row),
        out_shape=jax.ShapeDtypeStruct((n, d), F32),
        compiler_params=_cparams(("parallel",)),
    )(h, yg, route, pe, g2, b2, wg, bg, wp, g3, b3)


TOKEN_TILE = 512
QUERY_BLOCK = 128
MOE_ROWS = 512
MOE_ROWS_SAMPLE = 128


def _rope_tables(pos):
    rd = 2 * ROPE_HALF
    inv_freq = ROPE_THETA ** (-jnp.arange(ROPE_HALF, dtype=F32) * 2.0 / rd)
    ang = pos.astype(F32)[:, None] * inv_freq[None, :]
    cos, sin = jnp.cos(ang), jnp.sin(ang)
    ones = jnp.ones((pos.shape[0], HEAD_DIM - rd), F32)
    cos_h = jnp.concatenate([cos, cos, ones], axis=1)
    sin_h = jnp.concatenate([-sin, sin, 0.0 * ones], axis=1)
    reps = LANES // HEAD_DIM
    return jnp.tile(cos_h, (1, reps)), jnp.tile(sin_h, (1, reps))


def _layer(x_prompt, x_sample, p_prompt, p_sample, cache_k, cache_v, cache_kidx, state_conv, page_table,
           w_in, b_in, w_o, b_o, ln1_g, ln1_b, conv_w, conv_b, conv_ln_g, conv_ln_b,
           w_router, b_router, w_gate_up, b_gate_up, w_down, b_down, ln2_g, ln2_b,
           w_ple_gate, b_ple_gate, w_ple_proj, ln3_g, ln3_b):
    nb, t, d = x_prompt.shape
    db, dq, _ = x_sample.shape
    n_pages, page = page_table.shape[1], cache_k.shape[1]
    past = n_pages * page
    ch = conv_w.shape[1]
    tm = min(TOKEN_TILE, t)
    np_rows, ns_rows = nb * t, db * dq
    n = np_rows + ns_rows
    assert t % tm == 0 and ns_rows % tm == 0 and tm % dq == 0 and dq == SUBLANES
    row2 = lambda a: a.reshape(1, -1)

    n_head = Q_W + 2 * KV_W + QI_W + IDX_DIM + IDX_HEADS
    pad = LANES - IDX_DIM - IDX_HEADS
    w_pad = jnp.concatenate([w_in[:, :n_head], jnp.zeros((d, pad), F32), w_in[:, n_head:]], axis=1).astype(BF16)
    b_pad = jnp.concatenate([b_in[:n_head], jnp.zeros((pad,), F32), b_in[n_head:]]).reshape(1, -1)
    pos = jnp.concatenate([jnp.arange(t, dtype=I32), past + (jnp.arange(tm, dtype=I32) % dq)])
    cos_tab, sin_tab = _rope_tables(pos)
    tiles_p = np_rows // tm
    tab_index = lambda i: jnp.where(i < tiles_p, i % (t // tm), t // tm)
    q, k_f, v_f, k_b, v_t, qi, kiwi, ki_b, u, kt_p, vt_p, kit_p = _inproj_call(
        x_prompt.reshape(np_rows, d), x_sample.reshape(ns_rows, d), w_pad, b_pad, cos_tab, sin_tab, tab_index, tm,
        t // tm)

    ki_planes = ki_b[:np_rows].reshape(np_rows // PLANE_KEYS, SUBLANES, 32, IDX_DIM)
    ki_planes = ki_planes.transpose(0, 2, 1, 3).reshape(np_rows, IDX_DIM)
    attn_p = _prompt_attn_call(q, qi, kiwi, ki_planes, k_b, v_t, nb, t, min(QUERY_BLOCK, t), tm)

    wr_pad = jnp.concatenate([w_router, jnp.zeros((d, LANES - N_EXPERTS), F32)], axis=1).astype(BF16)
    br_pad = jnp.concatenate([b_router, jnp.full((LANES - N_EXPERTS,), -jnp.inf, F32)]).reshape(1, -1)
    mix_w = (conv_w, row2(conv_b), row2(conv_ln_g), row2(conv_ln_b), w_o.astype(BF16), row2(b_o),
             row2(ln1_g), row2(ln1_b), wr_pad, br_pad)
    hist_p = jnp.zeros((nb, HIST_ROWS, ch), F32)
    hist_s = jnp.concatenate([jnp.zeros((db, HIST_PAD, ch), F32), state_conv], axis=1)
    fin_w = (row2(ln2_g), row2(ln2_b), w_ple_gate.astype(BF16), row2(b_ple_gate), w_ple_proj.astype(BF16),
             row2(ln3_g), row2(ln3_b))

    def experts(h, h_b, route, bm):
        slot_tok, block_e, n_used, slot_orig = _dispatch_plan(route, bm)
        y_pad = _moe_call(block_e, n_used, h_b[slot_tok], w_gate_up, b_gate_up, w_down, b_down, bm)
        return y_pad[slot_orig.reshape(h.shape[0], TOP_K).T]

    h_p, hb_p, route_p = _mix_call(attn_p, u, hist_p, x_prompt.reshape(np_rows, d), mix_w, row_off=0, tm=tm,
                                   nseq=1, grid=(nb, t // tm))
    yg_p = experts(h_p, hb_p, route_p, MOE_ROWS)

    smp = lambda a: a[np_rows:].astype(F32).reshape(db, dq, -1)
    n_pool = cache_k.shape[0]
    kv_t = lambda pool: pool.transpose(0, 2, 3, 1).reshape(n_pool, KV_W, page)
    o_s = _sample_attn_call(page_table, smp(q), smp(qi), smp(kiwi), smp(k_f), smp(v_f),
                            cache_kidx.transpose(0, 2, 1), kv_t(cache_k), kv_t(cache_v))
    o_s = o_s.reshape(db, N_KV_HEADS, KV_GROUP, dq, N_KV_HEADS, HEAD_DIM)
    attn_s = jnp.stack([o_s[:, g, :, :, g, :] for g in range(N_KV_HEADS)], axis=1)
    attn_s = attn_s.transpose(0, 3, 1, 2, 4).reshape(ns_rows, Q_W)
    h_s, hb_s, route_s = _mix_call(attn_s, u, hist_s, x_sample.reshape(ns_rows, d), mix_w, row_off=np_rows, tm=tm,
                                   nseq=tm // dq, grid=(ns_rows // tm, 1))
    yg_s = experts(h_s, hb_s, route_s, MOE_ROWS_SAMPLE)
    y_p = _final_call(h_p, yg_p, route_p, p_prompt.reshape(np_rows, -1), fin_w, tm)
    y_s = _final_call(h_s, yg_s, route_s, p_sample.reshape(ns_rows, -1), fin_w, tm)

    u_p = u[:np_rows].reshape(nb, t, ch)
    u_s = u[np_rows:].reshape(db, dq, ch)
    keep = CONV_WIDTH - 1
    conv_p = jnp.concatenate([jnp.zeros((nb, keep, ch), F32), u_p], axis=1)[:, -keep:]
    conv_s = jnp.concatenate([state_conv, u_s], axis=1)[:, -keep:]
    kv = lambda a, lo, hi, b_, t_: a[lo:hi].reshape(b_, t_, N_KV_HEADS, HEAD_DIM)
    kv_t_out = lambda a: a.reshape(nb, N_KV_HEADS, HEAD_DIM, t).transpose(0, 3, 1, 2)
    return (y_p.reshape(nb, t, d), y_s.reshape(db, dq, d),
            kv_t_out(kt_p), kv_t_out(vt_p), kit_p.transpose(0, 2, 1), conv_p,
            kv(k_f, np_rows, n, db, dq), kv(v_f, np_rows, n, db, dq),
            kiwi[np_rows:, :IDX_DIM].reshape(db, dq, IDX_DIM), conv_s)


def kernel(x_prompt, x_sample, p_prompt, p_sample, cache_k, cache_v, cache_kidx, state_conv, page_table, w_in, b_in, w_o, b_o, ln1_g, ln1_b, conv_w, conv_b, conv_ln_g, conv_ln_b, w_router, b_router, w_gate_up, b_gate_up, w_down, b_down, ln2_g, ln2_b, w_ple_gate, b_ple_gate, w_ple_proj, ln3_g, ln3_b):
    assert w_in.shape[0] == DEPTH
    outs = _layer(x_prompt, x_sample, p_prompt[0], p_sample[0], cache_k[0], cache_v[0], cache_kidx[0],
                  state_conv[0], page_table, w_in[0], b_in[0], w_o[0], b_o[0], ln1_g[0], ln1_b[0],
                  conv_w[0], conv_b[0], conv_ln_g[0], conv_ln_b[0], w_router[0], b_router[0],
                  w_gate_up[0], b_gate_up[0], w_down[0], b_down[0], ln2_g[0], ln2_b[0],
                  w_ple_gate[0], b_ple_gate[0], w_ple_proj[0], ln3_g[0], ln3_b[0])
    y_p, y_s = outs[0], outs[1]
    return (y_p, y_s) + tuple(o[None] for o in outs[2:])
```

```python
import functools
import math

import jax
import jax.numpy as jnp
from jax import lax
from jax.experimental import pallas as pl
from jax.experimental.pallas import tpu as pltpu

F32 = jnp.float32
BF16 = jnp.bfloat16
I32 = jnp.int32

HEAD_DIM = 64
N_HEADS = 8
N_KV_HEADS = 2
KV_GROUP = N_HEADS // N_KV_HEADS
IDX_HEADS = 4
IDX_DIM = 64
INDEX_TOPK = 256
ROPE_THETA = 500000.0
ROPE_HALF = HEAD_DIM // 8
CONV_WIDTH = 31
N_EXPERTS = 32
TOP_K = 4
SWIGLU_LIMIT = 7.0
SWIGLU_ALPHA = 1.702
LN_EPS = 1e-5
DEPTH = 1
DN_ALPHA = (2 * DEPTH) ** 0.25
ATTN_SCALE = HEAD_DIM ** -0.5
IDX_W_SCALE = (IDX_HEADS * IDX_DIM) ** -0.5
LOG2E = math.log2(math.e)

LANES = 128
SUBLANES = 8
VMEM_LIMIT = 56 * 1024 * 1024
NEG_BIG = -1e30
INT_MIN = -(2 ** 31)

Q_W = N_HEADS * HEAD_DIM
KV_W = N_KV_HEADS * HEAD_DIM
QI_W = IDX_HEADS * IDX_DIM
VT_ROWS = HEAD_DIM + 16


def _cparams(sem):
    return pltpu.CompilerParams(dimension_semantics=sem, vmem_limit_bytes=VMEM_LIMIT)


def _rope_slab(z, cos, sin):
    lane = lax.broadcasted_iota(I32, (1, LANES), 1)
    first = (lane % HEAD_DIM) < ROPE_HALF
    up = pltpu.roll(z, LANES - ROPE_HALF, axis=1)
    dn = pltpu.roll(z, ROPE_HALF, axis=1)
    return z * cos + jnp.where(first, up, dn) * sin


def _inproj_kernel(xp_ref, xs_ref, w_ref, b_ref, cos_ref, sin_ref,
                   q_ref, kf_ref, vf_ref, kb_ref, vt_ref, qi_ref, kiwi_ref, kib_ref, u_ref,
                   ktf_ref, vtf_ref, kitf_ref, *, conv_ch, tiles_p):
    x = jnp.where(pl.program_id(0) < tiles_p, xp_ref[...], xs_ref[...]).astype(BF16)
    cos = cos_ref[...]
    sin = sin_ref[...]

    def proj(lo, width):
        return jnp.dot(x, w_ref[:, lo:lo + width], preferred_element_type=F32) + b_ref[:, lo:lo + width]

    def rope(z):
        n = z.shape[1] // LANES
        slabs = [_rope_slab(z[:, s * LANES:(s + 1) * LANES], cos, sin) for s in range(n)]
        return slabs[0] if n == 1 else jnp.concatenate(slabs, axis=1)

    o = 0
    q = rope(proj(o, Q_W))
    q_ref[...] = (q * (ATTN_SCALE * LOG2E)).astype(BF16)
    o += Q_W
    k = rope(proj(o, KV_W))
    kf_ref[...] = k
    kb_ref[...] = k.astype(BF16)
    o += KV_W
    v = proj(o, KV_W)
    vf_ref[...] = v
    vt = v.T
    ones = jnp.ones((VT_ROWS - HEAD_DIM, vt.shape[1]), F32)
    vt_ref[0] = jnp.concatenate(
        [piece for g in range(N_KV_HEADS) for piece in (vt[g * HEAD_DIM:(g + 1) * HEAD_DIM], ones)],
        axis=0).astype(BF16)
    o += KV_W
    qi_ref[...] = rope(proj(o, QI_W)).astype(BF16)
    o += QI_W
    lane = lax.broadcasted_iota(I32, (1, LANES), 1)
    is_ki = lane < IDX_DIM
    kiwi = _rope_slab(proj(o, LANES), jnp.where(is_ki, cos, 1.0), jnp.where(is_ki, sin, 0.0))
    kiwi = kiwi * jnp.where((lane >= IDX_DIM) & (lane < IDX_DIM + IDX_HEADS), IDX_W_SCALE, 1.0)
    kiwi_ref[...] = kiwi
    kib_ref[...] = kiwi[:, :IDX_DIM].astype(BF16)

    @pl.when(pl.program_id(0) < tiles_p)
    def _():
        ktf_ref[0] = k.T
        vtf_ref[0] = vt
        kitf_ref[0] = kiwi.T[:IDX_DIM]
    o += LANES
    a = proj(o, conv_ch)
    g = proj(o + conv_ch, conv_ch)
    u_ref[...] = a * jax.nn.sigmoid(g)


def _inproj_call(x_p, x_s, w_pad, b_pad, cos_tab, sin_tab, tab_index, tm, seq_tiles):
    d = x_p.shape[1]
    tiles_p, tiles_s = x_p.shape[0] // tm, x_s.shape[0] // tm
    nt = tiles_p + tiles_s
    n = nt * tm
    nw = w_pad.shape[1]
    conv_ch = (nw - (Q_W + 2 * KV_W + QI_W + LANES)) // 2
    row = lambda i: (i, 0)
    const = lambda i: (0, 0)
    out_shapes = (
        jax.ShapeDtypeStruct((n, Q_W), BF16),
        jax.ShapeDtypeStruct((n, KV_W), F32),
        jax.ShapeDtypeStruct((n, KV_W), F32),
        jax.ShapeDtypeStruct((n, KV_W), BF16),
        jax.ShapeDtypeStruct((nt, N_KV_HEADS * VT_ROWS, tm), BF16),
        jax.ShapeDtypeStruct((n, QI_W), BF16),
        jax.ShapeDtypeStruct((n, LANES), F32),
        jax.ShapeDtypeStruct((n, IDX_DIM), BF16),
        jax.ShapeDtypeStruct((n, conv_ch), F32),
        jax.ShapeDtypeStruct((tiles_p // seq_tiles, KV_W, seq_tiles * tm), F32),
        jax.ShapeDtypeStruct((tiles_p // seq_tiles, KV_W, seq_tiles * tm), F32),
        jax.ShapeDtypeStruct((tiles_p // seq_tiles, IDX_DIM, seq_tiles * tm), F32),
    )
    last_p = tiles_p - 1
    seq_blk = lambda i: (jnp.minimum(i, last_p) // seq_tiles, 0, jnp.minimum(i, last_p) % seq_tiles)
    out_specs = (
        pl.BlockSpec((tm, Q_W), row),
        pl.BlockSpec((tm, KV_W), row),
        pl.BlockSpec((tm, KV_W), row),
        pl.BlockSpec((tm, KV_W), row),
        pl.BlockSpec((1, N_KV_HEADS * VT_ROWS, tm), lambda i: (i, 0, 0)),
        pl.BlockSpec((tm, QI_W), row),
        pl.BlockSpec((tm, LANES), row),
        pl.BlockSpec((tm, IDX_DIM), row),
        pl.BlockSpec((tm, conv_ch), row),
        pl.BlockSpec((1, KV_W, tm), seq_blk),
        pl.BlockSpec((1, KV_W, tm), seq_blk),
        pl.BlockSpec((1, IDX_DIM, tm), seq_blk),
    )
    return pl.pallas_call(
        functools.partial(_inproj_kernel, conv_ch=conv_ch, tiles_p=tiles_p),
        grid=(nt,),
        in_specs=[
            pl.BlockSpec((tm, d), lambda i: (jnp.minimum(i, tiles_p - 1), 0)),
            pl.BlockSpec((tm, d), lambda i: (jnp.maximum(i - tiles_p, 0), 0)),
            pl.BlockSpec((d, nw), const),
            pl.BlockSpec((1, nw), const),
            pl.BlockSpec((tm, LANES), lambda i: (tab_index(i), 0)),
            pl.BlockSpec((tm, LANES), lambda i: (tab_index(i), 0)),
        ],
        out_specs=out_specs,
        out_shape=out_shapes,
        compiler_params=_cparams(("arbitrary",)),
    )(x_p, x_s, w_pad, b_pad, cos_tab, sin_tab)


def _sortable(score):
    bits = lax.bitcast_convert_type(score, I32)
    key = bits ^ ((bits >> 31) & 0x7FFFFFFF)
    return jnp.where(score == 0.0, 0, key)


def _kth_largest_by_pairs(count_ge, shape, n_sel):
    def pair_body(t, thr):
        step = jnp.left_shift(jnp.int32(1), 30 - 2 * t)
        c1 = thr + step
        c2 = c1 + step
        c3 = c2 + step
        n1, n2, n3 = count_ge(c1), count_ge(c2), count_ge(c3)
        return jnp.where(n3 >= n_sel, c3, jnp.where(n2 >= n_sel, c2, jnp.where(n1 >= n_sel, c1, thr)))

    return lax.fori_loop(0, 16, pair_body, jnp.full(shape, INT_MIN, I32))


def _tie_bound(count_eq_below, need, shape, idx_bits):
    def bit_body(t, j0):
        cand = j0 + jnp.left_shift(jnp.int32(1), idx_bits - 1 - t)
        return jnp.where(count_eq_below(cand) < need, cand, j0)

    return lax.fori_loop(0, idx_bits, bit_body, jnp.zeros(shape, I32))


NO_BOUND = 2 ** 30
PLANE_KEYS = 32 * SUBLANES


def _bit_transpose32(rows):
    a = list(rows[::-1])
    j, m = 16, 0x0000FFFF
    while j:
        k = 0
        while k < 32:
            t = (a[k] ^ (a[k + j] >> j)) & m
            a[k] = a[k] ^ t
            a[k + j] = a[k + j] ^ (t << j)
            k = (k + j + 1) & ~j
        j >>= 1
        m = (m ^ (m << j)) & 0xFFFFFFFF
    return a[::-1]


def _plane_order(n):
    p = jnp.arange(n, dtype=I32)
    within = p % PLANE_KEYS
    return p - within + (within % SUBLANES) * 32 + within // SUBLANES


N_PAIRS = N_HEADS // 2
PAIRS_PER_GROUP = KV_GROUP // 2


def _prompt_attn_kernel(q_ref, qi_ref, kiwi_ref, kib_ref, koff_ref, kb_ref, vt_ref, o_ref,
                        qst_ref, qit_ref, pln_ref, cand_ref, top_ref, sel_ref, bias_ref, j0_ref,
                        lg0_ref, lg1_ref, mx0_ref, mx1_ref, pr0_ref, pr1_ref, al0_ref, al1_ref, m_ref, acc_ref,
                        *, qb, kc, n_sel, idx_bits):
    i = pl.program_id(1)
    n_chunks = (i * qb + qb + kc - 1) // kc
    q_pos = i * qb + lax.broadcasted_iota(I32, (1, qb), 1)

    def key_pos(k0):
        return k0 + lax.broadcasted_iota(I32, (kc, 1), 0)

    qt = q_ref[...].astype(F32).T
    zero = jnp.zeros((HEAD_DIM, 2 * qb), F32)
    for p in range(N_PAIRS):
        g = p // PAIRS_PER_GROUP
        pair = jnp.concatenate([qt[(2 * p) * HEAD_DIM:(2 * p + 1) * HEAD_DIM],
                                qt[(2 * p + 1) * HEAD_DIM:(2 * p + 2) * HEAD_DIM]], axis=1)
        qst_ref[p] = jnp.concatenate([pair if gg == g else zero for gg in range(N_KV_HEADS)], axis=0).astype(BF16)
    qit = qi_ref[...].astype(F32).T
    qit_ref[...] = jnp.concatenate(
        [qit[h * IDX_DIM:(h + 1) * IDX_DIM] for h in range(IDX_HEADS)], axis=1).astype(BF16)
    wit = kiwi_ref[...].T[IDX_DIM:IDX_DIM + SUBLANES]

    planes_per_chunk = kc // PLANE_KEYS

    @pl.when(i == 0)
    def _():
        pln_ref[...] = jnp.zeros(pln_ref.shape, I32)

    def score_body(c, carry):
        k0 = pl.multiple_of(c * kc, kc)
        s = jnp.dot(kib_ref[pl.ds(k0, kc), :], qit_ref[...], preferred_element_type=F32)
        score = jnp.zeros((kc, qb), F32)
        for h in range(IDX_HEADS):
            score = score + wit[h:h + 1, :] * jnp.maximum(s[:, h * qb:(h + 1) * qb], 0.0)
        score = jnp.where(k0 + koff_ref[...] <= q_pos, score, -jnp.inf)
        key = _sortable(score)
        for hb in range(planes_per_chunk):
            base = hb * PLANE_KEYS
            planes = _bit_transpose32([key[base + j * SUBLANES:base + (j + 1) * SUBLANES] for j in range(32)])
            planes[31] = ~planes[31]
            r0 = pl.multiple_of((c * planes_per_chunk + hb) * SUBLANES, SUBLANES)
            for b in range(32):
                pln_ref[b, pl.ds(r0, SUBLANES), :] = planes[b]
        return carry

    lax.fori_loop(0, n_chunks, score_body, 0)

    n_rows = pln_ref.shape[1]
    row = lax.broadcasted_iota(I32, (n_rows, 1), 0)
    cand_ref[...] = jnp.broadcast_to(jnp.where(row < n_chunks * (kc // 32), -1, 0), (n_rows, qb))
    top_ref[...] = jnp.zeros((n_rows, qb), I32)

    def bit_count(words):
        cnt = lax.population_count(words)
        return jnp.sum(jnp.sum(cnt.reshape(n_rows // SUBLANES, SUBLANES, qb), axis=0), axis=0, keepdims=True)

    def bit_body(tb, need):
        plane = pln_ref[31 - tb]
        cand = cand_ref[...]
        ones = cand & plane
        cnt = bit_count(ones)
        take = cnt >= need
        skip = jnp.where(take, 0, -1)
        top_ref[...] = top_ref[...] | (ones & skip)
        cand_ref[...] = cand & (plane ^ skip)
        return jnp.where(take, need, need - cnt)

    need = lax.fori_loop(0, 32, bit_body, jnp.full((1, qb), n_sel, I32))
    tie = bit_count(cand_ref[...]) > need

    def low_mask(bound):
        d = bound - row * 32
        return jnp.where(d >= 32, -1, jnp.left_shift(1, jnp.clip(d, 0, 31)) - 1)

    j0_ref[...] = jnp.full((1, qb), NO_BOUND, I32)

    @pl.when(jnp.max(jnp.where(tie, 1, 0)) > 0)
    def _():
        j0 = _tie_bound(lambda bound: bit_count(cand_ref[...] & low_mask(bound)), need, (1, qb), idx_bits)
        j0_ref[...] = jnp.where(tie, j0 + 1, NO_BOUND)

    sel_ref[...] = top_ref[...] | (cand_ref[...] & low_mask(j0_ref[...]))

    words_per_chunk = kc // 32

    def bias_body(c, carry):
        k0 = pl.multiple_of(c * kc, kc)
        words = sel_ref[pl.ds(pl.multiple_of(c * words_per_chunk, words_per_chunk), words_per_chunk), :]
        sub = lax.broadcasted_iota(I32, (SUBLANES, 1), 0)
        pieces = []
        for v in range(kc // SUBLANES):
            word = jnp.broadcast_to(words[v // 4:v // 4 + 1, :], (SUBLANES, qb))
            pieces.append(lax.shift_right_logical(word, (v % 4) * SUBLANES + sub) & 1)
        chosen = jnp.concatenate(pieces, axis=0) == 1
        bias_ref[pl.ds(k0, kc), :] = jnp.where(chosen & (key_pos(k0) <= q_pos), 0.0, NEG_BIG)
        return carry

    lax.fori_loop(0, n_chunks, bias_body, 0)

    m_ref[...] = jnp.full(m_ref.shape, NEG_BIG, F32)
    acc_ref[...] = jnp.zeros(acc_ref.shape, F32)
    lgs, mxs, prs, als = (lg0_ref, lg1_ref), (mx0_ref, mx1_ref), (pr0_ref, pr1_ref), (al0_ref, al1_ref)

    bias_ref[pl.ds(pl.multiple_of(n_chunks * kc, kc), kc), :] = jnp.full((kc, qb), NEG_BIG, F32)

    def logits(c, slot):
        k0 = pl.multiple_of(jnp.minimum(c, n_chunks - 1) * kc, kc)
        b0 = pl.multiple_of(jnp.minimum(c, n_chunks) * kc, kc)
        for p in range(N_PAIRS):
            lg = jnp.dot(kb_ref[pl.ds(k0, kc), :], qst_ref[p], preferred_element_type=F32)
            halves = [lg[:, h * qb:(h + 1) * qb] + bias_ref[pl.ds(b0, kc), :] for h in range(2)]
            for h in range(2):
                lgs[slot][p, :, h * qb:(h + 1) * qb] = halves[h]
            mxs[slot][p] = jnp.concatenate([jnp.max(x, axis=0, keepdims=True) for x in halves], axis=1)

    def probs(slot):
        for p in range(N_PAIRS):
            m_old = m_ref[p]
            m_new = jnp.maximum(m_old, mxs[slot][p])
            als[slot][p] = jnp.exp2(m_old - m_new)
            prs[slot][p] = jnp.exp2(lgs[slot][p] - m_new).astype(BF16)
            m_ref[p] = m_new

    def weighted(c, slot):
        cc = jnp.clip(c, 0, n_chunks - 1)
        for p in range(N_PAIRS):
            g = p // PAIRS_PER_GROUP
            vt = vt_ref[cc, g * VT_ROWS:(g + 1) * VT_ROWS, :]
            acc_ref[p] = als[slot][p] * acc_ref[p] + jnp.dot(vt, prs[slot][p], preferred_element_type=F32)

    logits(0, 0)
    pr1_ref[...] = jnp.zeros(pr1_ref.shape, BF16)
    al1_ref[...] = jnp.ones(al1_ref.shape, F32)

    def attn_body(j, carry):
        c = 2 * j
        logits(c + 1, 1)
        probs(0)
        weighted(c - 1, 1)
        logits(c + 2, 0)
        probs(1)
        weighted(c, 0)
        return carry

    lax.fori_loop(0, (n_chunks + 1) // 2, attn_body, 0)
    weighted(n_chunks - 1, 1)

    heads = []
    for p in range(N_PAIRS):
        acc = acc_ref[p]
        o = acc[:HEAD_DIM] * (1.0 / acc[HEAD_DIM:HEAD_DIM + 1])
        heads += [o[:, :qb], o[:, qb:]]
    o_ref[...] = jnp.concatenate(heads, axis=0).T.astype(o_ref.dtype)


def _prompt_attn_call(q, qi, kiwi, kib, kb, vt, nb, t, qb, kc):
    assert kc % PLANE_KEYS == 0 and t % kc == 0
    nqb = t // qb
    n_sel = min(INDEX_TOPK, t // 4)
    blk = lambda b, i: (b * nqb + i, 0)
    seq = lambda b, i: (b, 0)
    koff = jnp.broadcast_to(_plane_order(kc)[:, None], (kc, qb))
    kern = functools.partial(_prompt_attn_kernel, qb=qb, kc=kc, n_sel=n_sel, idx_bits=max(1, (t - 1).bit_length()))
    return pl.pallas_call(
        kern,
        grid=(nb, nqb),
        in_specs=[
            pl.BlockSpec((qb, Q_W), blk),
            pl.BlockSpec((qb, QI_W), blk),
            pl.BlockSpec((qb, LANES), blk),
            pl.BlockSpec((t, IDX_DIM), seq),
            pl.BlockSpec((kc, qb), lambda b, i: (0, 0)),
            pl.BlockSpec((t, KV_W), seq),
            pl.BlockSpec((t // kc, N_KV_HEADS * VT_ROWS, kc), lambda b, i: (b, 0, 0)),
        ],
        out_specs=pl.BlockSpec((qb, Q_W), blk),
        out_shape=jax.ShapeDtypeStruct((nb * t, Q_W), BF16),
        scratch_shapes=[
            pltpu.VMEM((N_PAIRS, KV_W, 2 * qb), BF16),
            pltpu.VMEM((IDX_DIM, IDX_HEADS * qb), BF16),
            pltpu.VMEM((32, t // 32, qb), I32),
            pltpu.VMEM((t // 32, qb), I32),
            pltpu.VMEM((t // 32, qb), I32),
            pltpu.VMEM((t // 32, qb), I32),
            pltpu.VMEM((t + kc, qb), F32),
            pltpu.VMEM((1, qb), I32),
            pltpu.VMEM((N_PAIRS, kc, 2 * qb), F32),
            pltpu.VMEM((N_PAIRS, kc, 2 * qb), F32),
            pltpu.VMEM((N_PAIRS, 1, 2 * qb), F32),
            pltpu.VMEM((N_PAIRS, 1, 2 * qb), F32),
            pltpu.VMEM((N_PAIRS, kc, 2 * qb), BF16),
            pltpu.VMEM((N_PAIRS, kc, 2 * qb), BF16),
            pltpu.VMEM((N_PAIRS, 1, 2 * qb), F32),
            pltpu.VMEM((N_PAIRS, 1, 2 * qb), F32),
            pltpu.VMEM((N_PAIRS, 1, 2 * qb), F32),
            pltpu.VMEM((N_PAIRS, VT_ROWS, 2 * qb), F32),
        ],
        compiler_params=_cparams(("parallel", "arbitrary")),
    )(q, qi, kiwi, kib, koff, kb, vt)


PAGES_PER_DOT = 4


def _sample_attn_kernel(pt_ref, q_ref, qi_ref, kiwi_ref, kn_ref, vn_ref, cki_hbm, ck_hbm, cv_hbm, o_ref,
                        kit_buf, kt_buf, vt_buf, sem, kin_s, knew_s, vnew_s, keys_ref, bias_ref, lg_ref, j0_ref,
                        *, n_pages, dq, n_sel, idx_bits):
    b = pl.program_id(0)
    nb = pl.num_programs(0)
    slot = b % 2
    past = n_pages * LANES
    lp = past + LANES
    streams = ((cki_hbm, kit_buf), (ck_hbm, kt_buf), (cv_hbm, vt_buf))

    def page_copy(a, seq, pg, sl):
        src, dst = streams[a]
        return pltpu.make_async_copy(src.at[pt_ref[seq, pg]], dst.at[sl, pg], sem.at[sl, a])

    def fetch(seq, sl):
        def body(pg, carry):
            for a in range(len(streams)):
                page_copy(a, seq, pg, sl).start()
            return carry

        lax.fori_loop(0, n_pages, body, 0)

    @pl.when(b == 0)
    def _():
        fetch(0, 0)

    @pl.when(b + 1 < nb)
    def _():
        fetch(b + 1, 1 - slot)

    def wait_body(pg, carry):
        for a in range(len(streams)):
            page_copy(a, b, pg, slot).wait()
        return carry

    lax.fori_loop(0, n_pages, wait_body, 0)

    def pages(buf, c):
        return jnp.concatenate([buf[slot, c * PAGES_PER_DOT + r] for r in range(PAGES_PER_DOT)], axis=1).astype(BF16)

    kc = PAGES_PER_DOT * LANES
    q = q_ref[0]
    qi = qi_ref[0]
    kiwi = kiwi_ref[0]
    qrow = lax.broadcasted_iota(I32, (dq, 1), 0)
    idx = lax.broadcasted_iota(I32, (1, lp), 1)
    rel = idx - past
    adm = (idx < past) | ((rel <= qrow) & (rel < dq))

    kin_s[...] = jnp.zeros(kin_s.shape, F32)
    knew_s[...] = jnp.zeros(knew_s.shape, F32)
    vnew_s[...] = jnp.zeros(vnew_s.shape, F32)
    kin_s[0:dq, :] = kiwi[:, :IDX_DIM]
    knew_s[0:dq, :] = kn_ref[0]
    vnew_s[0:dq, :] = vn_ref[0]

    nt = (((1,), (1,)), ((), ()))
    qi_rows = jnp.concatenate([qi[:, h * IDX_DIM:(h + 1) * IDX_DIM] for h in range(IDX_HEADS)], axis=0).astype(BF16)
    wi_col = jnp.concatenate([kiwi[:, IDX_DIM + h:IDX_DIM + h + 1] for h in range(IDX_HEADS)], axis=0)

    def scores(s):
        r = jnp.maximum(s, 0.0) * wi_col
        out = r[0:dq]
        for h in range(1, IDX_HEADS):
            out = out + r[h * dq:(h + 1) * dq]
        return out

    for c in range(n_pages // PAGES_PER_DOT):
        s = jnp.dot(qi_rows, pages(kit_buf, c), preferred_element_type=F32)
        keys_ref[:, c * kc:(c + 1) * kc] = _sortable(scores(s))
    s_new = scores(lax.dot_general(qi_rows, kin_s[...].astype(BF16), nt, preferred_element_type=F32))
    keys_ref[:, past:] = _sortable(jnp.where(adm[:, past:], s_new, -jnp.inf))

    def count(pred):
        return jnp.sum(jnp.where(pred(keys_ref[...]), 1, 0).astype(I32), axis=1, keepdims=True)

    thr = _kth_largest_by_pairs(lambda cand: count(lambda k: k >= cand), (dq, 1), n_sel)
    need = n_sel - count(lambda k: k > thr)
    tie = count(lambda k: k >= thr) > n_sel
    j0_ref[...] = jnp.full((dq, 1), NO_BOUND, I32)

    @pl.when(jnp.max(jnp.where(tie, 1, 0)) > 0)
    def _():
        j0 = _tie_bound(lambda cand: count(lambda k: (k == thr) & (idx < cand)), need, (dq, 1), idx_bits)
        j0_ref[...] = jnp.where(tie, j0, NO_BOUND)

    j0 = j0_ref[...]
    keys = keys_ref[...]
    sel = ((keys > thr) | ((keys == thr) & (idx <= j0))) & adm
    bias_ref[...] = jnp.where(sel, 0.0, NEG_BIG)

    zero = jnp.zeros((dq, HEAD_DIM), F32)
    rows = []
    for g in range(N_KV_HEADS):
        for r in range(KV_GROUP):
            h = g * KV_GROUP + r
            piece = q[:, h * HEAD_DIM:(h + 1) * HEAD_DIM]
            rows.append(jnp.concatenate([piece, zero] if g == 0 else [zero, piece], axis=1))
    qbd = jnp.concatenate(rows, axis=0).astype(BF16)

    for c in range(n_pages // PAGES_PER_DOT):
        lg_ref[:, c * kc:(c + 1) * kc] = jnp.dot(qbd, pages(kt_buf, c), preferred_element_type=F32)
    lg_ref[:, past:] = lax.dot_general(qbd, knew_s[...].astype(BF16), nt, preferred_element_type=F32)

    bias = bias_ref[...]
    lg = lg_ref[...] + jnp.concatenate([bias] * N_HEADS, axis=0)
    m = jnp.max(lg, axis=1, keepdims=True)
    p = jnp.exp2(lg - m)
    l = jnp.sum(p, axis=1, keepdims=True)
    lg_ref[...] = p
    o = jnp.dot(lg_ref[:, past:].astype(BF16), vnew_s[...].astype(BF16), preferred_element_type=F32)
    for c in range(n_pages // PAGES_PER_DOT):
        o = o + lax.dot_general(lg_ref[:, c * kc:(c + 1) * kc].astype(BF16), pages(vt_buf, c), nt,
                                preferred_element_type=F32)
    o_ref[0] = o * (1.0 / l)


def _sample_attn_call(page_table, q, qi, kiwi, kn, vn, cki_t, ck_t, cv_t):
    nb, dq, _ = q.shape
    n_pages = page_table.shape[1]
    assert cki_t.shape[2] == LANES and n_pages % PAGES_PER_DOT == 0
    lp = (n_pages + 1) * LANES
    n_sel = min(INDEX_TOPK, (n_pages * LANES + dq) // 4)
    b3 = lambda b, pt: (b, 0, 0)
    kern = functools.partial(_sample_attn_kernel, n_pages=n_pages, dq=dq, n_sel=n_sel,
                             idx_bits=max(1, (lp - 1).bit_length()))
    grid_spec = pltpu.PrefetchScalarGridSpec(
        num_scalar_prefetch=1,
        grid=(nb,),
        in_specs=[
            pl.BlockSpec((1, dq, Q_W), b3),
            pl.BlockSpec((1, dq, QI_W), b3),
            pl.BlockSpec((1, dq, LANES), b3),
            pl.BlockSpec((1, dq, KV_W), b3),
            pl.BlockSpec((1, dq, KV_W), b3),
            pl.BlockSpec(memory_space=pl.ANY),
            pl.BlockSpec(memory_space=pl.ANY),
            pl.BlockSpec(memory_space=pl.ANY),
        ],
        out_specs=pl.BlockSpec((1, N_HEADS * dq, KV_W), b3),
        scratch_shapes=[
            pltpu.VMEM((2, n_pages, IDX_DIM, LANES), F32),
            pltpu.VMEM((2, n_pages, KV_W, LANES), F32),
            pltpu.VMEM((2, n_pages, KV_W, LANES), F32),
            pltpu.SemaphoreType.DMA((2, 3)),
            pltpu.VMEM((LANES, IDX_DIM), F32),
            pltpu.VMEM((LANES, KV_W), F32),
            pltpu.VMEM((LANES, KV_W), F32),
            pltpu.VMEM((dq, lp), I32),
            pltpu.VMEM((dq, lp), F32),
            pltpu.VMEM((N_HEADS * dq, lp), F32),
            pltpu.VMEM((dq, 1), I32),
        ],
    )
    return pl.pallas_call(
        kern,
        grid_spec=grid_spec,
        out_shape=jax.ShapeDtypeStruct((nb, N_HEADS * dq, KV_W), F32),
        compiler_params=_cparams(("arbitrary",)),
    )(page_table, q, qi, kiwi, kn, vn, cki_t, ck_t, cv_t)


def _layer_norm(x, g, b):
    mu = jnp.mean(x, axis=-1, keepdims=True)
    xc = x - mu
    var = jnp.mean(xc * xc, axis=-1, keepdims=True)
    return xc * lax.rsqrt(var + LN_EPS) * g + b


HIST_ROWS = 32
HIST_PAD = HIST_ROWS - (CONV_WIDTH - 1)
CONV_ROWS = 64


def _mix_kernel(attn_ref, u_ref, uprev_ref, hist_ref, x_ref, cw_ref, cb_ref, cg_ref, cbb_ref, wo_ref, bo_ref,
                g1_ref, b1_ref, wr_ref, br_ref, h_ref, hb_ref, route_ref, xp_ref, xs_ref, conv_ref,
                *, tm, nseq, use_prev):
    rows = tm // nseq
    ch = u_ref.shape[1]

    if use_prev:
        first = pl.program_id(1) == 0

        @pl.when(first)
        def _():
            xp_ref[:, 0:HIST_ROWS, :] = hist_ref[...]

        @pl.when(jnp.logical_not(first))
        def _():
            xp_ref[0, 0:HIST_ROWS, :] = uprev_ref[...]
    else:
        xp_ref[:, 0:HIST_ROWS, :] = hist_ref[...]
    xp_ref[:, HIST_ROWS:, :] = u_ref[...].reshape(nseq, rows, ch)

    rs = min(rows, CONV_ROWS)
    span = HIST_ROWS + rows - SUBLANES

    def seq_body(s, carry):
        for r in range(1, SUBLANES):
            xs_ref[r - 1] = xp_ref[s, r:r + span, :]
        for r0 in range(0, rows, rs):
            acc = jnp.zeros((rs, ch), F32)
            for j in range(CONV_WIDTH):
                r = (HIST_PAD + j) % SUBLANES
                lo = r0 + HIST_PAD + j - r
                win = xp_ref[s, lo:lo + rs, :] if r == 0 else xs_ref[r - 1, lo:lo + rs, :]
                acc = acc + cw_ref[j:j + 1, :] * win
            conv_ref[pl.ds(pl.multiple_of(s * rows + r0, SUBLANES), rs), :] = acc
        return carry

    if nseq == 1:
        seq_body(0, 0)
    else:
        lax.fori_loop(0, nseq, seq_body, 0)

    y = _layer_norm(conv_ref[...] + cb_ref[...], cg_ref[...], cbb_ref[...])
    conv = (y * jax.nn.sigmoid(y)).astype(BF16)
    aw = attn_ref.shape[1]
    mixed = (jnp.dot(attn_ref[...].astype(BF16), wo_ref[0:aw, :], preferred_element_type=F32)
             + jnp.dot(conv, wo_ref[aw:, :], preferred_element_type=F32) + bo_ref[...])
    h = _layer_norm(DN_ALPHA * x_ref[...] + mixed, g1_ref[...], b1_ref[...])
    h_ref[...] = h
    hb = h.astype(BF16)
    hb_ref[...] = hb

    logits = jnp.dot(hb, wr_ref[...], preferred_element_type=F32) + br_ref[...]
    lane = lax.broadcasted_iota(I32, (1, LANES), 1)
    work = logits
    vals, ids = [], []
    for _ in range(TOP_K):
        mx = jnp.max(work, axis=1, keepdims=True)
        ix = jnp.min(jnp.where(work == mx, lane, LANES), axis=1, keepdims=True)
        vals.append(mx)
        ids.append(ix)
        work = jnp.where(lane == ix, -jnp.inf, work)
    ex = [jnp.exp(v - vals[0]) for v in vals]
    den = ex[0]
    for e in ex[1:]:
        den = den + e
    route = jnp.zeros((tm, LANES), F32)
    for k in range(TOP_K):
        route = jnp.where(lane == k, ex[k] / den, route)
        route = jnp.where(lane == TOP_K + k, ids[k].astype(F32), route)
    route_ref[...] = route


def _mix_call(attn, u, hist, x, weights, *, row_off, tm, nseq, grid):
    cw, cb, cg, cbb, wo, bo, g1, b1, wr, br = weights
    n_rows, d = x.shape
    ch = u.shape[1]
    use_prev = grid[1] > 1
    tiles_per_seq = grid[1]
    off_t = row_off // tm
    tile = lambda b, i: (b * tiles_per_seq + i, 0)
    otile = lambda b, i: (off_t + b * tiles_per_seq + i, 0)
    const = lambda b, i: (0, 0)
    if use_prev:
        per_tile = tm // HIST_ROWS
        prev = lambda b, i: (jnp.maximum((off_t + b * tiles_per_seq + i) * per_tile - 1, 0), 0)
    else:
        prev = const
    in_specs = [
        pl.BlockSpec((tm, attn.shape[1]), tile),
        pl.BlockSpec((tm, ch), otile),
        pl.BlockSpec((HIST_ROWS, ch), prev),
        pl.BlockSpec((nseq, HIST_ROWS, ch), lambda b, i: (b, 0, 0)),
        pl.BlockSpec((tm, d), tile),
        pl.BlockSpec(cw.shape, const),
        pl.BlockSpec(cb.shape, const),
        pl.BlockSpec(cg.shape, const),
        pl.BlockSpec(cbb.shape, const),
        pl.BlockSpec(wo.shape, const),
        pl.BlockSpec(bo.shape, const),
        pl.BlockSpec(g1.shape, const),
        pl.BlockSpec(b1.shape, const),
        pl.BlockSpec(wr.shape, const),
        pl.BlockSpec(br.shape, const),
    ]
    kern = functools.partial(_mix_kernel, tm=tm, nseq=nseq, use_prev=use_prev)
    return pl.pallas_call(
        kern,
        grid=grid,
        in_specs=in_specs,
        out_specs=(
            pl.BlockSpec((tm, d), tile),
            pl.BlockSpec((tm, d), tile),
            pl.BlockSpec((tm, LANES), tile),
        ),
        out_shape=(
            jax.ShapeDtypeStruct((n_rows, d), F32),
            jax.ShapeDtypeStruct((n_rows, d), BF16),
            jax.ShapeDtypeStruct((n_rows, LANES), F32),
        ),
        scratch_shapes=[
            pltpu.VMEM((nseq, HIST_ROWS + tm // nseq, ch), F32),
            pltpu.VMEM((SUBLANES - 1, HIST_ROWS + tm // nseq - SUBLANES, ch), F32),
            pltpu.VMEM((tm, ch), F32),
        ],
        compiler_params=_cparams(("parallel", "arbitrary")),
    )(attn, u, u, hist, x, cw, cb, cg, cbb, wo, bo, g1, b1, wr, br)


CAST_ROWS = 128


def _moe_kernel(be_ref, nu_ref, x_ref, wgu_ref, bgu_ref, wd_ref, bd_ref, y_ref, wgu_s, wd_s):
    i = pl.program_id(0)
    used = i < nu_ref[0]
    changed = (i == 0) | (be_ref[i] != be_ref[jnp.maximum(i - 1, 0)])
    dff = wd_s.shape[0]

    @pl.when(used & changed)
    def _():
        def cast_gu(r, carry):
            r0 = pl.multiple_of(r * CAST_ROWS, CAST_ROWS)
            wgu_s[pl.ds(r0, CAST_ROWS), :] = wgu_ref[0, pl.ds(r0, CAST_ROWS), :].astype(BF16)
            return carry

        def cast_d(r, carry):
            r0 = pl.multiple_of(r * CAST_ROWS, CAST_ROWS)
            wd_s[pl.ds(r0, CAST_ROWS), :] = wd_ref[0, pl.ds(r0, CAST_ROWS), :].astype(BF16)
            return carry

        lax.fori_loop(0, wgu_s.shape[0] // CAST_ROWS, cast_gu, 0)
        lax.fori_loop(0, wd_s.shape[0] // CAST_ROWS, cast_d, 0)

    @pl.when(used)
    def _():
        gu = jnp.dot(x_ref[...], wgu_s[...], preferred_element_type=F32) + bgu_ref[0]
        gate = jnp.minimum(gu[:, :dff], SWIGLU_LIMIT)
        up = jnp.clip(gu[:, dff:], -SWIGLU_LIMIT, SWIGLU_LIMIT)
        act = (up + 1.0) * gate * jax.nn.sigmoid(SWIGLU_ALPHA * gate)
        y = jnp.dot(act.astype(BF16), wd_s[...], preferred_element_type=F32) + bd_ref[0]
        y_ref[...] = y.astype(y_ref.dtype)

    @pl.when(jnp.logical_not(used))
    def _():
        y_ref[...] = jnp.zeros(y_ref.shape, y_ref.dtype)


def _moe_call(block_e, n_used, x_pad, wgu, bgu, wd, bd, bm):
    p, d = x_pad.shape
    ne, _, dgu = wgu.shape
    dff = wd.shape[1]
    n_blocks = p // bm
    grid_spec = pltpu.PrefetchScalarGridSpec(
        num_scalar_prefetch=2,
        grid=(n_blocks,),
        in_specs=[
            pl.BlockSpec((bm, d), lambda i, be, nu: (i, 0)),
            pl.BlockSpec((1, d, dgu), lambda i, be, nu: (be[i], 0, 0)),
            pl.BlockSpec((1, 1, dgu), lambda i, be, nu: (be[i], 0, 0)),
            pl.BlockSpec((1, dff, d), lambda i, be, nu: (be[i], 0, 0)),
            pl.BlockSpec((1, 1, d), lambda i, be, nu: (be[i], 0, 0)),
        ],
        out_specs=pl.BlockSpec((bm, d), lambda i, be, nu: (i, 0)),
        scratch_shapes=[pltpu.VMEM((d, dgu), BF16), pltpu.VMEM((dff, d), BF16)],
    )
    return pl.pallas_call(
        _moe_kernel,
        grid_spec=grid_spec,
        out_shape=jax.ShapeDtypeStruct((p, d), BF16),
        compiler_params=_cparams(("arbitrary",)),
    )(block_e, n_used, x_pad, wgu, bgu.reshape(ne, 1, dgu), wd, bd.reshape(ne, 1, d))


def _dispatch_plan(route, bm):
    n = route.shape[0]
    nk = n * TOP_K
    experts = jnp.arange(N_EXPERTS, dtype=I32)
    e_flat = route[:, TOP_K:2 * TOP_K].astype(I32).reshape(-1)
    iota = jnp.arange(nk, dtype=I32)
    e_sorted, order = lax.sort((e_flat, iota), num_keys=1, is_stable=True)
    _, inverse = lax.sort((order, iota), num_keys=1)
    counts = jnp.sum((e_flat[:, None] == experts[None, :]).astype(I32), axis=0)
    padded = (counts + bm - 1) // bm * bm
    start = jnp.cumsum(counts) - counts
    pend = jnp.cumsum(padded)
    pstart = pend - padded
    slot_sorted = pstart[e_sorted] + iota - start[e_sorted]
    slot_orig = slot_sorted[inverse]
    n_blocks = -(-(nk + N_EXPERTS * (bm - 1)) // bm)
    block_start = jnp.arange(n_blocks, dtype=I32) * bm
    block_e = jnp.minimum(jnp.sum((pend[None, :] <= block_start[:, None]).astype(I32), axis=1), N_EXPERTS - 1)
    n_used = (pend[-1:] // bm).astype(I32)
    e_slot = jnp.repeat(block_e, bm)
    rank = jnp.arange(n_blocks * bm, dtype=I32) - pstart[e_slot]
    src = jnp.clip(start[e_slot] + rank, 0, nk - 1)
    slot_tok = jnp.where(rank < counts[e_slot], order[src] // TOP_K, 0)
    return slot_tok, block_e, n_used, slot_orig


def _final_kernel(h_ref, yg_ref, route_ref, pe_ref, g2_ref, b2_ref, wg_ref, bg_ref, wp_ref, g3_ref, b3_ref, o_ref):
    route = route_ref[...]
    y = route[:, 0:1] * yg_ref[0].astype(F32)
    for k in range(1, TOP_K):
        y = y + route[:, k:k + 1] * yg_ref[k].astype(F32)
    h2 = _layer_norm(DN_ALPHA * h_ref[...] + y, g2_ref[...], b2_ref[...])
    gate = jax.nn.sigmoid(jnp.dot(h2.astype(BF16), wg_ref[...], preferred_element_type=F32) + bg_ref[...])
    proj = jnp.dot(pe_ref[...].astype(BF16), wp_ref[...], preferred_element_type=F32)
    o_ref[...] = _layer_norm(DN_ALPHA * h2 + gate * proj, g3_ref[...], b3_ref[...])


def _final_call(h, yg, route, pe, weights, tm):
    g2, b2, wg, bg, wp, g3, b3 = weights
    n, pd = pe.shape
    d = h.shape[1]
    row = lambda i: (i, 0)
    const = lambda i: (0, 0)
    return pl.pallas_call(
        _final_kernel,
        grid=(n // tm,),
        in_specs=[
            pl.BlockSpec((tm, d), row),
            pl.BlockSpec((TOP_K, tm, d), lambda i: (0, i, 0)),
            pl.BlockSpec((tm, LANES), row),
            pl.BlockSpec((tm, pd), row),
            pl.BlockSpec(g2.shape, const),
            pl.BlockSpec(b2.shape, const),
            pl.BlockSpec(wg.shape, const),
            pl.BlockSpec(bg.shape, const),
            pl.BlockSpec(wp.shape, const),
            pl.BlockSpec(g3.shape, const),
            pl.BlockSpec(b3.shape, const),
        ],
        out_specs=pl.BlockSpec((tm, d), row),
        out_shape=jax.ShapeDtypeStruct((n, d), F32),
        compiler_params=_cparams(("parallel",)),
    )(h, yg, route, pe, g2, b2, wg, bg, wp, g3, b3)


TOKEN_TILE = 512
QUERY_BLOCK = 128
MOE_ROWS = 512
MOE_ROWS_SAMPLE = 128


def _rope_tables(pos):
    rd = 2 * ROPE_HALF
    inv_freq = ROPE_THETA ** (-jnp.arange(ROPE_HALF, dtype=F32) * 2.0 / rd)
    ang = pos.astype(F32)[:, None] * inv_freq[None, :]
    cos, sin = jnp.cos(ang), jnp.sin(ang)
    ones = jnp.ones((pos.shape[0], HEAD_DIM - rd), F32)
    cos_h = jnp.concatenate([cos, cos, ones], axis=1)
    sin_h = jnp.concatenate([-sin, sin, 0.0 * ones], axis=1)
    reps = LANES // HEAD_DIM
    return jnp.tile(cos_h, (1, reps)), jnp.tile(sin_h, (1, reps))


def _layer(x_prompt, x_sample, p_prompt, p_sample, cache_k, cache_v, cache_kidx, state_conv, page_table,
           w_in, b_in, w_o, b_o, ln1_g, ln1_b, conv_w, conv_b, conv_ln_g, conv_ln_b,
           w_router, b_router, w_gate_up, b_gate_up, w_down, b_down, ln2_g, ln2_b,
           w_ple_gate, b_ple_gate, w_ple_proj, ln3_g, ln3_b):
    nb, t, d = x_prompt.shape
    db, dq, _ = x_sample.shape
    n_pages, page = page_table.shape[1], cache_k.shape[1]
    past = n_pages * page
    ch = conv_w.shape[1]
    tm = min(TOKEN_TILE, t)
    np_rows, ns_rows = nb * t, db * dq
    n = np_rows + ns_rows
    assert t % tm == 0 and ns_rows % tm == 0 and tm % dq == 0 and dq == SUBLANES
    row2 = lambda a: a.reshape(1, -1)

    n_head = Q_W + 2 * KV_W + QI_W + IDX_DIM + IDX_HEADS
    pad = LANES - IDX_DIM - IDX_HEADS
    w_pad = jnp.concatenate([w_in[:, :n_head], jnp.zeros((d, pad), F32), w_in[:, n_head:]], axis=1).astype(BF16)
    b_pad = jnp.concatenate([b_in[:n_head], jnp.zeros((pad,), F32), b_in[n_head:]]).reshape(1, -1)
    pos = jnp.concatenate([jnp.arange(t, dtype=I32), past + (jnp.arange(tm, dtype=I32) % dq)])
    cos_tab, sin_tab = _rope_tables(pos)
    tiles_p = np_rows // tm
    tab_index = lambda i: jnp.where(i < tiles_p, i % (t // tm), t // tm)
    q, k_f, v_f, k_b, v_t, qi, kiwi, ki_b, u, kt_p, vt_p, kit_p = _inproj_call(
        x_prompt.reshape(np_rows, d), x_sample.reshape(ns_rows, d), w_pad, b_pad, cos_tab, sin_tab, tab_index, tm,
        t // tm)

    ki_planes = ki_b[:np_rows].reshape(np_rows // PLANE_KEYS, SUBLANES, 32, IDX_DIM)
    ki_planes = ki_planes.transpose(0, 2, 1, 3).reshape(np_rows, IDX_DIM)
    attn_p = _prompt_attn_call(q, qi, kiwi, ki_planes, k_b, v_t, nb, t, min(QUERY_BLOCK, t), tm)

    wr_pad = jnp.concatenate([w_router, jnp.zeros((d, LANES - N_EXPERTS), F32)], axis=1).astype(BF16)
    br_pad = jnp.concatenate([b_router, jnp.full((LANES - N_EXPERTS,), -jnp.inf, F32)]).reshape(1, -1)
    mix_w = (conv_w, row2(conv_b), row2(conv_ln_g), row2(conv_ln_b), w_o.astype(BF16), row2(b_o),
             row2(ln1_g), row2(ln1_b), wr_pad, br_pad)
    hist_p = jnp.zeros((nb, HIST_ROWS, ch), F32)
    hist_s = jnp.concatenate([jnp.zeros((db, HIST_PAD, ch), F32), state_conv], axis=1)
    fin_w = (row2(ln2_g), row2(ln2_b), w_ple_gate.astype(BF16), row2(b_ple_gate), w_ple_proj.astype(BF16),
             row2(ln3_g), row2(ln3_b))

    def experts(h, h_b, route, bm):
        slot_tok, block_e, n_used, slot_orig = _dispatch_plan(route, bm)
        y_pad = _moe_call(block_e, n_used, h_b[slot_tok], w_gate_up, b_gate_up, w_down, b_down, bm)
        return y_pad[slot_orig.reshape(h.shape[0], TOP_K).T]

    h_p, hb_p, route_p = _mix_call(attn_p, u, hist_p, x_prompt.reshape(np_rows, d), mix_w, row_off=0, tm=tm,
                                   nseq=1, grid=(nb, t // tm))
    yg_p = experts(h_p, hb_p, route_p, MOE_ROWS)

    smp = lambda a: a[np_rows:].astype(F32).reshape(db, dq, -1)
    n_pool = cache_k.shape[0]
    kv_t = lambda pool: pool.transpose(0, 2, 3, 1).reshape(n_pool, KV_W, page)
    o_s = _sample_attn_call(page_table, smp(q), smp(qi), smp(kiwi), smp(k_f), smp(v_f),
                            cache_kidx.transpose(0, 2, 1), kv_t(cache_k), kv_t(cache_v))
    o_s = o_s.reshape(db, N_KV_HEADS, KV_GROUP, dq, N_KV_HEADS, HEAD_DIM)
    attn_s = jnp.stack([o_s[:, g, :, :, g, :] for g in range(N_KV_HEADS)], axis=1)
    attn_s = attn_s.transpose(0, 3, 1, 2, 4).reshape(ns_rows, Q_W)
    h_s, hb_s, route_s = _mix_call(attn_s, u, hist_s, x_sample.reshape(ns_rows, d), mix_w, row_off=np_rows, tm=tm,
                                   nseq=tm // dq, grid=(ns_rows // tm, 1))
    yg_s = experts(h_s, hb_s, route_s, MOE_ROWS_SAMPLE)
    y_p = _final_call(h_p, yg_p, route_p, p_prompt.reshape(np_rows, -1), fin_w, tm)
    y_s = _final_call(h_s, yg_s, route_s, p_sample.reshape(ns_rows, -1), fin_w, tm)

    u_p = u[:np_rows].reshape(nb, t, ch)
    u_s = u[np_rows:].reshape(db, dq, ch)
    keep = CONV_WIDTH - 1
    conv_p = jnp.concatenate([jnp.zeros((nb, keep, ch), F32), u_p], axis=1)[:, -keep:]
    conv_s = jnp.concatenate([state_conv, u_s], axis=1)[:, -keep:]
    kv = lambda a, lo, hi, b_, t_: a[lo:hi].reshape(b_, t_, N_KV_HEADS, HEAD_DIM)
    kv_t_out = lambda a: a.reshape(nb, N_KV_HEADS, HEAD_DIM, t).transpose(0, 3, 1, 2)
    return (y_p.reshape(nb, t, d), y_s.reshape(db, dq, d),
            kv_t_out(kt_p), kv_t_out(vt_p), kit_p.transpose(0, 2, 1), conv_p,
            kv(k_f, np_rows, n, db, dq), kv(v_f, np_rows, n, db, dq),
            kiwi[np_rows:, :IDX_DIM].reshape(db, dq, IDX_DIM), conv_s)


def kernel(x_prompt, x_sample, p_prompt, p_sample, cache_k, cache_v, cache_kidx, state_conv, page_table, w_in, b_in, w_o, b_o, ln1_g, ln1_b, conv_w, conv_b, conv_ln_g, conv_ln_b, w_router, b_router, w_gate_up, b_gate_up, w_down, b_down, ln2_g, ln2_b, w_ple_gate, b_ple_gate, w_ple_proj, ln3_g, ln3_b):
    assert w_in.shape[0] == DEPTH
    outs = _layer(x_prompt, x_sample, p_prompt[0], p_sample[0], cache_k[0], cache_v[0], cache_kidx[0],
                  state_conv[0], page_table, w_in[0], b_in[0], w_o[0], b_o[0], ln1_g[0], ln1_b[0],
                  conv_w[0], conv_b[0], conv_ln_g[0], conv_ln_b[0], w_router[0], b_router[0],
                  w_gate_up[0], b_gate_up[0], w_down[0], b_down[0], ln2_g[0], ln2_b[0],
                  w_ple_gate[0], b_ple_gate[0], w_ple_proj[0], ln3_g[0], ln3_b[0])
    y_p, y_s = outs[0], outs[1]
    return (y_p, y_s) + tuple(o[None] for o in outs[2:])
```

```python
import functools
import math

import jax
import jax.numpy as jnp
from jax import lax
from jax.experimental import pallas as pl
from jax.experimental.pallas import tpu as pltpu

F32 = jnp.float32
BF16 = jnp.bfloat16
I32 = jnp.int32

HEAD_DIM = 64
N_HEADS = 8
N_KV_HEADS = 2
KV_GROUP = N_HEADS // N_KV_HEADS
IDX_HEADS = 4
IDX_DIM = 64
INDEX_TOPK = 256
ROPE_THETA = 500000.0
ROPE_HALF = HEAD_DIM // 8
CONV_WIDTH = 31
N_EXPERTS = 32
TOP_K = 4
SWIGLU_LIMIT = 7.0
SWIGLU_ALPHA = 1.702
LN_EPS = 1e-5
DEPTH = 1
DN_ALPHA = (2 * DEPTH) ** 0.25
ATTN_SCALE = HEAD_DIM ** -0.5
IDX_W_SCALE = (IDX_HEADS * IDX_DIM) ** -0.5
LOG2E = math.log2(math.e)

LANES = 128
SUBLANES = 8
VMEM_LIMIT = 56 * 1024 * 1024
NEG_BIG = -1e30
INT_MIN = -(2 ** 31)

Q_W = N_HEADS * HEAD_DIM
KV_W = N_KV_HEADS * HEAD_DIM
QI_W = IDX_HEADS * IDX_DIM
VT_ROWS = HEAD_DIM + 16


def _cparams(sem):
    return pltpu.CompilerParams(dimension_semantics=sem, vmem_limit_bytes=VMEM_LIMIT)


def _rope_slab(z, cos, sin):
    lane = lax.broadcasted_iota(I32, (1, LANES), 1)
    first = (lane % HEAD_DIM) < ROPE_HALF
    up = pltpu.roll(z, LANES - ROPE_HALF, axis=1)
    dn = pltpu.roll(z, ROPE_HALF, axis=1)
    return z * cos + jnp.where(first, up, dn) * sin


def _inproj_kernel(xp_ref, xs_ref, w_ref, b_ref, cos_ref, sin_ref,
                   q_ref, kf_ref, vf_ref, kb_ref, vt_ref, qi_ref, kiwi_ref, kib_ref, u_ref,
                   ktf_ref, vtf_ref, kitf_ref, *, conv_ch, tiles_p):
    x = jnp.where(pl.program_id(0) < tiles_p, xp_ref[...], xs_ref[...]).astype(BF16)
    cos = cos_ref[...]
    sin = sin_ref[...]

    def proj(lo, width):
        return jnp.dot(x, w_ref[:, lo:lo + width], preferred_element_type=F32) + b_ref[:, lo:lo + width]

    def rope(z):
        n = z.shape[1] // LANES
        slabs = [_rope_slab(z[:, s * LANES:(s + 1) * LANES], cos, sin) for s in range(n)]
        return slabs[0] if n == 1 else jnp.concatenate(slabs, axis=1)

    o = 0
    q = rope(proj(o, Q_W))
    q_ref[...] = (q * (ATTN_SCALE * LOG2E)).astype(BF16)
    o += Q_W
    k = rope(proj(o, KV_W))
    kf_ref[...] = k
    kb_ref[...] = k.astype(BF16)
    o += KV_W
    v = proj(o, KV_W)
    vf_ref[...] = v
    vt = v.T
    ones = jnp.ones((VT_ROWS - HEAD_DIM, vt.shape[1]), F32)
    vt_ref[0] = jnp.concatenate(
        [piece for g in range(N_KV_HEADS) for piece in (vt[g * HEAD_DIM:(g + 1) * HEAD_DIM], ones)],
        axis=0).astype(BF16)
    o += KV_W
    qi_ref[...] = rope(proj(o, QI_W)).astype(BF16)
    o += QI_W
    lane = lax.broadcasted_iota(I32, (1, LANES), 1)
    is_ki = lane < IDX_DIM
    kiwi = _rope_slab(proj(o, LANES), jnp.where(is_ki, cos, 1.0), jnp.where(is_ki, sin, 0.0))
    kiwi = kiwi * jnp.where((lane >= IDX_DIM) & (lane < IDX_DIM + IDX_HEADS), IDX_W_SCALE, 1.0)
    kiwi_ref[...] = kiwi
    kib_ref[...] = kiwi[:, :IDX_DIM].astype(BF16)

    @pl.when(pl.program_id(0) < tiles_p)
    def _():
        ktf_ref[0] = k.T
        vtf_ref[0] = vt
        kitf_ref[0] = kiwi.T[:IDX_DIM]
    o += LANES
    a = proj(o, conv_ch)
    g = proj(o + conv_ch, conv_ch)
    u_ref[...] = a * jax.nn.sigmoid(g)


def _inproj_call(x_p, x_s, w_pad, b_pad, cos_tab, sin_tab, tab_index, tm, seq_tiles):
    d = x_p.shape[1]
    tiles_p, tiles_s = x_p.shape[0] // tm, x_s.shape[0] // tm
    nt = tiles_p + tiles_s
    n = nt * tm
    nw = w_pad.shape[1]
    conv_ch = (nw - (Q_W + 2 * KV_W + QI_W + LANES)) // 2
    row = lambda i: (i, 0)
    const = lambda i: (0, 0)
    out_shapes = (
        jax.ShapeDtypeStruct((n, Q_W), BF16),
        jax.ShapeDtypeStruct((n, KV_W), F32),
        jax.ShapeDtypeStruct((n, KV_W), F32),
        jax.ShapeDtypeStruct((n, KV_W), BF16),
        jax.ShapeDtypeStruct((nt, N_KV_HEADS * VT_ROWS, tm), BF16),
        jax.ShapeDtypeStruct((n, QI_W), BF16),
        jax.ShapeDtypeStruct((n, LANES), F32),
        jax.ShapeDtypeStruct((n, IDX_DIM), BF16),
        jax.ShapeDtypeStruct((n, conv_ch), F32),
        jax.ShapeDtypeStruct((tiles_p // seq_tiles, KV_W, seq_tiles * tm), F32),
        jax.ShapeDtypeStruct((tiles_p // seq_tiles, KV_W, seq_tiles * tm), F32),
        jax.ShapeDtypeStruct((tiles_p // seq_tiles, IDX_DIM, seq_tiles * tm), F32),
    )
    last_p = tiles_p - 1
    seq_blk = lambda i: (jnp.minimum(i, last_p) // seq_tiles, 0, jnp.minimum(i, last_p) % seq_tiles)
    out_specs = (
        pl.BlockSpec((tm, Q_W), row),
        pl.BlockSpec((tm, KV_W), row),
        pl.BlockSpec((tm, KV_W), row),
        pl.BlockSpec((tm, KV_W), row),
        pl.BlockSpec((1, N_KV_HEADS * VT_ROWS, tm), lambda i: (i, 0, 0)),
        pl.BlockSpec((tm, QI_W), row),
        pl.BlockSpec((tm, LANES), row),
        pl.BlockSpec((tm, IDX_DIM), row),
        pl.BlockSpec((tm, conv_ch), row),
        pl.BlockSpec((1, KV_W, tm), seq_blk),
        pl.BlockSpec((1, KV_W, tm), seq_blk),
        pl.BlockSpec((1, IDX_DIM, tm), seq_blk),
    )
    return pl.pallas_call(
        functools.partial(_inproj_kernel, conv_ch=conv_ch, tiles_p=tiles_p),
        grid=(nt,),
        in_specs=[
            pl.BlockSpec((tm, d), lambda i: (jnp.minimum(i, tiles_p - 1), 0)),
            pl.BlockSpec((tm, d), lambda i: (jnp.maximum(i - tiles_p, 0), 0)),
            pl.BlockSpec((d, nw), const),
            pl.BlockSpec((1, nw), const),
            pl.BlockSpec((tm, LANES), lambda i: (tab_index(i), 0)),
            pl.BlockSpec((tm, LANES), lambda i: (tab_index(i), 0)),
        ],
        out_specs=out_specs,
        out_shape=out_shapes,
        compiler_params=_cparams(("arbitrary",)),
    )(x_p, x_s, w_pad, b_pad, cos_tab, sin_tab)


def _sortable(score):
    bits = lax.bitcast_convert_type(score, I32)
    key = bits ^ ((bits >> 31) & 0x7FFFFFFF)
    return jnp.where(score == 0.0, 0, key)


def _kth_largest_by_pairs(count_ge, shape, n_sel):
    def pair_body(t, thr):
        step = jnp.left_shift(jnp.int32(1), 30 - 2 * t)
        c1 = thr + step
        c2 = c1 + step
        c3 = c2 + step
        n1, n2, n3 = count_ge(c1), count_ge(c2), count_ge(c3)
        return jnp.where(n3 >= n_sel, c3, jnp.where(n2 >= n_sel, c2, jnp.where(n1 >= n_sel, c1, thr)))

    return lax.fori_loop(0, 16, pair_body, jnp.full(shape, INT_MIN, I32))


def _tie_bound(count_eq_below, need, shape, idx_bits):
    def bit_body(t, j0):
        cand = j0 + jnp.left_shift(jnp.int32(1), idx_bits - 1 - t)
        return jnp.where(count_eq_below(cand) < need, cand, j0)

    return lax.fori_loop(0, idx_bits, bit_body, jnp.zeros(shape, I32))


NO_BOUND = 2 ** 30
PLANE_KEYS = 32 * SUBLANES


def _bit_transpose32(rows):
    a = list(rows[::-1])
    j, m = 16, 0x0000FFFF
    while j:
        k = 0
        while k < 32:
            t = (a[k] ^ (a[k + j] >> j)) & m
            a[k] = a[k] ^ t
            a[k + j] = a[k + j] ^ (t << j)
            k = (k + j + 1) & ~j
        j >>= 1
        m = (m ^ (m << j)) & 0xFFFFFFFF
    return a[::-1]


def _plane_order(n):
    p = jnp.arange(n, dtype=I32)
    within = p % PLANE_KEYS
    return p - within + (within % SUBLANES) * 32 + within // SUBLANES


N_PAIRS = N_HEADS // 2
PAIRS_PER_GROUP = KV_GROUP // 2


def _prompt_attn_kernel(q_ref, qi_ref, kiwi_ref, kib_ref, koff_ref, kb_ref, vt_ref, o_ref,
                        qst_ref, qit_ref, pln_ref, cand_ref, top_ref, sel_ref, bias_ref, j0_ref,
                        lg0_ref, lg1_ref, mx0_ref, mx1_ref, pr0_ref, pr1_ref, al0_ref, al1_ref, m_ref, acc_ref,
                        *, qb, kc, n_sel, idx_bits):
    i = pl.program_id(1)
    n_chunks = (i * qb + qb + kc - 1) // kc
    q_pos = i * qb + lax.broadcasted_iota(I32, (1, qb), 1)

    def key_pos(k0):
        return k0 + lax.broadcasted_iota(I32, (kc, 1), 0)

    qt = q_ref[...].astype(F32).T
    zero = jnp.zeros((HEAD_DIM, 2 * qb), F32)
    for p in range(N_PAIRS):
        g = p // PAIRS_PER_GROUP
        pair = jnp.concatenate([qt[(2 * p) * HEAD_DIM:(2 * p + 1) * HEAD_DIM],
                                qt[(2 * p + 1) * HEAD_DIM:(2 * p + 2) * HEAD_DIM]], axis=1)
        qst_ref[p] = jnp.concatenate([pair if gg == g else zero for gg in range(N_KV_HEADS)], axis=0).astype(BF16)
    qit = qi_ref[...].astype(F32).T
    qit_ref[...] = jnp.concatenate(
        [qit[h * IDX_DIM:(h + 1) * IDX_DIM] for h in range(IDX_HEADS)], axis=1).astype(BF16)
    wit = kiwi_ref[...].T[IDX_DIM:IDX_DIM + SUBLANES]

    planes_per_chunk = kc // PLANE_KEYS

    @pl.when(i == 0)
    def _():
        pln_ref[...] = jnp.zeros(pln_ref.shape, I32)

    def score_body(c, carry):
        k0 = pl.multiple_of(c * kc, kc)
        s = jnp.dot(kib_ref[pl.ds(k0, kc), :], qit_ref[...], preferred_element_type=F32)
        score = jnp.zeros((kc, qb), F32)
        for h in range(IDX_HEADS):
            score = score + wit[h:h + 1, :] * jnp.maximum(s[:, h * qb:(h + 1) * qb], 0.0)
        score = jnp.where(k0 + koff_ref[...] <= q_pos, score, -jnp.inf)
        key = _sortable(score)
        for hb in range(planes_per_chunk):
            base = hb * PLANE_KEYS
            planes = _bit_transpose32([key[base + j * SUBLANES:base + (j + 1) * SUBLANES] for j in range(32)])
            planes[31] = ~planes[31]
            r0 = pl.multiple_of((c * planes_per_chunk + hb) * SUBLANES, SUBLANES)
            for b in range(32):
                pln_ref[b, pl.ds(r0, SUBLANES), :] = planes[b]
        return carry

    lax.fori_loop(0, n_chunks, score_body, 0)

    n_rows = pln_ref.shape[1]
    row = lax.broadcasted_iota(I32, (n_rows, 1), 0)
    cand_ref[...] = jnp.broadcast_to(jnp.where(row < n_chunks * (kc // 32), -1, 0), (n_rows, qb))
    top_ref[...] = jnp.zeros((n_rows, qb), I32)

    def bit_count(words):
        cnt = lax.population_count(words)
        return jnp.sum(jnp.sum(cnt.reshape(n_rows // SUBLANES, SUBLANES, qb), axis=0), axis=0, keepdims=True)

    def bit_body(tb, need):
        plane = pln_ref[31 - tb]
        cand = cand_ref[...]
        ones = cand & plane
        cnt = bit_count(ones)
        take = cnt >= need
        skip = jnp.where(take, 0, -1)
        top_ref[...] = top_ref[...] | (ones & skip)
        cand_ref[...] = cand & (plane ^ skip)
        return jnp.where(take, need, need - cnt)

    need = lax.fori_loop(0, 32, bit_body, jnp.full((1, qb), n_sel, I32))
    tie = bit_count(cand_ref[...]) > need

    def low_mask(bound):
        d = bound - row * 32
        return jnp.where(d >= 32, -1, jnp.left_shift(1, jnp.clip(d, 0, 31)) - 1)

    j0_ref[...] = jnp.full((1, qb), NO_BOUND, I32)

    @pl.when(jnp.max(jnp.where(tie, 1, 0)) > 0)
    def _():
        j0 = _tie_bound(lambda bound: bit_count(cand_ref[...] & low_mask(bound)), need, (1, qb), idx_bits)
        j0_ref[...] = jnp.where(tie, j0 + 1, NO_BOUND)

    sel_ref[...] = top_ref[...] | (cand_ref[...] & low_mask(j0_ref[...]))

    words_per_chunk = kc // 32

    def bias_body(c, carry):
        k0 = pl.multiple_of(c * kc, kc)
        words = sel_ref[pl.ds(pl.multiple_of(c * words_per_chunk, words_per_chunk), words_per_chunk), :]
        sub = lax.broadcasted_iota(I32, (SUBLANES, 1), 0)
        pieces = []
        for v in range(kc // SUBLANES):
            word = jnp.broadcast_to(words[v // 4:v // 4 + 1, :], (SUBLANES, qb))
            pieces.append(lax.shift_right_logical(word, (v % 4) * SUBLANES + sub) & 1)
        chosen = jnp.concatenate(pieces, axis=0) == 1
        bias_ref[pl.ds(k0, kc), :] = jnp.where(chosen & (key_pos(k0) <= q_pos), 0.0, NEG_BIG)
        return carry

    lax.fori_loop(0, n_chunks, bias_body, 0)

    m_ref[...] = jnp.full(m_ref.shape, NEG_BIG, F32)
    acc_ref[...] = jnp.zeros(acc_ref.shape, F32)
    lgs, mxs, prs, als = (lg0_ref, lg1_ref), (mx0_ref, mx1_ref), (pr0_ref, pr1_ref), (al0_ref, al1_ref)

    bias_ref[pl.ds(pl.multiple_of(n_chunks * kc, kc), kc), :] = jnp.full((kc, qb), NEG_BIG, F32)

    def logits(c, slot):
        k0 = pl.multiple_of(jnp.minimum(c, n_chunks - 1) * kc, kc)
        b0 = pl.multiple_of(jnp.minimum(c, n_chunks) * kc, kc)
        for p in range(N_PAIRS):
            lg = jnp.dot(kb_ref[pl.ds(k0, kc), :], qst_ref[p], preferred_element_type=F32)
            halves = [lg[:, h * qb:(h + 1) * qb] + bias_ref[pl.ds(b0, kc), :] for h in range(2)]
            for h in range(2):
                lgs[slot][p, :, h * qb:(h + 1) * qb] = halves[h]
            mxs[slot][p] = jnp.concatenate([jnp.max(x, axis=0, keepdims=True) for x in halves], axis=1)

    def probs(slot):
        for p in range(N_PAIRS):
            m_old = m_ref[p]
            m_new = jnp.maximum(m_old, mxs[slot][p])
            als[slot][p] = jnp.exp2(m_old - m_new)
            prs[slot][p] = jnp.exp2(lgs[slot][p] - m_new).astype(BF16)
            m_ref[p] = m_new

    def weighted(c, slot):
        cc = jnp.clip(c, 0, n_chunks - 1)
        for p in range(N_PAIRS):
            g = p // PAIRS_PER_GROUP
            vt = vt_ref[cc, g * VT_ROWS:(g + 1) * VT_ROWS, :]
            acc_ref[p] = als[slot][p] * acc_ref[p] + jnp.dot(vt, prs[slot][p], preferred_element_type=F32)

    logits(0, 0)
    pr1_ref[...] = jnp.zeros(pr1_ref.shape, BF16)
    al1_ref[...] = jnp.ones(al1_ref.shape, F32)

    def attn_body(j, carry):
        c = 2 * j
        logits(c + 1, 1)
        probs(0)
        weighted(c - 1, 1)
        logits(c + 2, 0)
        probs(1)
        weighted(c, 0)
        return carry

    lax.fori_loop(0, (n_chunks + 1) // 2, attn_body, 0)
    weighted(n_chunks - 1, 1)

    heads = []
    for p in range(N_PAIRS):
        acc = acc_ref[p]
        o = acc[:HEAD_DIM] * (1.0 / acc[HEAD_DIM:HEAD_DIM + 1])
        heads += [o[:, :qb], o[:, qb:]]
    o_ref[...] = jnp.concatenate(heads, axis=0).T.astype(o_ref.dtype)


def _prompt_attn_call(q, qi, kiwi, kib, kb, vt, nb, t, qb, kc):
    assert kc % PLANE_KEYS == 0 and t % kc == 0
    nqb = t // qb
    n_sel = min(INDEX_TOPK, t // 4)
    blk = lambda b, i: (b * nqb + i, 0)
    seq = lambda b, i: (b, 0)
    koff = jnp.broadcast_to(_plane_order(kc)[:, None], (kc, qb))
    kern = functools.partial(_prompt_attn_kernel, qb=qb, kc=kc, n_sel=n_sel, idx_bits=max(1, (t - 1).bit_length()))
    return pl.pallas_call(
        kern,
        grid=(nb, nqb),
        in_specs=[
            pl.BlockSpec((qb, Q_W), blk),
            pl.BlockSpec((qb, QI_W), blk),
            pl.BlockSpec((qb, LANES), blk),
            pl.BlockSpec((t, IDX_DIM), seq),
            pl.BlockSpec((kc, qb), lambda b, i: (0, 0)),
            pl.BlockSpec((t, KV_W), seq),
            pl.BlockSpec((t // kc, N_KV_HEADS * VT_ROWS, kc), lambda b, i: (b, 0, 0)),
        ],
        out_specs=pl.BlockSpec((qb, Q_W), blk),
        out_shape=jax.ShapeDtypeStruct((nb * t, Q_W), BF16),
        scratch_shapes=[
            pltpu.VMEM((N_PAIRS, KV_W, 2 * qb), BF16),
            pltpu.VMEM((IDX_DIM, IDX_HEADS * qb), BF16),
            pltpu.VMEM((32, t // 32, qb), I32),
            pltpu.VMEM((t // 32, qb), I32),
            pltpu.VMEM((t // 32, qb), I32),
            pltpu.VMEM((t // 32, qb), I32),
            pltpu.VMEM((t + kc, qb), F32),
            pltpu.VMEM((1, qb), I32),
            pltpu.VMEM((N_PAIRS, kc, 2 * qb), F32),
            pltpu.VMEM((N_PAIRS, kc, 2 * qb), F32),
            pltpu.VMEM((N_PAIRS, 1, 2 * qb), F32),
            pltpu.VMEM((N_PAIRS, 1, 2 * qb), F32),
            pltpu.VMEM((N_PAIRS, kc, 2 * qb), BF16),
            pltpu.VMEM((N_PAIRS, kc, 2 * qb), BF16),
            pltpu.VMEM((N_PAIRS, 1, 2 * qb), F32),
            pltpu.VMEM((N_PAIRS, 1, 2 * qb), F32),
            pltpu.VMEM((N_PAIRS, 1, 2 * qb), F32),
            pltpu.VMEM((N_PAIRS, VT_ROWS, 2 * qb), F32),
        ],
        compiler_params=_cparams(("parallel", "arbitrary")),
    )(q, qi, kiwi, kib, koff, kb, vt)


PAGES_PER_DOT = 4


def _sample_attn_kernel(pt_ref, q_ref, qi_ref, kiwi_ref, kn_ref, vn_ref, cki_hbm, ck_hbm, cv_hbm, o_ref,
                        kit_buf, kt_buf, vt_buf, sem, kin_s, knew_s, vnew_s, keys_ref, bias_ref, lg_ref, j0_ref,
                        *, n_pages, dq, n_sel, idx_bits):
    b = pl.program_id(0)
    nb = pl.num_programs(0)
    slot = b % 2
    past = n_pages * LANES
    lp = past + LANES
    streams = ((cki_hbm, kit_buf), (ck_hbm, kt_buf), (cv_hbm, vt_buf))

    def page_copy(a, seq, pg, sl):
        src, dst = streams[a]
        return pltpu.make_async_copy(src.at[pt_ref[seq, pg]], dst.at[sl, pg], sem.at[sl, a])

    def fetch(seq, sl):
        def body(pg, carry):
            for a in range(len(streams)):
                page_copy(a, seq, pg, sl).start()
            return carry

        lax.fori_loop(0, n_pages, body, 0)

    @pl.when(b == 0)
    def _():
        fetch(0, 0)

    @pl.when(b + 1 < nb)
    def _():
        fetch(b + 1, 1 - slot)

    def wait_body(pg, carry):
        for a in range(len(streams)):
            page_copy(a, b, pg, slot).wait()
        return carry

    lax.fori_loop(0, n_pages, wait_body, 0)

    def pages(buf, c):
        return jnp.concatenate([buf[slot, c * PAGES_PER_DOT + r] for r in range(PAGES_PER_DOT)], axis=1).astype(BF16)

    kc = PAGES_PER_DOT * LANES
    q = q_ref[0]
    qi = qi_ref[0]
    kiwi = kiwi_ref[0]
    qrow = lax.broadcasted_iota(I32, (dq, 1), 0)
    idx = lax.broadcasted_iota(I32, (1, lp), 1)
    rel = idx - past
    adm = (idx < past) | ((rel <= qrow) & (rel < dq))

    kin_s[...] = jnp.zeros(kin_s.shape, F32)
    knew_s[...] = jnp.zeros(knew_s.shape, F32)
    vnew_s[...] = jnp.zeros(vnew_s.shape, F32)
    kin_s[0:dq, :] = kiwi[:, :IDX_DIM]
    knew_s[0:dq, :] = kn_ref[0]
    vnew_s[0:dq, :] = vn_ref[0]

    nt = (((1,), (1,)), ((), ()))
    qi_rows = jnp.concatenate([qi[:, h * IDX_DIM:(h + 1) * IDX_DIM] for h in range(IDX_HEADS)], axis=0).astype(BF16)
    wi_col = jnp.concatenate([kiwi[:, IDX_DIM + h:IDX_DIM + h + 1] for h in range(IDX_HEADS)], axis=0)

    def scores(s):
        r = jnp.maximum(s, 0.0) * wi_col
        out = r[0:dq]
        for h in range(1, IDX_HEADS):
            out = out + r[h * dq:(h + 1) * dq]
        return out

    for c in range(n_pages // PAGES_PER_DOT):
        s = jnp.dot(qi_rows, pages(kit_buf, c), preferred_element_type=F32)
        keys_ref[:, c * kc:(c + 1) * kc] = _sortable(scores(s))
    s_new = scores(lax.dot_general(qi_rows, kin_s[...].astype(BF16), nt, preferred_element_type=F32))
    keys_ref[:, past:] = _sortable(jnp.where(adm[:, past:], s_new, -jnp.inf))

    def count(pred):
        return jnp.sum(jnp.where(pred(keys_ref[...]), 1, 0).astype(I32), axis=1, keepdims=True)

    thr = _kth_largest_by_pairs(lambda cand: count(lambda k: k >= cand), (dq, 1), n_sel)
    need = n_sel - count(lambda k: k > thr)
    tie = count(lambda k: k >= thr) > n_sel
    j0_ref[...] = jnp.full((dq, 1), NO_BOUND, I32)

    @pl.when(jnp.max(jnp.where(tie, 1, 0)) > 0)
    def _():
        j0 = _tie_bound(lambda cand: count(lambda k: (k == thr) & (idx < cand)), need, (dq, 1), idx_bits)
        j0_ref[...] = jnp.where(tie, j0, NO_BOUND)

    j0 = j0_ref[...]
    keys = keys_ref[...]
    sel = ((keys > thr) | ((keys == thr) & (idx <= j0))) & adm
    bias_ref[...] = jnp.where(sel, 0.0, NEG_BIG)

    zero = jnp.zeros((dq, HEAD_DIM), F32)
    rows = []
    for g in range(N_KV_HEADS):
        for r in range(KV_GROUP):
            h = g * KV_GROUP + r
            piece = q[:, h * HEAD_DIM:(h + 1) * HEAD_DIM]
            rows.append(jnp.concatenate([piece, zero] if g == 0 else [zero, piece], axis=1))
    qbd = jnp.concatenate(rows, axis=0).astype(BF16)

    for c in range(n_pages // PAGES_PER_DOT):
        lg_ref[:, c * kc:(c + 1) * kc] = jnp.dot(qbd, pages(kt_buf, c), preferred_element_type=F32)
    lg_ref[:, past:] = lax.dot_general(qbd, knew_s[...].astype(BF16), nt, preferred_element_type=F32)

    bias = bias_ref[...]
    lg = lg_ref[...] + jnp.concatenate([bias] * N_HEADS, axis=0)
    m = jnp.max(lg, axis=1, keepdims=True)
    p = jnp.exp2(lg - m)
    l = jnp.sum(p, axis=1, keepdims=True)
    lg_ref[...] = p
    o = jnp.dot(lg_ref[:, past:].astype(BF16), vnew_s[...].astype(BF16), preferred_element_type=F32)
    for c in range(n_pages // PAGES_PER_DOT):
        o = o + lax.dot_general(lg_ref[:, c * kc:(c + 1) * kc].astype(BF16), pages(vt_buf, c), nt,
                                preferred_element_type=F32)
    o_ref[0] = o * (1.0 / l)


def _sample_attn_call(page_table, q, qi, kiwi, kn, vn, cki_t, ck_t, cv_t):
    nb, dq, _ = q.shape
    n_pages = page_table.shape[1]
    assert cki_t.shape[2] == LANES and n_pages % PAGES_PER_DOT == 0
    lp = (n_pages + 1) * LANES
    n_sel = min(INDEX_TOPK, (n_pages * LANES + dq) // 4)
    b3 = lambda b, pt: (b, 0, 0)
    kern = functools.partial(_sample_attn_kernel, n_pages=n_pages, dq=dq, n_sel=n_sel,
                             idx_bits=max(1, (lp - 1).bit_length()))
    grid_spec = pltpu.PrefetchScalarGridSpec(
        num_scalar_prefetch=1,
        grid=(nb,),
        in_specs=[
            pl.BlockSpec((1, dq, Q_W), b3),
            pl.BlockSpec((1, dq, QI_W), b3),
            pl.BlockSpec((1, dq, LANES), b3),
            pl.BlockSpec((1, dq, KV_W), b3),
            pl.BlockSpec((1, dq, KV_W), b3),
            pl.BlockSpec(memory_space=pl.ANY),
            pl.BlockSpec(memory_space=pl.ANY),
            pl.BlockSpec(memory_space=pl.ANY),
        ],
        out_specs=pl.BlockSpec((1, N_HEADS * dq, KV_W), b3),
        scratch_shapes=[
            pltpu.VMEM((2, n_pages, IDX_DIM, LANES), F32),
            pltpu.VMEM((2, n_pages, KV_W, LANES), F32),
            pltpu.VMEM((2, n_pages, KV_W, LANES), F32),
            pltpu.SemaphoreType.DMA((2, 3)),
            pltpu.VMEM((LANES, IDX_DIM), F32),
            pltpu.VMEM((LANES, KV_W), F32),
            pltpu.VMEM((LANES, KV_W), F32),
            pltpu.VMEM((dq, lp), I32),
            pltpu.VMEM((dq, lp), F32),
            pltpu.VMEM((N_HEADS * dq, lp), F32),
            pltpu.VMEM((dq, 1), I32),
        ],
    )
    return pl.pallas_call(
        kern,
        grid_spec=grid_spec,
        out_shape=jax.ShapeDtypeStruct((nb, N_HEADS * dq, KV_W), F32),
        compiler_params=_cparams(("arbitrary",)),
    )(page_table, q, qi, kiwi, kn, vn, cki_t, ck_t, cv_t)


def _layer_norm(x, g, b):
    mu = jnp.mean(x, axis=-1, keepdims=True)
    xc = x - mu
    var = jnp.mean(xc * xc, axis=-1, keepdims=True)
    return xc * lax.rsqrt(var + LN_EPS) * g + b


HIST_ROWS = 32
HIST_PAD = HIST_ROWS - (CONV_WIDTH - 1)
CONV_ROWS = 64


def _mix_kernel(attn_ref, u_ref, uprev_ref, hist_ref, x_ref, cw_ref, cb_ref, cg_ref, cbb_ref, wo_ref, bo_ref,
                g1_ref, b1_ref, wr_ref, br_ref, h_ref, hb_ref, route_ref, xp_ref, xs_ref, conv_ref,
                *, tm, nseq, use_prev):
    rows = tm // nseq
    ch = u_ref.shape[1]

    if use_prev:
        first = pl.program_id(1) == 0

        @pl.when(first)
        def _():
            xp_ref[:, 0:HIST_ROWS, :] = hist_ref[...]

        @pl.when(jnp.logical_not(first))
        def _():
            xp_ref[0, 0:HIST_ROWS, :] = uprev_ref[...]
    else:
        xp_ref[:, 0:HIST_ROWS, :] = hist_ref[...]
    xp_ref[:, HIST_ROWS:, :] = u_ref[...].reshape(nseq, rows, ch)

    rs = min(rows, CONV_ROWS)
    span = HIST_ROWS + rows - SUBLANES

    def seq_body(s, carry):
        for r in range(1, SUBLANES):
            xs_ref[r - 1] = xp_ref[s, r:r + span, :]
        for r0 in range(0, rows, rs):
            acc = jnp.zeros((rs, ch), F32)
            for j in range(CONV_WIDTH):
                r = (HIST_PAD + j) % SUBLANES
                lo = r0 + HIST_PAD + j - r
                win = xp_ref[s, lo:lo + rs, :] if r == 0 else xs_ref[r - 1, lo:lo + rs, :]
                acc = acc + cw_ref[j:j + 1, :] * win
            conv_ref[pl.ds(pl.multiple_of(s * rows + r0, SUBLANES), rs), :] = acc
        return carry

    if nseq == 1:
        seq_body(0, 0)
    else:
        lax.fori_loop(0, nseq, seq_body, 0)

    y = _layer_norm(conv_ref[...] + cb_ref[...], cg_ref[...], cbb_ref[...])
    conv = (y * jax.nn.sigmoid(y)).astype(BF16)
    aw = attn_ref.shape[1]
    mixed = (jnp.dot(attn_ref[...].astype(BF16), wo_ref[0:aw, :], preferred_element_type=F32)
             + jnp.dot(conv, wo_ref[aw:, :], preferred_element_type=F32) + bo_ref[...])
    h = _layer_norm(DN_ALPHA * x_ref[...] + mixed, g1_ref[...], b1_ref[...])
    h_ref[...] = h
    hb = h.astype(BF16)
    hb_ref[...] = hb

    logits = jnp.dot(hb, wr_ref[...], preferred_element_type=F32) + br_ref[...]
    lane = lax.broadcasted_iota(I32, (1, LANES), 1)
    work = logits
    vals, ids = [], []
    for _ in range(TOP_K):
        mx = jnp.max(work, axis=1, keepdims=True)
        ix = jnp.min(jnp.where(work == mx, lane, LANES), axis=1, keepdims=True)
        vals.append(mx)
        ids.append(ix)
        work = jnp.where(lane == ix, -jnp.inf, work)
    ex = [jnp.exp(v - vals[0]) for v in vals]
    den = ex[0]
    for e in ex[1:]:
        den = den + e
    route = jnp.zeros((tm, LANES), F32)
    for k in range(TOP_K):
        route = jnp.where(lane == k, ex[k] / den, route)
        route = jnp.where(lane == TOP_K + k, ids[k].astype(F32), route)
    route_ref[...] = route


def _mix_call(attn, u, hist, x, weights, *, row_off, tm, nseq, grid):
    cw, cb, cg, cbb, wo, bo, g1, b1, wr, br = weights
    n_rows, d = x.shape
    ch = u.shape[1]
    use_prev = grid[1] > 1
    tiles_per_seq = grid[1]
    off_t = row_off // tm
    tile = lambda b, i: (b * tiles_per_seq + i, 0)
    otile = lambda b, i: (off_t + b * tiles_per_seq + i, 0)
    const = lambda b, i: (0, 0)
    if use_prev:
        per_tile = tm // HIST_ROWS
        prev = lambda b, i: (jnp.maximum((off_t + b * tiles_per_seq + i) * per_tile - 1, 0), 0)
    else:
        prev = const
    in_specs = [
        pl.BlockSpec((tm, attn.shape[1]), tile),
        pl.BlockSpec((tm, ch), otile),
        pl.BlockSpec((HIST_ROWS, ch), prev),
        pl.BlockSpec((nseq, HIST_ROWS, ch), lambda b, i: (b, 0, 0)),
        pl.BlockSpec((tm, d), tile),
        pl.BlockSpec(cw.shape, const),
        pl.BlockSpec(cb.shape, const),
        pl.BlockSpec(cg.shape, const),
        pl.BlockSpec(cbb.shape, const),
        pl.BlockSpec(wo.shape, const),
        pl.BlockSpec(bo.shape, const),
        pl.BlockSpec(g1.shape, const),
        pl.BlockSpec(b1.shape, const),
        pl.BlockSpec(wr.shape, const),
        pl.BlockSpec(br.shape, const),
    ]
    kern = functools.partial(_mix_kernel, tm=tm, nseq=nseq, use_prev=use_prev)
    return pl.pallas_call(
        kern,
        grid=grid,
        in_specs=in_specs,
        out_specs=(
            pl.BlockSpec((tm, d), tile),
            pl.BlockSpec((tm, d), tile),
            pl.BlockSpec((tm, LANES), tile),
        ),
        out_shape=(
            jax.ShapeDtypeStruct((n_rows, d), F32),
            jax.ShapeDtypeStruct((n_rows, d), BF16),
            jax.ShapeDtypeStruct((n_rows, LANES), F32),
        ),
        scratch_shapes=[
            pltpu.VMEM((nseq, HIST_ROWS + tm // nseq, ch), F32),
            pltpu.VMEM((SUBLANES - 1, HIST_ROWS + tm // nseq - SUBLANES, ch), F32),
            pltpu.VMEM((tm, ch), F32),
        ],
        compiler_params=_cparams(("parallel", "arbitrary")),
    )(attn, u, u, hist, x, cw, cb, cg, cbb, wo, bo, g1, b1, wr, br)


CAST_ROWS = 128


def _moe_kernel(be_ref, nu_ref, x_ref, wgu_ref, bgu_ref, wd_ref, bd_ref, y_ref, wgu_s, wd_s):
    i = pl.program_id(0)
    used = i < nu_ref[0]
    changed = (i == 0) | (be_ref[i] != be_ref[jnp.maximum(i - 1, 0)])
    dff = wd_s.shape[0]

    @pl.when(used & changed)
    def _():
        def cast_gu(r, carry):
            r0 = pl.multiple_of(r * CAST_ROWS, CAST_ROWS)
            wgu_s[pl.ds(r0, CAST_ROWS), :] = wgu_ref[0, pl.ds(r0, CAST_ROWS), :].astype(BF16)
            return carry

        def cast_d(r, carry):
            r0 = pl.multiple_of(r * CAST_ROWS, CAST_ROWS)
            wd_s[pl.ds(r0, CAST_ROWS), :] = wd_ref[0, pl.ds(r0, CAST_ROWS), :].astype(BF16)
            return carry

        lax.fori_loop(0, wgu_s.shape[0] // CAST_ROWS, cast_gu, 0)
        lax.fori_loop(0, wd_s.shape[0] // CAST_ROWS, cast_d, 0)

    @pl.when(used)
    def _():
        gu = jnp.dot(x_ref[...], wgu_s[...], preferred_element_type=F32) + bgu_ref[0]
        gate = jnp.minimum(gu[:, :dff], SWIGLU_LIMIT)
        up = jnp.clip(gu[:, dff:], -SWIGLU_LIMIT, SWIGLU_LIMIT)
        act = (up + 1.0) * gate * jax.nn.sigmoid(SWIGLU_ALPHA * gate)
        y = jnp.dot(act.astype(BF16), wd_s[...], preferred_element_type=F32) + bd_ref[0]
        y_ref[...] = y.astype(y_ref.dtype)

    @pl.when(jnp.logical_not(used))
    def _():
        y_ref[...] = jnp.zeros(y_ref.shape, y_ref.dtype)


def _moe_call(block_e, n_used, x_pad, wgu, bgu, wd, bd, bm):
    p, d = x_pad.shape
    ne, _, dgu = wgu.shape
    dff = wd.shape[1]
    n_blocks = p // bm
    grid_spec = pltpu.PrefetchScalarGridSpec(
        num_scalar_prefetch=2,
        grid=(n_blocks,),
        in_specs=[
            pl.BlockSpec((bm, d), lambda i, be, nu: (i, 0)),
            pl.BlockSpec((1, d, dgu), lambda i, be, nu: (be[i], 0, 0)),
            pl.BlockSpec((1, 1, dgu), lambda i, be, nu: (be[i], 0, 0)),
            pl.BlockSpec((1, dff, d), lambda i, be, nu: (be[i], 0, 0)),
            pl.BlockSpec((1, 1, d), lambda i, be, nu: (be[i], 0, 0)),
        ],
        out_specs=pl.BlockSpec((bm, d), lambda i, be, nu: (i, 0)),
        scratch_shapes=[pltpu.VMEM((d, dgu), BF16), pltpu.VMEM((dff, d), BF16)],
    )
    return pl.pallas_call(
        _moe_kernel,
        grid_spec=grid_spec,
        out_shape=jax.ShapeDtypeStruct((p, d), BF16),
        compiler_params=_cparams(("arbitrary",)),
    )(block_e, n_used, x_pad, wgu, bgu.reshape(ne, 1, dgu), wd, bd.reshape(ne, 1, d))


def _dispatch_plan(route, bm):
    n = route.shape[0]
    nk = n * TOP_K
    experts = jnp.arange(N_EXPERTS, dtype=I32)
    e_flat = route[:, TOP_K:2 * TOP_K].astype(I32).reshape(-1)
    iota = jnp.arange(nk, dtype=I32)
    e_sorted, order = lax.sort((e_flat, iota), num_keys=1, is_stable=True)
    _, inverse = lax.sort((order, iota), num_keys=1)
    counts = jnp.sum((e_flat[:, None] == experts[None, :]).astype(I32), axis=0)
    padded = (counts + bm - 1) // bm * bm
    start = jnp.cumsum(counts) - counts
    pend = jnp.cumsum(padded)
    pstart = pend - padded
    slot_sorted = pstart[e_sorted] + iota - start[e_sorted]
    slot_orig = slot_sorted[inverse]
    n_blocks = -(-(nk + N_EXPERTS * (bm - 1)) // bm)
    block_start = jnp.arange(n_blocks, dtype=I32) * bm
    block_e = jnp.minimum(jnp.sum((pend[None, :] <= block_start[:, None]).astype(I32), axis=1), N_EXPERTS - 1)
    n_used = (pend[-1:] // bm).astype(I32)
    e_slot = jnp.repeat(block_e, bm)
    rank = jnp.arange(n_blocks * bm, dtype=I32) - pstart[e_slot]
    src = jnp.clip(start[e_slot] + rank, 0, nk - 1)
    slot_tok = jnp.where(rank < counts[e_slot], order[src] // TOP_K, 0)
    return slot_tok, block_e, n_used, slot_orig


def _final_kernel(h_ref, yg_ref, route_ref, pe_ref, g2_ref, b2_ref, wg_ref, bg_ref, wp_ref, g3_ref, b3_ref, o_ref):
    route = route_ref[...]
    y = route[:, 0:1] * yg_ref[0].astype(F32)
    for k in range(1, TOP_K):
        y = y + route[:, k:k + 1] * yg_ref[k].astype(F32)
    h2 = _layer_norm(DN_ALPHA * h_ref[...] + y, g2_ref[...], b2_ref[...])
    gate = jax.nn.sigmoid(jnp.dot(h2.astype(BF16), wg_ref[...], preferred_element_type=F32) + bg_ref[...])
    proj = jnp.dot(pe_ref[...].astype(BF16), wp_ref[...], preferred_element_type=F32)
    o_ref[...] = _layer_norm(DN_ALPHA * h2 + gate * proj, g3_ref[...], b3_ref[...])


def _final_call(h, yg, route, pe, weights, tm):
    g2, b2, wg, bg, wp, g3, b3 = weights
    n, pd = pe.shape
    d = h.shape[1]
    row = lambda i: (i, 0)
    const = lambda i: (0, 0)
    return pl.pallas_call(
        _final_kernel,
        grid=(n // tm,),
        in_specs=[
            pl.BlockSpec((tm, d), row),
            pl.BlockSpec((TOP_K, tm, d), lambda i: (0, i, 0)),
            pl.BlockSpec((tm, LANES), row),
            pl.BlockSpec((tm, pd), row),
            pl.BlockSpec(g2.shape, const),
            pl.BlockSpec(b2.shape, const),
            pl.BlockSpec(wg.shape, const),
            pl.BlockSpec(bg.shape, const),
            pl.BlockSpec(wp.shape, const),
            pl.BlockSpec(g3.shape, const),
            pl.BlockSpec(b3.shape, const),
        ],
        out_specs=pl.BlockSpec((tm, d), row),
        out_shape=jax.ShapeDtypeStruct((n, d), F32),
        compiler_params=_cparams(("parallel",)),
    )(h, yg, route, pe, g2, b2, wg, bg, wp, g3, b3)


TOKEN_TILE = 512
QUERY_BLOCK = 128
MOE_ROWS = 512
MOE_ROWS_SAMPLE = 128


def _rope_tables(pos):
    rd = 2 * ROPE_HALF
    inv_freq = ROPE_THETA ** (-jnp.arange(ROPE_HALF, dtype=F32) * 2.0 / rd)
    ang = pos.astype(F32)[:, None] * inv_freq[None, :]
    cos, sin = jnp.cos(ang), jnp.sin(ang)
    ones = jnp.ones((pos.shape[0], HEAD_DIM - rd), F32)
    cos_h = jnp.concatenate([cos, cos, ones], axis=1)
    sin_h = jnp.concatenate([-sin, sin, 0.0 * ones], axis=1)
    reps = LANES // HEAD_DIM
    return jnp.tile(cos_h, (1, reps)), jnp.tile(sin_h, (1, reps))


def _layer(x_prompt, x_sample, p_prompt, p_sample, cache_k, cache_v, cache_kidx, state_conv, page_table,
           w_in, b_in, w_o, b_o, ln1_g, ln1_b, conv_w, conv_b, conv_ln_g, conv_ln_b,
           w_router, b_router, w_gate_up, b_gate_up, w_down, b_down, ln2_g, ln2_b,
           w_ple_gate, b_ple_gate, w_ple_proj, ln3_g, ln3_b):
    nb, t, d = x_prompt.shape
    db, dq, _ = x_sample.shape
    n_pages, page = page_table.shape[1], cache_k.shape[1]
    past = n_pages * page
    ch = conv_w.shape[1]
    tm = min(TOKEN_TILE, t)
    np_rows, ns_rows = nb * t, db * dq
    n = np_rows + ns_rows
    assert t % tm == 0 and ns_rows % tm == 0 and tm % dq == 0 and dq == SUBLANES
    row2 = lambda a: a.reshape(1, -1)

    n_head = Q_W + 2 * KV_W + QI_W + IDX_DIM + IDX_HEADS
    pad = LANES - IDX_DIM - IDX_HEADS
    w_pad = jnp.concatenate([w_in[:, :n_head], jnp.zeros((d, pad), F32), w_in[:, n_head:]], axis=1).astype(BF16)
    b_pad = jnp.concatenate([b_in[:n_head], jnp.zeros((pad,), F32), b_in[n_head:]]).reshape(1, -1)
    pos = jnp.concatenate([jnp.arange(t, dtype=I32), past + (jnp.arange(tm, dtype=I32) % dq)])
    cos_tab, sin_tab = _rope_tables(pos)
    tiles_p = np_rows // tm
    tab_index = lambda i: jnp.where(i < tiles_p, i % (t // tm), t // tm)
    q, k_f, v_f, k_b, v_t, qi, kiwi, ki_b, u, kt_p, vt_p, kit_p = _inproj_call(
        x_prompt.reshape(np_rows, d), x_sample.reshape(ns_rows, d), w_pad, b_pad, cos_tab, sin_tab, tab_index, tm,
        t // tm)

    ki_planes = ki_b[:np_rows].reshape(np_rows // PLANE_KEYS, SUBLANES, 32, IDX_DIM)
    ki_planes = ki_planes.transpose(0, 2, 1, 3).reshape(np_rows, IDX_DIM)
    attn_p = _prompt_attn_call(q, qi, kiwi, ki_planes, k_b, v_t, nb, t, min(QUERY_BLOCK, t), tm)

    wr_pad = jnp.concatenate([w_router, jnp.zeros((d, LANES - N_EXPERTS), F32)], axis=1).astype(BF16)
    br_pad = jnp.concatenate([b_router, jnp.full((LANES - N_EXPERTS,), -jnp.inf, F32)]).reshape(1, -1)
    mix_w = (conv_w, row2(conv_b), row2(conv_ln_g), row2(conv_ln_b), w_o.astype(BF16), row2(b_o),
             row2(ln1_g), row2(ln1_b), wr_pad, br_pad)
    hist_p = jnp.zeros((nb, HIST_ROWS, ch), F32)
    hist_s = jnp.concatenate([jnp.zeros((db, HIST_PAD, ch), F32), state_conv], axis=1)
    fin_w = (row2(ln2_g), row2(ln2_b), w_ple_gate.astype(BF16), row2(b_ple_gate), w_ple_proj.astype(BF16),
             row2(ln3_g), row2(ln3_b))

    def experts(h, h_b, route, bm):
        slot_tok, block_e, n_used, slot_orig = _dispatch_plan(route, bm)
        y_pad = _moe_call(block_e, n_used, h_b[slot_tok], w_gate_up, b_gate_up, w_down, b_down, bm)
        return y_pad[slot_orig.reshape(h.shape[0], TOP_K).T]

    h_p, hb_p, route_p = _mix_call(attn_p, u, hist_p, x_prompt.reshape(np_rows, d), mix_w, row_off=0, tm=tm,
                                   nseq=1, grid=(nb, t // tm))
    smp = lambda a: a[np_rows:].astype(F32).reshape(db, dq, -1)
    n_pool = cache_k.shape[0]
    kv_t = lambda pool: pool.transpose(0, 2, 3, 1).reshape(n_pool, KV_W, page)
    o_s = _sample_attn_call(page_table, smp(q), smp(qi), smp(kiwi), smp(k_f), smp(v_f),
                            cache_kidx.transpose(0, 2, 1), kv_t(cache_k), kv_t(cache_v))
    o_s = o_s.reshape(db, N_KV_HEADS, KV_GROUP, dq, N_KV_HEADS, HEAD_DIM)
    attn_s = jnp.stack([o_s[:, g, :, :, g, :] for g in range(N_KV_HEADS)], axis=1)
    attn_s = attn_s.transpose(0, 3, 1, 2, 4).reshape(ns_rows, Q_W)
    yg_p = experts(h_p, hb_p, route_p, MOE_ROWS)
    h_s, hb_s, route_s = _mix_call(attn_s, u, hist_s, x_sample.reshape(ns_rows, d), mix_w, row_off=np_rows, tm=tm,
                                   nseq=tm // dq, grid=(ns_rows // tm, 1))
    yg_s = experts(h_s, hb_s, route_s, MOE_ROWS_SAMPLE)
    y_p = _final_call(h_p, yg_p, route_p, p_prompt.reshape(np_rows, -1), fin_w, tm)
    y_s = _final_call(h_s, yg_s, route_s, p_sample.reshape(ns_rows, -1), fin_w, tm)

    u_p = u[:np_rows].reshape(nb, t, ch)
    u_s = u[np_rows:].reshape(db, dq, ch)
    keep = CONV_WIDTH - 1
    conv_p = jnp.concatenate([jnp.zeros((nb, keep, ch), F32), u_p], axis=1)[:, -keep:]
    conv_s = jnp.concatenate([state_conv, u_s], axis=1)[:, -keep:]
    kv = lambda a, lo, hi, b_, t_: a[lo:hi].reshape(b_, t_, N_KV_HEADS, HEAD_DIM)
    kv_t_out = lambda a: a.reshape(nb, N_KV_HEADS, HEAD_DIM, t).transpose(0, 3, 1, 2)
    return (y_p.reshape(nb, t, d), y_s.reshape(db, dq, d),
            kv_t_out(kt_p), kv_t_out(vt_p), kit_p.transpose(0, 2, 1), conv_p,
            kv(k_f, np_rows, n, db, dq), kv(v_f, np_rows, n, db, dq),
            kiwi[np_rows:, :IDX_DIM].reshape(db, dq, IDX_DIM), conv_s)


def kernel(x_prompt, x_sample, p_prompt, p_sample, cache_k, cache_v, cache_kidx, state_conv, page_table, w_in, b_in, w_o, b_o, ln1_g, ln1_b, conv_w, conv_b, conv_ln_g, conv_ln_b, w_router, b_router, w_gate_up, b_gate_up, w_down, b_down, ln2_g, ln2_b, w_ple_gate, b_ple_gate, w_ple_proj, ln3_g, ln3_b):
    assert w_in.shape[0] == DEPTH
    outs = _layer(x_prompt, x_sample, p_prompt[0], p_sample[0], cache_k[0], cache_v[0], cache_kidx[0],
                  state_conv[0], page_table, w_in[0], b_in[0], w_o[0], b_o[0], ln1_g[0], ln1_b[0],
                  conv_w[0], conv_b[0], conv_ln_g[0], conv_ln_b[0], w_router[0], b_router[0],
                  w_gate_up[0], b_gate_up[0], w_down[0], b_down[0], ln2_g[0], ln2_b[0],
                  w_ple_gate[0], b_ple_gate[0], w_ple_proj[0], ln3_g[0], ln3_b[0])
    y_p, y_s = outs[0], outs[1]
    return (y_p, y_s) + tuple(o[None] for o in outs[2:])
```
